```python
import jax, jax.numpy as jnp
from jax import lax
import numpy as np

D_MODEL = 1024
BATCH = 8
SEQ = 16384
DEPTH = 4

CONV_DIM = 512
CONV_WIDTH = 31
SC_DIM = 512
SC_WIDTH = 3
MLA_HEADS = 8
QK_NOPE_DIM = 64
QK_ROPE_DIM = 32
V_HEAD_DIM = 64
Q_LORA_RANK = 256
KV_LORA_RANK = 128
ROPE_THETA = 10000.0
ATTN_BLOCK = 128
POOL_WINDOWS = (2, 4, 8, 16)
POOL_GROUPS = 4
POOL_DIM = 512
POOL_GROUP_DIM = POOL_DIM // POOL_GROUPS
N_BRANCHES = 4
D_FF = 2816
FFN_CONV_WIDTH = 3
LN_EPS = 1e-5
RMS_EPS = 1e-6
DEEPNORM_ALPHA = (2.0 * DEPTH) ** 0.25
DEEPNORM_BETA = (8.0 * DEPTH) ** -0.25

OFF_CONV_A = CONV_DIM
OFF_CONV_B = OFF_CONV_A + CONV_DIM
OFF_SC_B = OFF_CONV_B + SC_DIM
OFF_SC_C = OFF_SC_B + SC_DIM
OFF_SC_X = OFF_SC_C + SC_DIM
OFF_Q_LAT = OFF_SC_X + Q_LORA_RANK
OFF_KV_LAT = OFF_Q_LAT + KV_LORA_RANK
OFF_K_ROPE = OFF_KV_LAT + QK_ROPE_DIM
OFF_POOL = OFF_K_ROPE + POOL_DIM
IN_COLS = OFF_POOL + N_BRANCHES * D_MODEL
IN_OFFSETS = (OFF_CONV_A, OFF_CONV_B, OFF_SC_B, OFF_SC_C, OFF_SC_X, OFF_Q_LAT, OFF_KV_LAT, OFF_K_ROPE, OFF_POOL)

kernel_name = 'hybrid_gated_conv_mla_pool_deepnorm'


def layer_norm(x, g, b):
    xf = x.astype(jnp.float32)
    mu = jnp.mean(xf, axis=-1, keepdims=True)
    var = jnp.mean(jnp.square(xf - mu), axis=-1, keepdims=True)
    y = (xf - mu) * lax.rsqrt(var + LN_EPS)
    return (y * g.astype(jnp.float32) + b.astype(jnp.float32)).astype(x.dtype)


def rms_norm(x, g):
    xf = x.astype(jnp.float32)
    y = xf * lax.rsqrt(jnp.mean(jnp.square(xf), axis=-1, keepdims=True) + RMS_EPS)
    return (y * g.astype(jnp.float32)).astype(x.dtype)


def causal_dwconv(x, w):
    k, c = w.shape
    return lax.conv_general_dilated(
        x, w[:, None, :].astype(x.dtype), window_strides=(1,), padding=[(k - 1, 0)],
        dimension_numbers=('NWC', 'WIO', 'NWC'), feature_group_count=c)


def rope_tables(positions):
    inv = 1.0 / (ROPE_THETA ** (jnp.arange(0, QK_ROPE_DIM, 2, dtype=jnp.float32) / QK_ROPE_DIM))
    ang = positions.astype(jnp.float32)[..., None] * inv
    return jnp.cos(ang), jnp.sin(ang)


def apply_rope(x, cos, sin):
    half = QK_ROPE_DIM // 2
    xf = x.astype(jnp.float32)
    x1, x2 = xf[..., :half], xf[..., half:]
    return jnp.concatenate([x1 * cos - x2 * sin, x2 * cos + x1 * sin], axis=-1).astype(x.dtype)


def mla_attention(q_nope, q_rope, k_nope, k_rope, v):
    b, s, h, _ = q_nope.shape
    nb = s // ATTN_BLOCK
    scale = (QK_NOPE_DIM + QK_ROPE_DIM) ** -0.5
    key_idx = jnp.arange(s)

    def to_blocks(t):
        return jnp.moveaxis(t.reshape(b, nb, ATTN_BLOCK, *t.shape[2:]), 1, 0)

    def one_block(args):
        qn, qr, blk = args
        sc = (jnp.einsum('bqhd,bkhd->bhqk', qn, k_nope, preferred_element_type=jnp.float32)
              + jnp.einsum('bqhd,bkd->bhqk', qr, k_rope, preferred_element_type=jnp.float32))
        q_idx = blk * ATTN_BLOCK + jnp.arange(ATTN_BLOCK)
        mask = key_idx[None, :] <= q_idx[:, None]
        p = jax.nn.softmax(jnp.where(mask, sc * scale, -jnp.inf), axis=-1)
        return jnp.einsum('bhqk,bkhd->bqhd', p.astype(v.dtype), v)

    out = lax.map(one_block, (to_blocks(q_nope), to_blocks(q_rope), jnp.arange(nb)))
    return jnp.moveaxis(out, 0, 1).reshape(b, s, h * V_HEAD_DIM)


def multiscale_pool(u):
    s = u.shape[1]
    uf = u.astype(jnp.float32)
    cs = jnp.cumsum(uf, axis=1)
    t = jnp.arange(s)
    means = []
    for g, w in enumerate(POOL_WINDOWS):
        cg = cs[..., g * POOL_GROUP_DIM:(g + 1) * POOL_GROUP_DIM]
        prev = jnp.pad(cg, ((0, 0), (w, 0), (0, 0)))[:, :s]
        cnt = jnp.minimum(t + 1, w).astype(jnp.float32)[None, :, None]
        means.append((cg - prev) / cnt)
    return (jnp.concatenate(means, axis=-1) - uf).astype(u.dtype)


def _fwd_setup_inputs(seed: int = 0) -> dict:
    key = jax.random.key(seed)
    ks = iter(jax.random.split(key, 40))
    L, D = DEPTH, D_MODEL
    beta = DEEPNORM_BETA

    def nrm(shape, scale):
        return scale * jax.random.normal(next(ks), shape, jnp.float32)

    x = nrm((BATCH, SEQ, D), 1.0)
    c = nrm((BATCH, D), 1.0)
    offsets = jax.random.randint(next(ks), (BATCH, 1), 0, 4096, dtype=jnp.int32)
    positions = offsets + jnp.arange(SEQ, dtype=jnp.int32)[None, :]
    return {
        'x': x,
        'c': c,
        'positions': positions,
        'w_ada': nrm((L, D, 6 * D), 0.1 * D ** -0.5),
        'b_ada': nrm((L, 6 * D), 0.02),
        'w_in': nrm((L, D, IN_COLS), D ** -0.5),
        'b_in': nrm((L, IN_COLS), 0.02),
        'conv_dw': nrm((L, CONV_WIDTH, CONV_DIM), CONV_WIDTH ** -0.5),
        'conv_ln_g': 1.0 + nrm((L, CONV_DIM), 0.1),
        'conv_ln_b': nrm((L, CONV_DIM), 0.02),
        'w_conv_out': nrm((L, CONV_DIM, D), beta * CONV_DIM ** -0.5),
        'sc_dw': nrm((L, SC_WIDTH, SC_DIM), SC_WIDTH ** -0.5),
        'w_sc_out': nrm((L, SC_DIM, D), beta * SC_DIM ** -0.5),
        'q_norm_g': 1.0 + nrm((L, Q_LORA_RANK), 0.1),
        'w_uq': nrm((L, Q_LORA_RANK, MLA_HEADS * (QK_NOPE_DIM + QK_ROPE_DIM)), Q_LORA_RANK ** -0.5),
        'kv_norm_g': 1.0 + nrm((L, KV_LORA_RANK), 0.1),
        'w_ukv': nrm((L, KV_LORA_RANK, MLA_HEADS * (QK_NOPE_DIM + V_HEAD_DIM)), KV_LORA_RANK ** -0.5),
        'w_mla_out': nrm((L, MLA_HEADS * V_HEAD_DIM, D), beta * (MLA_HEADS * V_HEAD_DIM) ** -0.5),
        'w_pool': nrm((L, POOL_GROUPS, POOL_GROUP_DIM, POOL_GROUP_DIM), POOL_GROUP_DIM ** -0.5),
        'pool_scale': 1.0 + nrm((L, POOL_DIM), 0.1),
        'w_pool_out': nrm((L, POOL_DIM, D), beta * POOL_DIM ** -0.5),
        'w_o': nrm((L, D, D), beta * D ** -0.5),
        'ln1_g': 1.0 + nrm((L, D), 0.1),
        'ln1_b': nrm((L, D), 0.02),
        'w_up': nrm((L, D, 2 * D_FF), D ** -0.5),
        'ffn_dw': nrm((L, FFN_CONV_WIDTH, 2 * D_FF), FFN_CONV_WIDTH ** -0.5),
        'w_down': nrm((L, D_FF, D), beta * D_FF ** -0.5),
        'ln2_g': 1.0 + nrm((L, D), 0.1),
        'ln2_b': nrm((L, D), 0.02),
    }


def _fwd_reference(x, c, positions, w_ada, b_ada, w_in, b_in, conv_dw, conv_ln_g, conv_ln_b, w_conv_out,
              sc_dw, w_sc_out, q_norm_g, w_uq, kv_norm_g, w_ukv, w_mla_out, w_pool, pool_scale,
              w_pool_out, w_o, ln1_g, ln1_b, w_up, ffn_dw, w_down, ln2_g, ln2_b):
    b, s, d = x.shape
    cos, sin = rope_tables(positions)
    c_act = jax.nn.silu(c)
    for l in range(DEPTH):
        mod = (c_act @ w_ada[l] + b_ada[l])[:, None, :]
        sh1, sc1, g1, sh2, sc2, g2 = jnp.split(mod, 6, axis=-1)

        h = x * (1.0 + sc1) + sh1
        proj = h @ w_in[l] + b_in[l]
        conv_a, conv_b, sc_bg, sc_cg, sc_x, q_lat, kv_lat, k_rope, pool_u, gates = jnp.split(
            proj, IN_OFFSETS, axis=-1)

        ya = conv_a * jax.nn.sigmoid(conv_b)
        ya = causal_dwconv(ya, conv_dw[l])
        ya = jax.nn.silu(layer_norm(ya, conv_ln_g[l], conv_ln_b[l]))
        ya = ya @ w_conv_out[l]

        yb = (sc_bg * causal_dwconv(sc_cg * sc_x, sc_dw[l])) @ w_sc_out[l]

        q = (rms_norm(q_lat, q_norm_g[l]) @ w_uq[l]).reshape(b, s, MLA_HEADS, QK_NOPE_DIM + QK_ROPE_DIM)
        q_nope = q[..., :QK_NOPE_DIM]
        q_rope = apply_rope(q[..., QK_NOPE_DIM:], cos[:, :, None, :], sin[:, :, None, :])
        kv = (rms_norm(kv_lat, kv_norm_g[l]) @ w_ukv[l]).reshape(b, s, MLA_HEADS, QK_NOPE_DIM + V_HEAD_DIM)
        k_nope, v = kv[..., :QK_NOPE_DIM], kv[..., QK_NOPE_DIM:]
        k_rope_r = apply_rope(k_rope, cos, sin)
        yc = mla_attention(q_nope, q_rope, k_nope, k_rope_r, v) @ w_mla_out[l]

        pd = multiscale_pool(pool_u).reshape(b, s, POOL_GROUPS, POOL_GROUP_DIM)
        yd = jnp.einsum('bsgc,gcd->bsgd', pd, w_pool[l]).reshape(b, s, POOL_DIM) * pool_scale[l]
        yd = yd @ w_pool_out[l]

        gt = jax.nn.sigmoid(gates.astype(jnp.float32)).astype(x.dtype).reshape(b, s, N_BRANCHES, d)
        merged = gt[:, :, 0] * ya + gt[:, :, 1] * yb + gt[:, :, 2] * yc + gt[:, :, 3] * yd
        mix = merged @ w_o[l]
        x = layer_norm(DEEPNORM_ALPHA * x + (1.0 + g1) * mix, ln1_g[l], ln1_b[l])

        h = x * (1.0 + sc2) + sh2
        up = causal_dwconv(h @ w_up[l], ffn_dw[l])
        val, gate = jnp.split(up, 2, axis=-1)
        ffn = (jax.nn.silu(gate) * val) @ w_down[l]
        x = layer_norm(DEEPNORM_ALPHA * x + (1.0 + g2) * ffn, ln2_g[l], ln2_b[l])
    return x


import jax as _jax
import jax.numpy as _jnp

TWIN_FORMAT = 'train_step'
FWD_PARAMS = ['x', 'c', 'positions', 'w_ada', 'b_ada', 'w_in', 'b_in', 'conv_dw', 'conv_ln_g', 'conv_ln_b', 'w_conv_out', 'sc_dw', 'w_sc_out', 'q_norm_g', 'w_uq', 'kv_norm_g', 'w_ukv', 'w_mla_out', 'w_pool', 'pool_scale', 'w_pool_out', 'w_o', 'ln1_g', 'ln1_b', 'w_up', 'ffn_dw', 'w_down', 'ln2_g', 'ln2_b']
TWIN_WEIGHTS = ['w_ada', 'b_ada', 'w_in', 'b_in', 'conv_dw', 'conv_ln_g', 'conv_ln_b', 'w_conv_out', 'sc_dw', 'w_sc_out', 'q_norm_g', 'w_uq', 'kv_norm_g', 'w_ukv', 'w_mla_out', 'w_pool', 'pool_scale', 'w_pool_out', 'w_o', 'ln1_g', 'ln1_b', 'w_up', 'ffn_dw', 'w_down', 'ln2_g', 'ln2_b']
TWIN_DIFF_INPUT = 'x'
TWIN_INPUTS = ['x', 'c', 'positions', 'w_ada', 'b_ada', 'w_in', 'b_in', 'conv_dw', 'conv_ln_g', 'conv_ln_b', 'w_conv_out', 'sc_dw', 'w_sc_out', 'q_norm_g', 'w_uq', 'kv_norm_g', 'w_ukv', 'w_mla_out', 'w_pool', 'pool_scale', 'w_pool_out', 'w_o', 'ln1_g', 'ln1_b', 'w_up', 'ffn_dw', 'w_down', 'ln2_g', 'ln2_b', 'loss_target', 'm_w_ada', 'm_b_ada', 'm_w_in', 'm_b_in', 'm_conv_dw', 'm_conv_ln_g', 'm_conv_ln_b', 'm_w_conv_out', 'm_sc_dw', 'm_w_sc_out', 'm_q_norm_g', 'm_w_uq', 'm_kv_norm_g', 'm_w_ukv', 'm_w_mla_out', 'm_w_pool', 'm_pool_scale', 'm_w_pool_out', 'm_w_o', 'm_ln1_g', 'm_ln1_b', 'm_w_up', 'm_ffn_dw', 'm_w_down', 'm_ln2_g', 'm_ln2_b', 'v_w_ada', 'v_b_ada', 'v_w_in', 'v_b_in', 'v_conv_dw', 'v_conv_ln_g', 'v_conv_ln_b', 'v_w_conv_out', 'v_sc_dw', 'v_w_sc_out', 'v_q_norm_g', 'v_w_uq', 'v_kv_norm_g', 'v_w_ukv', 'v_w_mla_out', 'v_w_pool', 'v_pool_scale', 'v_w_pool_out', 'v_w_o', 'v_ln1_g', 'v_ln1_b', 'v_w_up', 'v_ffn_dw', 'v_w_down', 'v_ln2_g', 'v_ln2_b']
TWIN_OUTPUTS = ['loss', 'grad_x', 'grad_w_ada', 'grad_b_ada', 'grad_w_in', 'grad_b_in', 'grad_conv_dw', 'grad_conv_ln_g', 'grad_conv_ln_b', 'grad_w_conv_out', 'grad_sc_dw', 'grad_w_sc_out', 'grad_q_norm_g', 'grad_w_uq', 'grad_kv_norm_g', 'grad_w_ukv', 'grad_w_mla_out', 'grad_w_pool', 'grad_pool_scale', 'grad_w_pool_out', 'grad_w_o', 'grad_ln1_g', 'grad_ln1_b', 'grad_w_up', 'grad_ffn_dw', 'grad_w_down', 'grad_ln2_g', 'grad_ln2_b', 'delta_w_ada', 'delta_b_ada', 'delta_w_in', 'delta_b_in', 'delta_conv_dw', 'delta_conv_ln_g', 'delta_conv_ln_b', 'delta_w_conv_out', 'delta_sc_dw', 'delta_w_sc_out', 'delta_q_norm_g', 'delta_w_uq', 'delta_kv_norm_g', 'delta_w_ukv', 'delta_w_mla_out', 'delta_w_pool', 'delta_pool_scale', 'delta_w_pool_out', 'delta_w_o', 'delta_ln1_g', 'delta_ln1_b', 'delta_w_up', 'delta_ffn_dw', 'delta_w_down', 'delta_ln2_g', 'delta_ln2_b', 'new_m_w_ada', 'new_m_b_ada', 'new_m_w_in', 'new_m_b_in', 'new_m_conv_dw', 'new_m_conv_ln_g', 'new_m_conv_ln_b', 'new_m_w_conv_out', 'new_m_sc_dw', 'new_m_w_sc_out', 'new_m_q_norm_g', 'new_m_w_uq', 'new_m_kv_norm_g', 'new_m_w_ukv', 'new_m_w_mla_out', 'new_m_w_pool', 'new_m_pool_scale', 'new_m_w_pool_out', 'new_m_w_o', 'new_m_ln1_g', 'new_m_ln1_b', 'new_m_w_up', 'new_m_ffn_dw', 'new_m_w_down', 'new_m_ln2_g', 'new_m_ln2_b', 'new_v_w_ada', 'new_v_b_ada', 'new_v_w_in', 'new_v_b_in', 'new_v_conv_dw', 'new_v_conv_ln_g', 'new_v_conv_ln_b', 'new_v_w_conv_out', 'new_v_sc_dw', 'new_v_w_sc_out', 'new_v_q_norm_g', 'new_v_w_uq', 'new_v_kv_norm_g', 'new_v_w_ukv', 'new_v_w_mla_out', 'new_v_w_pool', 'new_v_pool_scale', 'new_v_w_pool_out', 'new_v_w_o', 'new_v_ln1_g', 'new_v_ln1_b', 'new_v_w_up', 'new_v_ffn_dw', 'new_v_w_down', 'new_v_ln2_g', 'new_v_ln2_b']
TWIN_LEAF_KINDS = {'loss': 'loss', 'grad_x': 'grad_x', 'grad_w_ada': 'grad_w', 'grad_b_ada': 'grad_w', 'grad_w_in': 'grad_w', 'grad_b_in': 'grad_w', 'grad_conv_dw': 'grad_w', 'grad_conv_ln_g': 'grad_w', 'grad_conv_ln_b': 'grad_w', 'grad_w_conv_out': 'grad_w', 'grad_sc_dw': 'grad_w', 'grad_w_sc_out': 'grad_w', 'grad_q_norm_g': 'grad_w', 'grad_w_uq': 'grad_w', 'grad_kv_norm_g': 'grad_w', 'grad_w_ukv': 'grad_w', 'grad_w_mla_out': 'grad_w', 'grad_w_pool': 'grad_w', 'grad_pool_scale': 'grad_w', 'grad_w_pool_out': 'grad_w', 'grad_w_o': 'grad_w', 'grad_ln1_g': 'grad_w', 'grad_ln1_b': 'grad_w', 'grad_w_up': 'grad_w', 'grad_ffn_dw': 'grad_w', 'grad_w_down': 'grad_w', 'grad_ln2_g': 'grad_w', 'grad_ln2_b': 'grad_w', 'delta_w_ada': 'delta_w', 'delta_b_ada': 'delta_w', 'delta_w_in': 'delta_w', 'delta_b_in': 'delta_w', 'delta_conv_dw': 'delta_w', 'delta_conv_ln_g': 'delta_w', 'delta_conv_ln_b': 'delta_w', 'delta_w_conv_out': 'delta_w', 'delta_sc_dw': 'delta_w', 'delta_w_sc_out': 'delta_w', 'delta_q_norm_g': 'delta_w', 'delta_w_uq': 'delta_w', 'delta_kv_norm_g': 'delta_w', 'delta_w_ukv': 'delta_w', 'delta_w_mla_out': 'delta_w', 'delta_w_pool': 'delta_w', 'delta_pool_scale': 'delta_w', 'delta_w_pool_out': 'delta_w', 'delta_w_o': 'delta_w', 'delta_ln1_g': 'delta_w', 'delta_ln1_b': 'delta_w', 'delta_w_up': 'delta_w', 'delta_ffn_dw': 'delta_w', 'delta_w_down': 'delta_w', 'delta_ln2_g': 'delta_w', 'delta_ln2_b': 'delta_w', 'new_m_w_ada': 'new_m', 'new_m_b_ada': 'new_m', 'new_m_w_in': 'new_m', 'new_m_b_in': 'new_m', 'new_m_conv_dw': 'new_m', 'new_m_conv_ln_g': 'new_m', 'new_m_conv_ln_b': 'new_m', 'new_m_w_conv_out': 'new_m', 'new_m_sc_dw': 'new_m', 'new_m_w_sc_out': 'new_m', 'new_m_q_norm_g': 'new_m', 'new_m_w_uq': 'new_m', 'new_m_kv_norm_g': 'new_m', 'new_m_w_ukv': 'new_m', 'new_m_w_mla_out': 'new_m', 'new_m_w_pool': 'new_m', 'new_m_pool_scale': 'new_m', 'new_m_w_pool_out': 'new_m', 'new_m_w_o': 'new_m', 'new_m_ln1_g': 'new_m', 'new_m_ln1_b': 'new_m', 'new_m_w_up': 'new_m', 'new_m_ffn_dw': 'new_m', 'new_m_w_down': 'new_m', 'new_m_ln2_g': 'new_m', 'new_m_ln2_b': 'new_m', 'new_v_w_ada': 'new_v', 'new_v_b_ada': 'new_v', 'new_v_w_in': 'new_v', 'new_v_b_in': 'new_v', 'new_v_conv_dw': 'new_v', 'new_v_conv_ln_g': 'new_v', 'new_v_conv_ln_b': 'new_v', 'new_v_w_conv_out': 'new_v', 'new_v_sc_dw': 'new_v', 'new_v_w_sc_out': 'new_v', 'new_v_q_norm_g': 'new_v', 'new_v_w_uq': 'new_v', 'new_v_kv_norm_g': 'new_v', 'new_v_w_ukv': 'new_v', 'new_v_w_mla_out': 'new_v', 'new_v_w_pool': 'new_v', 'new_v_pool_scale': 'new_v', 'new_v_w_pool_out': 'new_v', 'new_v_w_o': 'new_v', 'new_v_ln1_g': 'new_v', 'new_v_ln1_b': 'new_v', 'new_v_w_up': 'new_v', 'new_v_ffn_dw': 'new_v', 'new_v_w_down': 'new_v', 'new_v_ln2_g': 'new_v', 'new_v_ln2_b': 'new_v'}


def _forward(args):
    return _fwd_reference(*[args[k] for k in FWD_PARAMS])


def _output_shape():
    def fwd():
        inp = _fwd_setup_inputs(0)
        return _fwd_reference(*[inp[k] for k in FWD_PARAMS])
    out = _jax.eval_shape(fwd)
    return out.shape, out.dtype

N_MICROBATCH = 1
ADAM_LR = 0.001
ADAM_B1 = 0.9
ADAM_B2 = 0.999
ADAM_EPS = 1e-08
ADAM_WD = 0.01
ADAM_STEP = 10
PER_EXAMPLE_BATCH_AXIS = {'x': 0, 'c': 0, 'positions': 0, 'loss_target': 0}
SHARED_INPUTS = []
_WEIGHT_DTYPES = {'w_ada': _jnp.float32, 'b_ada': _jnp.float32, 'w_in': _jnp.float32, 'b_in': _jnp.float32, 'conv_dw': _jnp.float32, 'conv_ln_g': _jnp.float32, 'conv_ln_b': _jnp.float32, 'w_conv_out': _jnp.float32, 'sc_dw': _jnp.float32, 'w_sc_out': _jnp.float32, 'q_norm_g': _jnp.float32, 'w_uq': _jnp.float32, 'kv_norm_g': _jnp.float32, 'w_ukv': _jnp.float32, 'w_mla_out': _jnp.float32, 'w_pool': _jnp.float32, 'pool_scale': _jnp.float32, 'w_pool_out': _jnp.float32, 'w_o': _jnp.float32, 'ln1_g': _jnp.float32, 'ln1_b': _jnp.float32, 'w_up': _jnp.float32, 'ffn_dw': _jnp.float32, 'w_down': _jnp.float32, 'ln2_g': _jnp.float32, 'ln2_b': _jnp.float32}
MOMENT_SCALE = {'w_ada': 5.147302e-02, 'b_ada': 1.688335e-01, 'w_in': 1.794239e-02, 'b_in': 3.630492e-02, 'conv_dw': 2.015917e-02, 'conv_ln_g': 4.538671e-02, 'conv_ln_b': 7.068091e-02, 'w_conv_out': 4.639387e-02, 'sc_dw': 3.209984e-02, 'w_sc_out': 5.309013e-02, 'q_norm_g': 8.302174e-03, 'w_uq': 4.775195e-03, 'kv_norm_g': 1.780993e-02, 'w_ukv': 6.356240e-03, 'w_mla_out': 1.259780e-02, 'w_pool': 3.444334e-02, 'pool_scale': 3.659719e-02, 'w_pool_out': 5.871272e-02, 'w_o': 9.041654e-02, 'ln1_g': 2.694305e+01, 'ln1_b': 2.066341e+00, 'w_up': 3.587313e-02, 'ffn_dw': 3.588686e-02, 'w_down': 1.424367e-01, 'ln2_g': 7.662072e+01, 'ln2_b': 3.713658e+00}


def _to_microbatches(a, axis):
    t = _jnp.moveaxis(a, axis, 0)
    t = t.reshape((N_MICROBATCH, t.shape[0] // N_MICROBATCH) + t.shape[1:])
    return _jnp.moveaxis(t, 1, axis + 1)


def setup_inputs(seed: int = 0) -> dict:
    inp = _fwd_setup_inputs(seed)
    key = _jax.random.fold_in(_jax.random.key(seed), 7919)
    shape, _ = _output_shape()
    out = dict(inp)
    out["loss_target"] = _jax.random.normal(_jax.random.fold_in(key, 0), shape, _jnp.float32)
    for i, name in enumerate(TWIN_WEIGHTS):
        w = inp[name].astype(_jnp.float32)
        if MOMENT_SCALE is None:
            s = _jnp.sqrt(_jnp.mean(_jnp.square(w)) + 1e-30)
        else:
            s = MOMENT_SCALE[name]
        km, kv = _jax.random.split(_jax.random.fold_in(key, i + 1))
        out[name] = w
        out["m_" + name] = s * _jax.random.normal(km, w.shape, _jnp.float32)
        out["v_" + name] = (s * s) * _jax.random.uniform(kv, w.shape, _jnp.float32, 0.5, 1.5)
    if N_MICROBATCH > 1:
        for name, axis in PER_EXAMPLE_BATCH_AXIS.items():
            out[name] = _to_microbatches(out[name], axis)
    return {'x': out['x'], 'c': out['c'], 'positions': out['positions'], 'w_ada': out['w_ada'], 'b_ada': out['b_ada'], 'w_in': out['w_in'], 'b_in': out['b_in'], 'conv_dw': out['conv_dw'], 'conv_ln_g': out['conv_ln_g'], 'conv_ln_b': out['conv_ln_b'], 'w_conv_out': out['w_conv_out'], 'sc_dw': out['sc_dw'], 'w_sc_out': out['w_sc_out'], 'q_norm_g': out['q_norm_g'], 'w_uq': out['w_uq'], 'kv_norm_g': out['kv_norm_g'], 'w_ukv': out['w_ukv'], 'w_mla_out': out['w_mla_out'], 'w_pool': out['w_pool'], 'pool_scale': out['pool_scale'], 'w_pool_out': out['w_pool_out'], 'w_o': out['w_o'], 'ln1_g': out['ln1_g'], 'ln1_b': out['ln1_b'], 'w_up': out['w_up'], 'ffn_dw': out['ffn_dw'], 'w_down': out['w_down'], 'ln2_g': out['ln2_g'], 'ln2_b': out['ln2_b'], 'loss_target': out['loss_target'], 'm_w_ada': out['m_w_ada'], 'm_b_ada': out['m_b_ada'], 'm_w_in': out['m_w_in'], 'm_b_in': out['m_b_in'], 'm_conv_dw': out['m_conv_dw'], 'm_conv_ln_g': out['m_conv_ln_g'], 'm_conv_ln_b': out['m_conv_ln_b'], 'm_w_conv_out': out['m_w_conv_out'], 'm_sc_dw': out['m_sc_dw'], 'm_w_sc_out': out['m_w_sc_out'], 'm_q_norm_g': out['m_q_norm_g'], 'm_w_uq': out['m_w_uq'], 'm_kv_norm_g': out['m_kv_norm_g'], 'm_w_ukv': out['m_w_ukv'], 'm_w_mla_out': out['m_w_mla_out'], 'm_w_pool': out['m_w_pool'], 'm_pool_scale': out['m_pool_scale'], 'm_w_pool_out': out['m_w_pool_out'], 'm_w_o': out['m_w_o'], 'm_ln1_g': out['m_ln1_g'], 'm_ln1_b': out['m_ln1_b'], 'm_w_up': out['m_w_up'], 'm_ffn_dw': out['m_ffn_dw'], 'm_w_down': out['m_w_down'], 'm_ln2_g': out['m_ln2_g'], 'm_ln2_b': out['m_ln2_b'], 'v_w_ada': out['v_w_ada'], 'v_b_ada': out['v_b_ada'], 'v_w_in': out['v_w_in'], 'v_b_in': out['v_b_in'], 'v_conv_dw': out['v_conv_dw'], 'v_conv_ln_g': out['v_conv_ln_g'], 'v_conv_ln_b': out['v_conv_ln_b'], 'v_w_conv_out': out['v_w_conv_out'], 'v_sc_dw': out['v_sc_dw'], 'v_w_sc_out': out['v_w_sc_out'], 'v_q_norm_g': out['v_q_norm_g'], 'v_w_uq': out['v_w_uq'], 'v_kv_norm_g': out['v_kv_norm_g'], 'v_w_ukv': out['v_w_ukv'], 'v_w_mla_out': out['v_w_mla_out'], 'v_w_pool': out['v_w_pool'], 'v_pool_scale': out['v_pool_scale'], 'v_w_pool_out': out['v_w_pool_out'], 'v_w_o': out['v_w_o'], 'v_ln1_g': out['v_ln1_g'], 'v_ln1_b': out['v_ln1_b'], 'v_w_up': out['v_w_up'], 'v_ffn_dw': out['v_ffn_dw'], 'v_w_down': out['v_w_down'], 'v_ln2_g': out['v_ln2_g'], 'v_ln2_b': out['v_ln2_b']}


def _loss(weights, diff, rest, loss_target):
    with _jax.named_scope("forward"):
        args = {**rest, TWIN_DIFF_INPUT: diff, **{k: w.astype(_WEIGHT_DTYPES[k]) for k, w in weights.items()}}
        y = _forward(args)
    with _jax.named_scope("loss_head"):
        err = _jnp.square(y.astype(_jnp.float32) - loss_target)
        return 0.5 * _jnp.sum(_jnp.mean(err, axis=-1)) if err.ndim else 0.5 * err


def _adamw(w, g, m, v):
    m = ADAM_B1 * m + (1.0 - ADAM_B1) * g
    v = ADAM_B2 * v + (1.0 - ADAM_B2) * _jnp.square(g)
    m_hat = m / (1.0 - ADAM_B1 ** ADAM_STEP)
    v_hat = v / (1.0 - ADAM_B2 ** ADAM_STEP)
    delta = -ADAM_LR * (m_hat / (_jnp.sqrt(v_hat) + ADAM_EPS) + ADAM_WD * w)
    return delta, m, v


def reference(x, c, positions, w_ada, b_ada, w_in, b_in, conv_dw, conv_ln_g, conv_ln_b, w_conv_out, sc_dw, w_sc_out, q_norm_g, w_uq, kv_norm_g, w_ukv, w_mla_out, w_pool, pool_scale, w_pool_out, w_o, ln1_g, ln1_b, w_up, ffn_dw, w_down, ln2_g, ln2_b, loss_target, m_w_ada, m_b_ada, m_w_in, m_b_in, m_conv_dw, m_conv_ln_g, m_conv_ln_b, m_w_conv_out, m_sc_dw, m_w_sc_out, m_q_norm_g, m_w_uq, m_kv_norm_g, m_w_ukv, m_w_mla_out, m_w_pool, m_pool_scale, m_w_pool_out, m_w_o, m_ln1_g, m_ln1_b, m_w_up, m_ffn_dw, m_w_down, m_ln2_g, m_ln2_b, v_w_ada, v_b_ada, v_w_in, v_b_in, v_conv_dw, v_conv_ln_g, v_conv_ln_b, v_w_conv_out, v_sc_dw, v_w_sc_out, v_q_norm_g, v_w_uq, v_kv_norm_g, v_w_ukv, v_w_mla_out, v_w_pool, v_pool_scale, v_w_pool_out, v_w_o, v_ln1_g, v_ln1_b, v_w_up, v_ffn_dw, v_w_down, v_ln2_g, v_ln2_b):
    given = dict(x=x, c=c, positions=positions, w_ada=w_ada, b_ada=b_ada, w_in=w_in, b_in=b_in, conv_dw=conv_dw, conv_ln_g=conv_ln_g, conv_ln_b=conv_ln_b, w_conv_out=w_conv_out, sc_dw=sc_dw, w_sc_out=w_sc_out, q_norm_g=q_norm_g, w_uq=w_uq, kv_norm_g=kv_norm_g, w_ukv=w_ukv, w_mla_out=w_mla_out, w_pool=w_pool, pool_scale=pool_scale, w_pool_out=w_pool_out, w_o=w_o, ln1_g=ln1_g, ln1_b=ln1_b, w_up=w_up, ffn_dw=ffn_dw, w_down=w_down, ln2_g=ln2_g, ln2_b=ln2_b, loss_target=loss_target, m_w_ada=m_w_ada, m_b_ada=m_b_ada, m_w_in=m_w_in, m_b_in=m_b_in, m_conv_dw=m_conv_dw, m_conv_ln_g=m_conv_ln_g, m_conv_ln_b=m_conv_ln_b, m_w_conv_out=m_w_conv_out, m_sc_dw=m_sc_dw, m_w_sc_out=m_w_sc_out, m_q_norm_g=m_q_norm_g, m_w_uq=m_w_uq, m_kv_norm_g=m_kv_norm_g, m_w_ukv=m_w_ukv, m_w_mla_out=m_w_mla_out, m_w_pool=m_w_pool, m_pool_scale=m_pool_scale, m_w_pool_out=m_w_pool_out, m_w_o=m_w_o, m_ln1_g=m_ln1_g, m_ln1_b=m_ln1_b, m_w_up=m_w_up, m_ffn_dw=m_ffn_dw, m_w_down=m_w_down, m_ln2_g=m_ln2_g, m_ln2_b=m_ln2_b, v_w_ada=v_w_ada, v_b_ada=v_b_ada, v_w_in=v_w_in, v_b_in=v_b_in, v_conv_dw=v_conv_dw, v_conv_ln_g=v_conv_ln_g, v_conv_ln_b=v_conv_ln_b, v_w_conv_out=v_w_conv_out, v_sc_dw=v_sc_dw, v_w_sc_out=v_w_sc_out, v_q_norm_g=v_q_norm_g, v_w_uq=v_w_uq, v_kv_norm_g=v_kv_norm_g, v_w_ukv=v_w_ukv, v_w_mla_out=v_w_mla_out, v_w_pool=v_w_pool, v_pool_scale=v_pool_scale, v_w_pool_out=v_w_pool_out, v_w_o=v_w_o, v_ln1_g=v_ln1_g, v_ln1_b=v_ln1_b, v_w_up=v_w_up, v_ffn_dw=v_ffn_dw, v_w_down=v_w_down, v_ln2_g=v_ln2_g, v_ln2_b=v_ln2_b)
    weights = {n: given[n] for n in TWIN_WEIGHTS}
    shared = {n: given[n] for n in SHARED_INPUTS}
    per_example = {n: given[n] for n in ['x', 'c', 'positions']}
    grad_fn = _jax.value_and_grad(_loss, argnums=(0, 1))

    def one_microbatch(ex, loss_target):
        ex = dict(ex)
        diff = ex.pop(TWIN_DIFF_INPUT)
        return grad_fn(weights, diff, {**shared, **ex}, loss_target)

    if N_MICROBATCH == 1:
        loss, (grad_w, grad_x) = one_microbatch(per_example, given["loss_target"])
    else:
        def body(carry, xs):
            loss_sum, grad_sum = carry
            l_k, (gw_k, gx_k) = one_microbatch(xs[0], xs[1])
            with _jax.named_scope("update"):
                return (loss_sum + l_k, _jax.tree.map(_jnp.add, grad_sum, gw_k)), gx_k

        init = (_jnp.zeros((), _jnp.float32), _jax.tree.map(_jnp.zeros_like, weights))
        (loss, grad_w), grad_x = _jax.lax.scan(body, init, (per_example, given["loss_target"]))
    with _jax.named_scope("update"):
        delta_w, new_m, new_v = {}, {}, {}
        for n in TWIN_WEIGHTS:
            delta_w[n], new_m[n], new_v[n] = _adamw(weights[n], grad_w[n], given["m_" + n], given["v_" + n])
    return (loss, grad_x, *[grad_w[n] for n in TWIN_WEIGHTS], *[delta_w[n] for n in TWIN_WEIGHTS],
            *[new_m[n] for n in TWIN_WEIGHTS], *[new_v[n] for n in TWIN_WEIGHTS])
```

```python
import functools

import jax
import jax.numpy as jnp
from jax import lax
from jax.experimental import pallas as pl
from jax.experimental.pallas import tpu as pltpu

F32 = jnp.float32
BF16 = jnp.bfloat16
MXU = jnp.bfloat16

D_MODEL = 1024
N_HEADS = 8
QK_NOPE = 64
QK_ROPE = 32
V_DIM = 64
HEAD_PAD = 128
ROPE_THETA = 10000.0
POOL_WINDOWS = (2, 4, 8, 16)
LN_EPS = 1e-5
RMS_EPS = 1e-6
DEPTH_FOR_DEEPNORM = 4
ALPHA = (2.0 * DEPTH_FOR_DEEPNORM) ** 0.25
ATTN_SCALE = (QK_NOPE + QK_ROPE) ** -0.5
IN_COLS = 7584
IN_PAD_AT = 2976
IN_PAD = 96
ADAM_LR, ADAM_B1, ADAM_B2, ADAM_EPS, ADAM_WD, ADAM_STEP = 0.001, 0.9, 0.999, 1e-08, 0.01, 10

VMEM_LIMIT = 48 * 1024 * 1024
LANE = 128
CONV_HALO = 32
PACK_W = 512
SHARD_ROW_QUANTUM = 256
MESH_T = pl.DeviceIdType.MESH

SHARDED = (
    ("w_ada", (1024, 6144), 1, True),
    ("w_in", (1024, IN_COLS), 1, True),
    ("conv_dw", (31, 512), 1, False),
    ("w_conv_out", (512, 1024), 1, True),
    ("sc_dw", (3, 512), 1, False),
    ("w_sc_out", (512, 1024), 1, True),
    ("w_uq", (256, 768), 1, True),
    ("w_ukv", (128, 1024), 1, True),
    ("w_mla_out", (512, 1024), 1, True),
    ("w_pool_out", (512, 1024), 1, True),
    ("w_o", (1024, 1024), 0, True),
    ("w_up", (1024, 5632), 1, True),
    ("ffn_dw", (3, 5632), 1, False),
    ("w_down", (2816, 1024), 0, True),
)
REPLICATED = ("b_ada", "b_in", "conv_ln_g", "conv_ln_b", "q_norm_g", "kv_norm_g", "w_pool", "pool_scale",
              "ln1_g", "ln1_b", "ln2_g", "ln2_b")
WEIGHT_ORDER = ("w_ada", "b_ada", "w_in", "b_in", "conv_dw", "conv_ln_g", "conv_ln_b", "w_conv_out", "sc_dw",
                "w_sc_out", "q_norm_g", "w_uq", "kv_norm_g", "w_ukv", "w_mla_out", "w_pool", "pool_scale",
                "w_pool_out", "w_o", "ln1_g", "ln1_b", "w_up", "ffn_dw", "w_down", "ln2_g", "ln2_b")


def _pick(dim, cap, quantum=LANE):
    best = None
    t = quantum
    while t <= min(dim, cap):
        if dim % t == 0:
            best = t
        t += quantum
    return dim if best is None else best


def _params(n_grid):
    return pltpu.CompilerParams(dimension_semantics=("arbitrary",) * n_grid, vmem_limit_bytes=VMEM_LIMIT)


def _mm(a, b, *, ta=False, tb=False, bias=None, name):
    if ta:
        k_dim, m_dim = a.shape
    else:
        m_dim, k_dim = a.shape
    if tb:
        n_dim, k2 = b.shape
    else:
        k2, n_dim = b.shape
    assert k_dim == k2, (a.shape, b.shape, ta, tb)
    tm = _pick(m_dim, 1024)
    tn = _pick(n_dim, 1024)
    tk = _pick(k_dim, 1024)
    nk = k_dim // tk
    a_spec = pl.BlockSpec((tk, tm), lambda i, j, k: (k, i)) if ta else pl.BlockSpec((tm, tk), lambda i, j, k: (i, k))
    b_spec = pl.BlockSpec((tn, tk), lambda i, j, k: (j, k)) if tb else pl.BlockSpec((tk, tn), lambda i, j, k: (k, j))
    dims = (((0 if ta else 1,), (1 if tb else 0,)), ((), ()))
    has_bias = bias is not None

    def body(*refs):
        if has_bias:
            a_ref, b_ref, bias_ref, o_ref, acc_ref = refs
        else:
            a_ref, b_ref, o_ref, acc_ref = refs
        k = pl.program_id(2)

        @pl.when(k == 0)
        def _():
            acc_ref[...] = jnp.zeros_like(acc_ref)

        acc_ref[...] += lax.dot_general(a_ref[...].astype(MXU), b_ref[...].astype(MXU), dims,
                                        preferred_element_type=F32)

        @pl.when(k == nk - 1)
        def _():
            out = acc_ref[...]
            if has_bias:
                out = out + bias_ref[...]
            o_ref[...] = out

    in_specs = [a_spec, b_spec]
    args = [a, b]
    if has_bias:
        in_specs.append(pl.BlockSpec((1, tn), lambda i, j, k: (0, j)))
        args.append(bias.reshape(1, n_dim))
    return pl.pallas_call(
        body, name=name, grid=(m_dim // tm, n_dim // tn, nk), in_specs=in_specs,
        out_specs=pl.BlockSpec((tm, tn), lambda i, j, k: (i, j)),
        out_shape=jax.ShapeDtypeStruct((m_dim, n_dim), F32),
        scratch_shapes=[pltpu.VMEM((tm, tn), F32)], compiler_params=_params(3),
    )(*args)


def _colsum(x, *, name):
    rows, n = x.shape
    tr = _pick(rows, 1024, 8)
    tn = _pick(n, 1024)

    def body(x_ref, o_ref):
        @pl.when(pl.program_id(1) == 0)
        def _():
            o_ref[...] = jnp.zeros_like(o_ref)

        o_ref[...] += jnp.sum(x_ref[...], axis=0, keepdims=True)

    out = pl.pallas_call(
        body, name=name, grid=(n // tn, rows // tr), in_specs=[pl.BlockSpec((tr, tn), lambda j, i: (i, j))],
        out_specs=pl.BlockSpec((1, tn), lambda j, i: (0, j)), out_shape=jax.ShapeDtypeStruct((1, n), F32),
        compiler_params=_params(2),
    )(x)
    return out[0]


@functools.partial(jax.custom_vjp, nondiff_argnums=(2,))
def linear(x, w, name):
    return _mm(x, w, name=name + "_f")


def _linear_fwd(x, w, name):
    return _mm(x, w, name=name + "_f"), (x, w)


def _linear_bwd(name, res, dy):
    x, w = res
    return _mm(dy, w, tb=True, name=name + "_dx"), _mm(x, dy, ta=True, name=name + "_dw")


linear.defvjp(_linear_fwd, _linear_bwd)


@functools.partial(jax.custom_vjp, nondiff_argnums=(3,))
def linear_bias(x, w, b, name):
    return _mm(x, w, bias=b, name=name + "_f")


def _linear_bias_fwd(x, w, b, name):
    return _mm(x, w, bias=b, name=name + "_f"), (x, w)


def _linear_bias_bwd(name, res, dy):
    x, w = res
    return (_mm(dy, w, tb=True, name=name + "_dx"), _mm(x, dy, ta=True, name=name + "_dw"),
            _colsum(dy, name=name + "_db"))


linear_bias.defvjp(_linear_bias_fwd, _linear_bias_bwd)


def _norm_stats(x, center, eps):
    if center:
        mu = jnp.mean(x, axis=-1, keepdims=True)
        xc = x - mu
    else:
        xc = x
    rstd = lax.rsqrt(jnp.mean(xc * xc, axis=-1, keepdims=True) + eps)
    return xc * rstd


def _norm_fwd_call(x, g, b, *, center, name):
    rows, d = x.shape
    tr = _pick(rows, 512, 8)
    eps = LN_EPS if center else RMS_EPS

    def body(x_ref, g_ref, b_ref, o_ref):
        xhat = _norm_stats(x_ref[...], center, eps)
        y = xhat * g_ref[...]
        if center:
            y = y + b_ref[...]
        o_ref[...] = y

    vec = pl.BlockSpec((1, d), lambda i: (0, 0))
    return pl.pallas_call(
        body, name=name, grid=(rows // tr,), in_specs=[pl.BlockSpec((tr, d), lambda i: (i, 0)), vec, vec],
        out_specs=pl.BlockSpec((tr, d), lambda i: (i, 0)), out_shape=jax.ShapeDtypeStruct((rows, d), F32),
        compiler_params=_params(1),
    )(x, g.reshape(1, d), b.reshape(1, d))


def _norm_bwd_call(x, g, dy, *, center, name):
    rows, d = x.shape
    tr = _pick(rows, 512, 8)
    eps = LN_EPS if center else RMS_EPS

    def body(x_ref, g_ref, dy_ref, dx_ref, dg_ref, db_ref):
        @pl.when(pl.program_id(0) == 0)
        def _():
            dg_ref[...] = jnp.zeros_like(dg_ref)
            db_ref[...] = jnp.zeros_like(db_ref)

        x = x_ref[...]
        dy = dy_ref[...]
        if center:
            mu = jnp.mean(x, axis=-1, keepdims=True)
            xc = x - mu
        else:
            xc = x
        rstd = lax.rsqrt(jnp.mean(xc * xc, axis=-1, keepdims=True) + eps)
        xhat = xc * rstd
        dyg = dy * g_ref[...]
        proj = jnp.mean(dyg * xhat, axis=-1, keepdims=True)
        dx = dyg - xhat * proj
        if center:
            dx = dx - jnp.mean(dyg, axis=-1, keepdims=True)
        dx_ref[...] = dx * rstd
        dg_ref[...] += jnp.sum(dy * xhat, axis=0, keepdims=True)
        db_ref[...] += jnp.sum(dy, axis=0, keepdims=True)

    vec = pl.BlockSpec((1, d), lambda i: (0, 0))
    row = pl.BlockSpec((tr, d), lambda i: (i, 0))
    dx, dg, db = pl.pallas_call(
        body, name=name, grid=(rows // tr,), in_specs=[row, vec, row], out_specs=[row, vec, vec],
        out_shape=[jax.ShapeDtypeStruct((rows, d), F32), jax.ShapeDtypeStruct((1, d), F32),
                   jax.ShapeDtypeStruct((1, d), F32)],
        compiler_params=_params(1),
    )(x, g.reshape(1, d), dy)
    return dx, dg[0], db[0]


@functools.partial(jax.custom_vjp, nondiff_argnums=(3,))
def layer_norm(x, g, b, name):
    return _norm_fwd_call(x, g, b, center=True, name=name + "_f")


def _layer_norm_fwd(x, g, b, name):
    return _norm_fwd_call(x, g, b, center=True, name=name + "_f"), (x, g)


def _layer_norm_bwd(name, res, dy):
    x, g = res
    return _norm_bwd_call(x, g, dy, center=True, name=name + "_b")


layer_norm.defvjp(_layer_norm_fwd, _layer_norm_bwd)


@functools.partial(jax.custom_vjp, nondiff_argnums=(2,))
def rms_norm(x, g, name):
    return _norm_fwd_call(x, g, jnp.zeros_like(g), center=False, name=name + "_f")


def _rms_norm_fwd(x, g, name):
    return _norm_fwd_call(x, g, jnp.zeros_like(g), center=False, name=name + "_f"), (x, g)


def _rms_norm_bwd(name, res, dy):
    x, g = res
    dx, dg, _ = _norm_bwd_call(x, g, dy, center=False, name=name + "_b")
    return dx, dg


rms_norm.defvjp(_rms_norm_fwd, _rms_norm_bwd)


def _conv_tiles(rows, ch):
    tr = _pick(rows, 512, CONV_HALO)
    assert tr >= CONV_HALO and rows % tr == 0
    return tr, _pick(ch, 512)


def _pad_taps(w):
    k = w.shape[0]
    kp = -(-k // 8) * 8
    return jnp.pad(w, ((0, kp - k), (0, 0))), k, kp


def _dwconv_fwd_call(x, w, *, name):
    rows, ch = x.shape
    tr, tc = _conv_tiles(rows, ch)
    wp, taps, kp = _pad_taps(w)
    assert taps - 1 <= CONV_HALO
    ext = tr + CONV_HALO

    def body(x_ref, xprev_ref, w_ref, o_ref):
        i = pl.program_id(1)
        halo = xprev_ref[tr - CONV_HALO:, :]
        halo = jnp.where(i > 0, halo, jnp.zeros_like(halo))
        xx = jnp.concatenate([halo, x_ref[...]], axis=0)
        acc = jnp.zeros((tr, tc), F32)
        for k in range(taps):
            shift = taps - 1 - k
            term = xx if shift == 0 else pltpu.roll(xx, shift, 0)
            acc = acc + w_ref[k:k + 1, :] * term[CONV_HALO:, :]
        o_ref[...] = acc

    return pl.pallas_call(
        body, name=name, grid=(ch // tc, rows // tr),
        in_specs=[pl.BlockSpec((tr, tc), lambda j, i: (i, j)),
                  pl.BlockSpec((tr, tc), lambda j, i: (jnp.maximum(i - 1, 0), j)),
                  pl.BlockSpec((kp, tc), lambda j, i: (0, j))],
        out_specs=pl.BlockSpec((tr, tc), lambda j, i: (i, j)), out_shape=jax.ShapeDtypeStruct((rows, ch), F32),
        compiler_params=_params(2),
    )(x, x, wp)


def _dwconv_bwd_call(x, w, dy, *, name):
    rows, ch = x.shape
    tr, tc = _conv_tiles(rows, ch)
    wp, taps, kp = _pad_taps(w)
    n_row_tiles = rows // tr
    ext = tr + CONV_HALO

    def body(x_ref, xprev_ref, dy_ref, dynext_ref, w_ref, dx_ref, dw_ref):
        i = pl.program_id(1)

        @pl.when(i == 0)
        def _():
            dw_ref[...] = jnp.zeros_like(dw_ref)

        halo = xprev_ref[tr - CONV_HALO:, :]
        halo = jnp.where(i > 0, halo, jnp.zeros_like(halo))
        xx = jnp.concatenate([halo, x_ref[...]], axis=0)
        dy = dy_ref[...]
        ahead = dynext_ref[:CONV_HALO, :]
        ahead = jnp.where(i < n_row_tiles - 1, ahead, jnp.zeros_like(ahead))
        yy = jnp.concatenate([dy, ahead], axis=0)
        dx = jnp.zeros((tr, tc), F32)
        for k in range(taps):
            shift = taps - 1 - k
            fwd = yy if shift == 0 else pltpu.roll(yy, ext - shift, 0)
            dx = dx + w_ref[k:k + 1, :] * fwd[:tr, :]
            back = xx if shift == 0 else pltpu.roll(xx, shift, 0)
            dw_ref[k:k + 1, :] += jnp.sum(dy * back[CONV_HALO:, :], axis=0, keepdims=True)
        dx_ref[...] = dx

    cur = pl.BlockSpec((tr, tc), lambda j, i: (i, j))
    dx, dw = pl.pallas_call(
        body, name=name, grid=(ch // tc, n_row_tiles),
        in_specs=[cur, pl.BlockSpec((tr, tc), lambda j, i: (jnp.maximum(i - 1, 0), j)), cur,
                  pl.BlockSpec((tr, tc), lambda j, i: (jnp.minimum(i + 1, n_row_tiles - 1), j)),
                  pl.BlockSpec((kp, tc), lambda j, i: (0, j))],
        out_specs=[cur, pl.BlockSpec((kp, tc), lambda j, i: (0, j))],
        out_shape=[jax.ShapeDtypeStruct((rows, ch), F32), jax.ShapeDtypeStruct((kp, ch), F32)],
        compiler_params=_params(2),
    )(x, x, dy, dy, wp)
    return dx, dw[:taps]


@functools.partial(jax.custom_vjp, nondiff_argnums=(2,))
def dwconv(x, w, name):
    return _dwconv_fwd_call(x, w, name=name + "_f")


def _dwconv_fwd(x, w, name):
    return _dwconv_fwd_call(x, w, name=name + "_f"), (x, w)


def _dwconv_bwd(name, res, dy):
    x, w = res
    return _dwconv_bwd_call(x, w, dy, name=name + "_b")


dwconv.defvjp(_dwconv_fwd, _dwconv_bwd)


def _attn_tiles(seq):
    return _pick(seq, 1024), _pick(seq, 512)


def _lane_tile(v, width):
    return v if width == LANE else jnp.tile(v, (1, width // LANE))


def _attn_fwd_call(q, k, v, *, name):
    heads, seq, dh = q.shape
    tq, tk = _attn_tiles(seq)
    nq, nk = seq // tq, seq // tk

    def body(q_ref, k_ref, v_ref, o_ref, lse_ref, m_sc, l_sc, acc_sc):
        i = pl.program_id(1)
        j = pl.program_id(2)

        @pl.when(j == 0)
        def _():
            m_sc[...] = jnp.full(m_sc.shape, -jnp.inf, F32)
            l_sc[...] = jnp.zeros_like(l_sc)
            acc_sc[...] = jnp.zeros_like(acc_sc)

        def step(masked):
            s = lax.dot_general(q_ref[0], k_ref[0], (((1,), (1,)), ((), ())), preferred_element_type=F32) * ATTN_SCALE
            if masked:
                row = i * tq + lax.broadcasted_iota(jnp.int32, (tq, tk), 0)
                col = j * tk + lax.broadcasted_iota(jnp.int32, (tq, tk), 1)
                s = jnp.where(col <= row, s, -jnp.inf)
            m_prev = m_sc[...]
            m_new = jnp.maximum(m_prev, jnp.max(s, axis=1, keepdims=True))
            alpha = jnp.exp(m_prev - m_new)
            p = jnp.exp(s - _lane_tile(m_new, tk))
            l_sc[...] = alpha * l_sc[...] + jnp.sum(p, axis=1, keepdims=True)
            acc_sc[...] = alpha * acc_sc[...] + jnp.dot(p.astype(MXU), v_ref[0], preferred_element_type=F32)
            m_sc[...] = m_new

        first_col = j * tk
        last_col = first_col + tk - 1
        first_row = i * tq
        last_row = first_row + tq - 1

        @pl.when(last_col <= first_row)
        def _():
            step(False)

        @pl.when(jnp.logical_and(last_col > first_row, first_col <= last_row))
        def _():
            step(True)

        @pl.when(j == nk - 1)
        def _():
            o_ref[0] = acc_sc[...] / l_sc[...]
            lse_ref[0] = m_sc[...] + jnp.log(l_sc[...])

    def kv_map(h, i, j):
        return (h, jnp.minimum(j, (i * tq + tq - 1) // tk), 0)

    q_spec = pl.BlockSpec((1, tq, dh), lambda h, i, j: (h, i, 0))
    kv_spec = pl.BlockSpec((1, tk, dh), kv_map)
    return pl.pallas_call(
        body, name=name, grid=(heads, nq, nk), in_specs=[q_spec, kv_spec, kv_spec], out_specs=[q_spec, q_spec],
        out_shape=[jax.ShapeDtypeStruct((heads, seq, dh), F32), jax.ShapeDtypeStruct((heads, seq, LANE), F32)],
        scratch_shapes=[pltpu.VMEM((tq, LANE), F32), pltpu.VMEM((tq, LANE), F32), pltpu.VMEM((tq, dh), F32)],
        compiler_params=_params(3),
    )(q, k, v)


def _attn_bwd_call(q, k, v, o, lse, do, *, name):
    heads, seq, dh = q.shape
    tk, tq = _attn_tiles(seq)
    nq, nk = seq // tq, seq // tk

    def body(q_ref, k_ref, v_ref, o_ref, lse_ref, do_ref, dq_ref, dk_ref, dv_ref, dk_sc, dv_sc):
        j = pl.program_id(1)
        i = pl.program_id(2)

        @pl.when(jnp.logical_and(j == 0, i == 0))
        def _():
            dq_ref[...] = jnp.zeros_like(dq_ref)

        @pl.when(i == 0)
        def _():
            dk_sc[...] = jnp.zeros_like(dk_sc)
            dv_sc[...] = jnp.zeros_like(dv_sc)

        def step(masked):
            qb = q_ref[0]
            kb = k_ref[0]
            do_f = do_ref[0]
            do_b = do_f.astype(MXU)
            s = lax.dot_general(qb, kb, (((1,), (1,)), ((), ())), preferred_element_type=F32) * ATTN_SCALE
            if masked:
                row = i * tq + lax.broadcasted_iota(jnp.int32, (tq, tk), 0)
                col = j * tk + lax.broadcasted_iota(jnp.int32, (tq, tk), 1)
                s = jnp.where(col <= row, s, -jnp.inf)
            p = jnp.exp(s - _lane_tile(lse_ref[0], tk))
            dp = lax.dot_general(do_b, v_ref[0], (((1,), (1,)), ((), ())), preferred_element_type=F32)
            delta = jnp.sum(do_f * o_ref[0], axis=1, keepdims=True)
            ds = (p * (dp - delta) * ATTN_SCALE).astype(MXU)
            dv_sc[...] += lax.dot_general(p.astype(MXU), do_b, (((0,), (0,)), ((), ())), preferred_element_type=F32)
            dk_sc[...] += lax.dot_general(ds, qb, (((0,), (0,)), ((), ())), preferred_element_type=F32)
            rows = pl.ds(pl.multiple_of(i * tq, tq), tq)
            dq_ref[0, rows, :] += jnp.dot(ds, kb, preferred_element_type=F32)

        first_col = j * tk
        last_col = first_col + tk - 1
        first_row = i * tq
        last_row = first_row + tq - 1

        @pl.when(last_col <= first_row)
        def _():
            step(False)

        @pl.when(jnp.logical_and(last_col > first_row, first_col <= last_row))
        def _():
            step(True)

        @pl.when(i == nq - 1)
        def _():
            dk_ref[0] = dk_sc[...]
            dv_ref[0] = dv_sc[...]

    def q_map(h, j, i):
        return (h, jnp.maximum(i, (j * tk) // tq), 0)

    q_spec = pl.BlockSpec((1, tq, dh), q_map)
    lse_spec = pl.BlockSpec((1, tq, LANE), q_map)
    kv_spec = pl.BlockSpec((1, tk, dh), lambda h, j, i: (h, j, 0))
    head_spec = pl.BlockSpec((1, seq, dh), lambda h, j, i: (h, 0, 0))
    return pl.pallas_call(
        body, name=name, grid=(heads, nk, nq), in_specs=[q_spec, kv_spec, kv_spec, q_spec, lse_spec, q_spec],
        out_specs=[head_spec, kv_spec, kv_spec],
        out_shape=[jax.ShapeDtypeStruct((heads, seq, dh), F32)] * 3,
        scratch_shapes=[pltpu.VMEM((tk, dh), F32), pltpu.VMEM((tk, dh), F32)],
        compiler_params=_params(3),
    )(q, k, v, o, lse, do)


@functools.partial(jax.custom_vjp, nondiff_argnums=(3,))
def attention(q, k, v, name):
    return _attn_fwd_call(q.astype(MXU), k.astype(MXU), v.astype(MXU), name=name + "_f")[0]


def _attention_fwd(q, k, v, name):
    qb, kb, vb = q.astype(MXU), k.astype(MXU), v.astype(MXU)
    o, lse = _attn_fwd_call(qb, kb, vb, name=name + "_f")
    return o, (qb, kb, vb, o, lse)


def _attention_bwd(name, res, do):
    qb, kb, vb, o, lse = res
    return tuple(_attn_bwd_call(qb, kb, vb, o, lse, do, name=name + "_b"))


attention.defvjp(_attention_fwd, _attention_bwd)


def _loss_call(y, target):
    rows, d = y.shape
    tr = _pick(rows, 512, 8)

    def body(y_ref, t_ref, dy_ref, loss_ref):
        @pl.when(pl.program_id(0) == 0)
        def _():
            loss_ref[...] = jnp.zeros_like(loss_ref)

        err = y_ref[...] - t_ref[...]
        dy_ref[...] = err * (1.0 / d)
        loss_ref[...] += jnp.sum(err * err, axis=0, keepdims=True)

    row = pl.BlockSpec((tr, d), lambda i: (i, 0))
    vec = pl.BlockSpec((1, d), lambda i: (0, 0))
    dy, part = pl.pallas_call(
        body, name="loss_head", grid=(rows // tr,), in_specs=[row, row], out_specs=[row, vec],
        out_shape=[jax.ShapeDtypeStruct((rows, d), F32), jax.ShapeDtypeStruct((1, d), F32)],
        compiler_params=_params(1),
    )(y, target)
    return 0.5 * jnp.sum(part) / d, dy


def _adamw_call(w, g, m, v, *, name):
    rows, width = w.shape
    tr = _pick(rows, 512, 8)
    c1 = 1.0 - ADAM_B1 ** ADAM_STEP
    c2 = 1.0 - ADAM_B2 ** ADAM_STEP

    def body(w_ref, g_ref, m_ref, v_ref, d_ref, nm_ref, nv_ref):
        g = g_ref[...]
        m = ADAM_B1 * m_ref[...] + (1.0 - ADAM_B1) * g
        v = ADAM_B2 * v_ref[...] + (1.0 - ADAM_B2) * (g * g)
        m_hat = m / c1
        v_hat = v / c2
        d_ref[...] = -ADAM_LR * (m_hat / (jnp.sqrt(v_hat) + ADAM_EPS) + ADAM_WD * w_ref[...])
        nm_ref[...] = m
        nv_ref[...] = v

    row = pl.BlockSpec((tr, width), lambda i: (i, 0))
    return pl.pallas_call(
        body, name=name, grid=(rows // tr,), in_specs=[row] * 4, out_specs=[row] * 3,
        out_shape=[jax.ShapeDtypeStruct((rows, width), F32)] * 3, compiler_params=_params(1),
    )(w, g, m, v)


def _ordered_sum(y, *, name):
    n, rows, width = y.shape
    tr = _pick(rows, 256, 8)

    def body(y_ref, o_ref):
        acc = y_ref[0]
        for s in range(1, n):
            acc = acc + y_ref[s]
        o_ref[...] = acc

    return pl.pallas_call(
        body, name=name, grid=(rows // tr,), in_specs=[pl.BlockSpec((n, tr, width), lambda i: (0, i, 0))],
        out_specs=pl.BlockSpec((tr, width), lambda i: (i, 0)), out_shape=jax.ShapeDtypeStruct((rows, width), y.dtype),
        compiler_params=_params(1),
    )(y)


_GROUP_FLIPS = {
    "chips": ((1, 0, 0), (0, 1, 0), (1, 1, 0)),
    "cores": ((0, 0, 1),),
    "all": ((0, 0, 1), (0, 1, 0), (0, 1, 1), (1, 0, 0), (1, 0, 1), (1, 1, 0), (1, 1, 1)),
}


def _exchange(x, *, group, broadcast, name):
    flips = _GROUP_FLIPS[group]
    n = len(flips) + 1
    block = x.shape if broadcast else x.shape[1:]
    if not broadcast:
        assert x.shape[0] == n

    def body(x_ref, o_ref, send_sems, recv_sems, local_sem):
        mx, my, mc = lax.axis_index("x"), lax.axis_index("y"), lax.axis_index("c")

        def index(px, py, pc):
            return {"chips": 2 * px + py, "cores": pc, "all": 4 * px + 2 * py + pc}[group]

        def block_for(d):
            return x_ref if broadcast else x_ref.at[d]

        me = index(mx, my, mc)
        mine = pltpu.make_async_copy(block_for(me), o_ref.at[me], local_sem)
        mine.start()
        sends, recvs = [], []
        for k, (fx, fy, fc) in enumerate(flips):
            px = 1 - mx if fx else mx
            py = 1 - my if fy else my
            pc = 1 - mc if fc else mc
            peer = index(px, py, pc)
            sends.append(pltpu.make_async_remote_copy(
                src_ref=block_for(peer), dst_ref=o_ref.at[me], send_sem=send_sems.at[k], recv_sem=recv_sems.at[k],
                device_id=(px, py, pc), device_id_type=MESH_T))
            recvs.append(pltpu.make_async_remote_copy(
                src_ref=block_for(peer), dst_ref=o_ref.at[peer], send_sem=send_sems.at[k], recv_sem=recv_sems.at[k],
                device_id=(px, py, pc), device_id_type=MESH_T))
        for cp in sends:
            cp.start()
        for cp in recvs:
            cp.wait_recv()
        for cp in sends:
            cp.wait_send()
        mine.wait()

    any_spec = pl.BlockSpec(memory_space=pl.ANY)
    return pl.pallas_call(
        body, name=name, in_specs=[any_spec], out_specs=any_spec,
        out_shape=jax.ShapeDtypeStruct((n,) + tuple(block), x.dtype),
        scratch_shapes=[pltpu.SemaphoreType.DMA((n - 1,)), pltpu.SemaphoreType.DMA((n - 1,)), pltpu.SemaphoreType.DMA],
    )(x)


def _reduce_scatter(g, *, tag):
    n, rows, width = g.shape
    half = rows // 2
    halves = g.reshape(n, 2, half, width).transpose(1, 0, 2, 3)
    both = _exchange(halves, group="cores", broadcast=False, name=tag + "_pair")
    pair = _ordered_sum(both.reshape(2, n * half, width), name=tag + "_pair_sum").reshape(n, half, width)
    quad = _exchange(pair, group="chips", broadcast=False, name=tag + "_quad")
    mine = _ordered_sum(quad, name=tag + "_quad_sum")
    full = _exchange(mine, group="cores", broadcast=True, name=tag + "_share")
    return full.reshape(rows, width)


def _shard_shape(shape, axis):
    s = list(shape)
    s[axis] //= 4
    return tuple(s)


def _round_up(n, q):
    return -(-n // q) * q


def _pack_rows(flat, quantum_rows):
    n = flat.shape[-1]
    total = _round_up(n, PACK_W * quantum_rows)
    pad = [(0, 0)] * (flat.ndim - 1) + [(0, total - n)]
    return jnp.pad(flat, pad).reshape(flat.shape[:-1] + (total // PACK_W, PACK_W))


def _pack_weights_bf16(blocks):
    parts = []
    for name, _, _, as_bf16 in SHARDED:
        a = blocks[name]
        if as_bf16:
            parts.append(a.astype(BF16).reshape(-1))
        else:
            parts.append(lax.bitcast_convert_type(a, BF16).reshape(-1))
    return _pack_rows(jnp.concatenate(parts), 16)


def _unpack_weights(gathered):
    flat = gathered.reshape(4, -1)
    out, off = {}, 0
    for name, shape, axis, as_bf16 in SHARDED:
        sshape = _shard_shape(shape, axis)
        n = sshape[0] * sshape[1]
        if as_bf16:
            seg = flat[:, off:off + n].reshape((4,) + sshape).astype(F32)
            off += n
        else:
            seg = lax.bitcast_convert_type(flat[:, off:off + 2 * n].reshape((4,) + sshape + (2,)), F32)
            off += 2 * n
        out[name] = jnp.concatenate([seg[s] for s in range(4)], axis=axis)
    return out


def _pack_shards_f32(blocks):
    return _pack_rows(jnp.concatenate([blocks[name].reshape(-1) for name, _, _, _ in SHARDED]), SHARD_ROW_QUANTUM)


def _pack_grads(full):
    per_chip = []
    for s in range(4):
        parts = []
        for name, shape, axis, _ in SHARDED:
            width = shape[axis] // 4
            parts.append(lax.slice_in_dim(full[name], s * width, (s + 1) * width, axis=axis).reshape(-1))
        per_chip.append(jnp.concatenate(parts))
    return _pack_rows(jnp.stack(per_chip), SHARD_ROW_QUANTUM)


def _unpack_shards(packed):
    flat = packed.reshape(-1)
    out, off = {}, 0
    for name, shape, axis, _ in SHARDED:
        sshape = _shard_shape(shape, axis)
        n = sshape[0] * sshape[1]
        out[name] = flat[off:off + n].reshape(sshape)
        off += n
    return out


def _pack_replicated(tree):
    return _pack_rows(jnp.concatenate([tree[name].reshape(-1) for name in REPLICATED]), 8)


def _unpack_replicated(packed, like):
    flat = packed.reshape(-1)
    out, off = {}, 0
    for name in REPLICATED:
        n = like[name].size
        out[name] = flat[off:off + n].reshape(like[name].shape)
        off += n
    return out


def _rope(t, cos, sin):
    half = QK_ROPE // 2
    t1, t2 = t[..., :half], t[..., half:]
    return jnp.concatenate([t1 * cos - t2 * sin, t2 * cos + t1 * sin], axis=-1)


def _pad_in_cols(a):
    z = jnp.zeros(a.shape[:-1] + (IN_PAD,), a.dtype)
    return jnp.concatenate([a[..., :IN_PAD_AT], z, a[..., IN_PAD_AT:]], axis=-1)


def _pool_constants(seq):
    taps = max(POOL_WINDOWS)
    win = jnp.repeat(jnp.asarray(POOL_WINDOWS, jnp.int32), 512 // len(POOL_WINDOWS))
    lag = taps - 1 - jnp.arange(taps, dtype=jnp.int32)
    mask = (lag[:, None] < win[None, :]).astype(F32)
    cnt = jnp.minimum(jnp.arange(seq, dtype=jnp.int32)[:, None] + 1, win[None, :]).astype(F32)
    return mask, 1.0 / cnt


def _block_diag(w):
    g, n, _ = w.shape
    out = jnp.zeros((g * n, g * n), w.dtype)
    for i in range(g):
        out = lax.dynamic_update_slice(out, w[i], (i * n, i * n))
    return out


def _layer(x, c_act, p, cos, sin, pool_mask, pool_inv_cnt, tag):
    seq = x.shape[0]
    d = D_MODEL
    mod = linear_bias(c_act, p["w_ada"], p["b_ada"], tag + "ada")[0]
    sh1, sc1, g1, sh2, sc2, g2 = [mod[i * d:(i + 1) * d][None, :] for i in range(6)]

    h = x * (1.0 + sc1) + sh1
    proj = linear_bias(h, _pad_in_cols(p["w_in"]), _pad_in_cols(p["b_in"]), tag + "in")
    conv_a, conv_b = proj[:, 0:512], proj[:, 512:1024]
    sc_bg, sc_cg, sc_x = proj[:, 1024:1536], proj[:, 1536:2048], proj[:, 2048:2560]
    q_lat, kv_lat, k_rope = proj[:, 2560:2816], proj[:, 2816:2944], proj[:, 2944:2976]
    pool_u, gates = proj[:, 3072:3584], proj[:, 3584:7680]

    ya = conv_a * jax.nn.sigmoid(conv_b)
    ya = dwconv(ya, p["conv_dw"], tag + "convA")
    ya = jax.nn.silu(layer_norm(ya, p["conv_ln_g"], p["conv_ln_b"], tag + "convA_ln"))
    ya = linear(ya, p["w_conv_out"], tag + "convA_out")

    yb = linear(sc_bg * dwconv(sc_cg * sc_x, p["sc_dw"], tag + "sc"), p["w_sc_out"], tag + "sc_out")

    q = linear(rms_norm(q_lat, p["q_norm_g"], tag + "q_rms"), p["w_uq"], tag + "uq")
    q = q.reshape(seq, N_HEADS, QK_NOPE + QK_ROPE)
    kv = linear(rms_norm(kv_lat, p["kv_norm_g"], tag + "kv_rms"), p["w_ukv"], tag + "ukv")
    kv = kv.reshape(seq, N_HEADS, QK_NOPE + V_DIM)
    q_rope = _rope(q[..., QK_NOPE:], cos[:, None, :], sin[:, None, :])
    k_rope_r = jnp.broadcast_to(_rope(k_rope, cos, sin)[:, None, :], (seq, N_HEADS, QK_ROPE))
    zq = jnp.zeros((seq, N_HEADS, HEAD_PAD - QK_NOPE - QK_ROPE), F32)
    zv = jnp.zeros((seq, N_HEADS, HEAD_PAD - V_DIM), F32)
    qh = jnp.concatenate([q[..., :QK_NOPE], q_rope, zq], axis=-1).transpose(1, 0, 2)
    kh = jnp.concatenate([kv[..., :QK_NOPE], k_rope_r, zq], axis=-1).transpose(1, 0, 2)
    vh = jnp.concatenate([kv[..., QK_NOPE:], zv], axis=-1).transpose(1, 0, 2)
    att = attention(qh, kh, vh, tag + "attn")
    att = att[:, :, :V_DIM].transpose(1, 0, 2).reshape(seq, N_HEADS * V_DIM)
    yc = linear(att, p["w_mla_out"], tag + "mla_out")

    pd = dwconv(pool_u, pool_mask, tag + "pool") * pool_inv_cnt - pool_u
    yd = linear(pd, _block_diag(p["w_pool"]), tag + "pool_mix") * p["pool_scale"][None, :]
    yd = linear(yd, p["w_pool_out"], tag + "pool_out")

    gt = jax.nn.sigmoid(gates)
    merged = (gt[:, 0:d] * ya + gt[:, d:2 * d] * yb + gt[:, 2 * d:3 * d] * yc + gt[:, 3 * d:4 * d] * yd)
    mix = linear(merged, p["w_o"], tag + "o")
    x = layer_norm(ALPHA * x + (1.0 + g1) * mix, p["ln1_g"], p["ln1_b"], tag + "ln1")

    h = x * (1.0 + sc2) + sh2
    up = dwconv(linear(h, p["w_up"], tag + "up"), p["ffn_dw"], tag + "ffn_conv")
    d_ff = up.shape[1] // 2
    val, gate = up[:, :d_ff], up[:, d_ff:]
    ffn = linear(jax.nn.silu(gate) * val, p["w_down"], tag + "down")
    return layer_norm(ALPHA * x + (1.0 + g2) * ffn, p["ln2_g"], p["ln2_b"], tag + "ln2")


def _forward(x, layers, c_act, cos, sin, pool_mask, pool_inv_cnt):
    for l, p in enumerate(layers):
        x = _layer(x, c_act, p, cos, sin, pool_mask, pool_inv_cnt, "L%d_" % l)
    return x


def kernel(x, c, positions, w_ada, b_ada, w_in, b_in, conv_dw, conv_ln_g, conv_ln_b, w_conv_out, sc_dw, w_sc_out, q_norm_g, w_uq, kv_norm_g, w_ukv, w_mla_out, w_pool, pool_scale, w_pool_out, w_o, ln1_g, ln1_b, w_up, ffn_dw, w_down, ln2_g, ln2_b, loss_target, m_w_ada, m_b_ada, m_w_in, m_b_in, m_conv_dw, m_conv_ln_g, m_conv_ln_b, m_w_conv_out, m_sc_dw, m_w_sc_out, m_q_norm_g, m_w_uq, m_kv_norm_g, m_w_ukv, m_w_mla_out, m_w_pool, m_pool_scale, m_w_pool_out, m_w_o, m_ln1_g, m_ln1_b, m_w_up, m_ffn_dw, m_w_down, m_ln2_g, m_ln2_b, v_w_ada, v_b_ada, v_w_in, v_b_in, v_conv_dw, v_conv_ln_g, v_conv_ln_b, v_w_conv_out, v_sc_dw, v_w_sc_out, v_q_norm_g, v_w_uq, v_kv_norm_g, v_w_ukv, v_w_mla_out, v_w_pool, v_pool_scale, v_w_pool_out, v_w_o, v_ln1_g, v_ln1_b, v_w_up, v_ffn_dw, v_w_down, v_ln2_g, v_ln2_b):
    given = dict(locals())
    weights = {n: given[n] for n in WEIGHT_ORDER}
    mom_m = {n: given["m_" + n] for n in WEIGHT_ORDER}
    mom_v = {n: given["v_" + n] for n in WEIGHT_ORDER}
    depth = w_ada.shape[0]
    seq = x.shape[1]
    sharded_names = [s[0] for s in SHARDED]

    layers = []
    for l in range(depth):
        packed = _pack_weights_bf16({n: weights[n][l] for n in sharded_names})
        gathered = _exchange(packed, group="chips", broadcast=True, name="gather_weights")
        full = _unpack_weights(gathered)
        for n in REPLICATED:
            full[n] = weights[n][l]
        layers.append(full)

    inv = 1.0 / (ROPE_THETA ** (jnp.arange(0, QK_ROPE, 2, dtype=F32) / QK_ROPE))
    ang = positions[0].astype(F32)[:, None] * inv
    cos, sin = jnp.cos(ang), jnp.sin(ang)
    c_act = jnp.pad(jax.nn.silu(c), ((0, 15), (0, 0)))
    pool_mask, pool_inv_cnt = _pool_constants(seq)

    y, vjp_fn = jax.vjp(lambda xx, ll: _forward(xx, ll, c_act, cos, sin, pool_mask, pool_inv_cnt), x[0], layers)
    loss_local, dy = _loss_call(y, loss_target[0])
    grad_x, grad_layers = vjp_fn(dy)
    loss = lax.psum(loss_local, ("x", "y", "c"))

    grad_sh = []
    for l in range(depth):
        reduced = _reduce_scatter(_pack_grads(grad_layers[l]), tag="rs")
        grad_sh.append(reduced)
    rep_like = {n: weights[n] for n in REPLICATED}
    rep_local = _pack_replicated({n: jnp.stack([grad_layers[l][n] for l in range(depth)]) for n in REPLICATED})
    rep_all = _exchange(rep_local, group="all", broadcast=True, name="gather_small_grads")
    grad_rep = _ordered_sum(rep_all, name="small_grads_sum")

    outs = {"grad": {}, "delta": {}, "m": {}, "v": {}}
    per_layer = {k: [] for k in outs}
    for l in range(depth):
        w_p = _pack_shards_f32({n: weights[n][l] for n in sharded_names})
        m_p = _pack_shards_f32({n: mom_m[n][l] for n in sharded_names})
        v_p = _pack_shards_f32({n: mom_v[n][l] for n in sharded_names})
        delta, new_m, new_v = _adamw_call(w_p, grad_sh[l], m_p, v_p, name="adamw_shards")
        for key, arr in (("grad", grad_sh[l]), ("delta", delta), ("m", new_m), ("v", new_v)):
            per_layer[key].append(_unpack_shards(arr))
    for key in outs:
        for n in sharded_names:
            outs[key][n] = jnp.stack([per_layer[key][l][n] for l in range(depth)])
    w_r = _pack_replicated(rep_like)
    m_r = _pack_replicated({n: mom_m[n] for n in REPLICATED})
    v_r = _pack_replicated({n: mom_v[n] for n in REPLICATED})
    delta_r, new_m_r, new_v_r = _adamw_call(w_r, grad_rep, m_r, v_r, name="adamw_small")
    for key, arr in (("grad", grad_rep), ("delta", delta_r), ("m", new_m_r), ("v", new_v_r)):
        outs[key].update(_unpack_replicated(arr, rep_like))

    result = [loss, grad_x[None]]
    for key in ("grad", "delta", "m", "v"):
        result.extend(outs[key][n] for n in WEIGHT_ORDER)
    return tuple(result)
```

```python
import functools
import math

import jax
import jax.numpy as jnp
import numpy as np
from jax import lax
from jax.experimental import pallas as pl
from jax.experimental.pallas import tpu as pltpu

F32 = jnp.float32
BF16 = jnp.bfloat16
MXU = jnp.bfloat16

D_MODEL = 1024
N_HEADS = 8
QK_NOPE = 64
QK_ROPE = 32
V_DIM = 64
HEAD_PAD = 128
ROPE_THETA = 10000.0
POOL_WINDOWS = (2, 4, 8, 16)
LN_EPS = 1e-5
RMS_EPS = 1e-6
DEPTH_FOR_DEEPNORM = 4
ALPHA = (2.0 * DEPTH_FOR_DEEPNORM) ** 0.25
ATTN_SCALE = (QK_NOPE + QK_ROPE) ** -0.5
LOG2_E = math.log2(math.e)
IN_COLS = 7584
IN_PAD_AT = 2976
IN_PAD = 96
ADAM_LR, ADAM_B1, ADAM_B2, ADAM_EPS, ADAM_WD, ADAM_STEP = 0.001, 0.9, 0.999, 1e-08, 0.01, 10

VMEM_LIMIT = 48 * 1024 * 1024
LANE = 128
CONV_HALO = 32
PACK_W = 512
SHARD_ROW_QUANTUM = 256
MESH_T = pl.DeviceIdType.MESH

SHARDED = (
    ("w_ada", (1024, 6144), 1, True),
    ("w_in", (1024, IN_COLS), 1, True),
    ("conv_dw", (31, 512), 1, False),
    ("w_conv_out", (512, 1024), 1, True),
    ("sc_dw", (3, 512), 1, False),
    ("w_sc_out", (512, 1024), 1, True),
    ("w_uq", (256, 768), 1, True),
    ("w_ukv", (128, 1024), 1, True),
    ("w_mla_out", (512, 1024), 1, True),
    ("w_pool_out", (512, 1024), 1, True),
    ("w_o", (1024, 1024), 0, True),
    ("w_up", (1024, 5632), 1, True),
    ("ffn_dw", (3, 5632), 1, False),
    ("w_down", (2816, 1024), 0, True),
)
REPLICATED = ("b_ada", "b_in", "conv_ln_g", "conv_ln_b", "q_norm_g", "kv_norm_g", "w_pool", "pool_scale",
              "ln1_g", "ln1_b", "ln2_g", "ln2_b")
WEIGHT_ORDER = ("w_ada", "b_ada", "w_in", "b_in", "conv_dw", "conv_ln_g", "conv_ln_b", "w_conv_out", "sc_dw",
                "w_sc_out", "q_norm_g", "w_uq", "kv_norm_g", "w_ukv", "w_mla_out", "w_pool", "pool_scale",
                "w_pool_out", "w_o", "ln1_g", "ln1_b", "w_up", "ffn_dw", "w_down", "ln2_g", "ln2_b")


def _pick(dim, cap, quantum=LANE):
    best = None
    t = quantum
    while t <= min(dim, cap):
        if dim % t == 0:
            best = t
        t += quantum
    return dim if best is None else best


def _params(n_grid):
    return pltpu.CompilerParams(dimension_semantics=("arbitrary",) * n_grid, vmem_limit_bytes=VMEM_LIMIT)


def _mm(a, b, *, ta=False, tb=False, bias=None, name):
    if ta:
        k_dim, m_dim = a.shape
    else:
        m_dim, k_dim = a.shape
    if tb:
        n_dim, k2 = b.shape
    else:
        k2, n_dim = b.shape
    assert k_dim == k2, (a.shape, b.shape, ta, tb)
    tm = _pick(m_dim, 1024)
    tn = _pick(n_dim, 1024)
    tk = _pick(k_dim, 1024)
    nk = k_dim // tk
    a_spec = pl.BlockSpec((tk, tm), lambda i, j, k: (k, i)) if ta else pl.BlockSpec((tm, tk), lambda i, j, k: (i, k))
    b_spec = pl.BlockSpec((tn, tk), lambda i, j, k: (j, k)) if tb else pl.BlockSpec((tk, tn), lambda i, j, k: (k, j))
    dims = (((0 if ta else 1,), (1 if tb else 0,)), ((), ()))
    has_bias = bias is not None

    def body(*refs):
        if has_bias:
            a_ref, b_ref, bias_ref, o_ref, acc_ref = refs
        else:
            a_ref, b_ref, o_ref, acc_ref = refs
        k = pl.program_id(2)

        @pl.when(k == 0)
        def _():
            acc_ref[...] = jnp.zeros_like(acc_ref)

        acc_ref[...] += lax.dot_general(a_ref[...].astype(MXU), b_ref[...].astype(MXU), dims,
                                        preferred_element_type=F32)

        @pl.when(k == nk - 1)
        def _():
            out = acc_ref[...]
            if has_bias:
                out = out + bias_ref[...]
            o_ref[...] = out

    in_specs = [a_spec, b_spec]
    args = [a, b]
    if has_bias:
        in_specs.append(pl.BlockSpec((1, tn), lambda i, j, k: (0, j)))
        args.append(bias.reshape(1, n_dim))
    return pl.pallas_call(
        body, name=name, grid=(m_dim // tm, n_dim // tn, nk), in_specs=in_specs,
        out_specs=pl.BlockSpec((tm, tn), lambda i, j, k: (i, j)),
        out_shape=jax.ShapeDtypeStruct((m_dim, n_dim), F32),
        scratch_shapes=[pltpu.VMEM((tm, tn), F32)], compiler_params=_params(3),
    )(*args)


def _colsum(x, *, name):
    rows, n = x.shape
    tr = _pick(rows, 1024, 8)
    tn = _pick(n, 1024)

    def body(x_ref, o_ref):
        @pl.when(pl.program_id(1) == 0)
        def _():
            o_ref[...] = jnp.zeros_like(o_ref)

        o_ref[...] += jnp.sum(x_ref[...], axis=0, keepdims=True)

    out = pl.pallas_call(
        body, name=name, grid=(n // tn, rows // tr), in_specs=[pl.BlockSpec((tr, tn), lambda j, i: (i, j))],
        out_specs=pl.BlockSpec((1, tn), lambda j, i: (0, j)), out_shape=jax.ShapeDtypeStruct((1, n), F32),
        compiler_params=_params(2),
    )(x)
    return out[0]


@functools.partial(jax.custom_vjp, nondiff_argnums=(2,))
def linear(x, w, name):
    return _mm(x, w, name=name + "_f")


def _linear_fwd(x, w, name):
    return _mm(x, w, name=name + "_f"), (x, w)


def _linear_bwd(name, res, dy):
    x, w = res
    return _mm(dy, w, tb=True, name=name + "_dx"), _mm(x, dy, ta=True, name=name + "_dw")


linear.defvjp(_linear_fwd, _linear_bwd)


@functools.partial(jax.custom_vjp, nondiff_argnums=(3,))
def linear_bias(x, w, b, name):
    return _mm(x, w, bias=b, name=name + "_f")


def _linear_bias_fwd(x, w, b, name):
    return _mm(x, w, bias=b, name=name + "_f"), (x, w)


def _linear_bias_bwd(name, res, dy):
    x, w = res
    return (_mm(dy, w, tb=True, name=name + "_dx"), _mm(x, dy, ta=True, name=name + "_dw"),
            _colsum(dy, name=name + "_db"))


linear_bias.defvjp(_linear_bias_fwd, _linear_bias_bwd)


def _norm_stats(x, center, eps):
    if center:
        mu = jnp.mean(x, axis=-1, keepdims=True)
        xc = x - mu
    else:
        xc = x
    rstd = lax.rsqrt(jnp.mean(xc * xc, axis=-1, keepdims=True) + eps)
    return xc * rstd


def _norm_fwd_call(x, g, b, *, center, name):
    rows, d = x.shape
    tr = _pick(rows, 512, 8)
    eps = LN_EPS if center else RMS_EPS

    def body(x_ref, g_ref, b_ref, o_ref):
        xhat = _norm_stats(x_ref[...], center, eps)
        y = xhat * g_ref[...]
        if center:
            y = y + b_ref[...]
        o_ref[...] = y

    vec = pl.BlockSpec((1, d), lambda i: (0, 0))
    return pl.pallas_call(
        body, name=name, grid=(rows // tr,), in_specs=[pl.BlockSpec((tr, d), lambda i: (i, 0)), vec, vec],
        out_specs=pl.BlockSpec((tr, d), lambda i: (i, 0)), out_shape=jax.ShapeDtypeStruct((rows, d), F32),
        compiler_params=_params(1),
    )(x, g.reshape(1, d), b.reshape(1, d))


def _norm_bwd_call(x, g, dy, *, center, name):
    rows, d = x.shape
    tr = _pick(rows, 512, 8)
    eps = LN_EPS if center else RMS_EPS

    def body(x_ref, g_ref, dy_ref, dx_ref, dg_ref, db_ref):
        @pl.when(pl.program_id(0) == 0)
        def _():
            dg_ref[...] = jnp.zeros_like(dg_ref)
            db_ref[...] = jnp.zeros_like(db_ref)

        x = x_ref[...]
        dy = dy_ref[...]
        if center:
            mu = jnp.mean(x, axis=-1, keepdims=True)
            xc = x - mu
        else:
            xc = x
        rstd = lax.rsqrt(jnp.mean(xc * xc, axis=-1, keepdims=True) + eps)
        xhat = xc * rstd
        dyg = dy * g_ref[...]
        proj = jnp.mean(dyg * xhat, axis=-1, keepdims=True)
        dx = dyg - xhat * proj
        if center:
            dx = dx - jnp.mean(dyg, axis=-1, keepdims=True)
        dx_ref[...] = dx * rstd
        dg_ref[...] += jnp.sum(dy * xhat, axis=0, keepdims=True)
        db_ref[...] += jnp.sum(dy, axis=0, keepdims=True)

    vec = pl.BlockSpec((1, d), lambda i: (0, 0))
    row = pl.BlockSpec((tr, d), lambda i: (i, 0))
    dx, dg, db = pl.pallas_call(
        body, name=name, grid=(rows // tr,), in_specs=[row, vec, row], out_specs=[row, vec, vec],
        out_shape=[jax.ShapeDtypeStruct((rows, d), F32), jax.ShapeDtypeStruct((1, d), F32),
                   jax.ShapeDtypeStruct((1, d), F32)],
        compiler_params=_params(1),
    )(x, g.reshape(1, d), dy)
    return dx, dg[0], db[0]


@functools.partial(jax.custom_vjp, nondiff_argnums=(3,))
def layer_norm(x, g, b, name):
    return _norm_fwd_call(x, g, b, center=True, name=name + "_f")


def _layer_norm_fwd(x, g, b, name):
    return _norm_fwd_call(x, g, b, center=True, name=name + "_f"), (x, g)


def _layer_norm_bwd(name, res, dy):
    x, g = res
    return _norm_bwd_call(x, g, dy, center=True, name=name + "_b")


layer_norm.defvjp(_layer_norm_fwd, _layer_norm_bwd)


@functools.partial(jax.custom_vjp, nondiff_argnums=(2,))
def rms_norm(x, g, name):
    return _norm_fwd_call(x, g, jnp.zeros_like(g), center=False, name=name + "_f")


def _rms_norm_fwd(x, g, name):
    return _norm_fwd_call(x, g, jnp.zeros_like(g), center=False, name=name + "_f"), (x, g)


def _rms_norm_bwd(name, res, dy):
    x, g = res
    dx, dg, _ = _norm_bwd_call(x, g, dy, center=False, name=name + "_b")
    return dx, dg


rms_norm.defvjp(_rms_norm_fwd, _rms_norm_bwd)


def _conv_tiles(rows, ch):
    tr = _pick(rows, 512, CONV_HALO)
    assert tr >= CONV_HALO and rows % tr == 0
    return tr, _pick(ch, 512)


def _pad_taps(w):
    k = w.shape[0]
    kp = -(-k // 8) * 8
    return jnp.pad(w, ((0, kp - k), (0, 0))), k, kp


def _dwconv_fwd_call(x, w, *, name):
    rows, ch = x.shape
    tr, tc = _conv_tiles(rows, ch)
    wp, taps, kp = _pad_taps(w)
    assert taps - 1 <= CONV_HALO
    halo_per_tile = tr // CONV_HALO

    def body(x_ref, xprev_ref, w_ref, o_ref):
        i = pl.program_id(1)
        halo = xprev_ref[...]
        halo = jnp.where(i > 0, halo, jnp.zeros_like(halo))
        xx = jnp.concatenate([halo, x_ref[...]], axis=0)
        acc = jnp.zeros((tr, tc), F32)
        for k in range(taps):
            shift = taps - 1 - k
            term = xx if shift == 0 else pltpu.roll(xx, shift, 0)
            acc = acc + w_ref[k:k + 1, :] * term[CONV_HALO:, :]
        o_ref[...] = acc

    return pl.pallas_call(
        body, name=name, grid=(ch // tc, rows // tr),
        in_specs=[pl.BlockSpec((tr, tc), lambda j, i: (i, j)),
                  pl.BlockSpec((CONV_HALO, tc), lambda j, i: (jnp.maximum(i * halo_per_tile - 1, 0), j)),
                  pl.BlockSpec((kp, tc), lambda j, i: (0, j))],
        out_specs=pl.BlockSpec((tr, tc), lambda j, i: (i, j)), out_shape=jax.ShapeDtypeStruct((rows, ch), F32),
        compiler_params=_params(2),
    )(x, x, wp)


def _dwconv_bwd_call(x, w, dy, *, name):
    rows, ch = x.shape
    tr, tc = _conv_tiles(rows, ch)
    wp, taps, kp = _pad_taps(w)
    n_row_tiles = rows // tr
    halo_per_tile = tr // CONV_HALO
    n_halo_blocks = rows // CONV_HALO
    ext = tr + CONV_HALO

    def body(x_ref, xprev_ref, dy_ref, dynext_ref, w_ref, dx_ref, dw_ref):
        i = pl.program_id(1)

        @pl.when(i == 0)
        def _():
            dw_ref[...] = jnp.zeros_like(dw_ref)

        halo = xprev_ref[...]
        halo = jnp.where(i > 0, halo, jnp.zeros_like(halo))
        xx = jnp.concatenate([halo, x_ref[...]], axis=0)
        dy = dy_ref[...]
        ahead = dynext_ref[...]
        ahead = jnp.where(i < n_row_tiles - 1, ahead, jnp.zeros_like(ahead))
        yy = jnp.concatenate([dy, ahead], axis=0)
        dx = jnp.zeros((tr, tc), F32)
        for k in range(taps):
            shift = taps - 1 - k
            fwd = yy if shift == 0 else pltpu.roll(yy, ext - shift, 0)
            dx = dx + w_ref[k:k + 1, :] * fwd[:tr, :]
            back = xx if shift == 0 else pltpu.roll(xx, shift, 0)
            dw_ref[k:k + 1, :] += jnp.sum(dy * back[CONV_HALO:, :], axis=0, keepdims=True)
        dx_ref[...] = dx

    cur = pl.BlockSpec((tr, tc), lambda j, i: (i, j))
    dx, dw = pl.pallas_call(
        body, name=name, grid=(ch // tc, n_row_tiles),
        in_specs=[cur, pl.BlockSpec((CONV_HALO, tc), lambda j, i: (jnp.maximum(i * halo_per_tile - 1, 0), j)), cur,
                  pl.BlockSpec((CONV_HALO, tc),
                               lambda j, i: (jnp.minimum((i + 1) * halo_per_tile, n_halo_blocks - 1), j)),
                  pl.BlockSpec((kp, tc), lambda j, i: (0, j))],
        out_specs=[cur, pl.BlockSpec((kp, tc), lambda j, i: (0, j))],
        out_shape=[jax.ShapeDtypeStruct((rows, ch), F32), jax.ShapeDtypeStruct((kp, ch), F32)],
        compiler_params=_params(2),
    )(x, x, dy, dy, wp)
    return dx, dw[:taps]


@functools.partial(jax.custom_vjp, nondiff_argnums=(2,))
def dwconv(x, w, name):
    return _dwconv_fwd_call(x, w, name=name + "_f")


def _dwconv_fwd(x, w, name):
    return _dwconv_fwd_call(x, w, name=name + "_f"), (x, w)


def _dwconv_bwd(name, res, dy):
    x, w = res
    return _dwconv_bwd_call(x, w, dy, name=name + "_b")


dwconv.defvjp(_dwconv_fwd, _dwconv_bwd)


def _attn_tiles(seq):
    return _pick(seq, 1024), _pick(seq, 512)


def _lane_tile(v, width):
    return v if width == LANE else jnp.tile(v, (1, width // LANE))


def _causal_pairs(nq, nk, tq, tk, key_major):
    pairs = [(i, j) for i in range(nq) for j in range(nk) if j * tk <= i * tq + tq - 1]
    if key_major:
        pairs.sort(key=lambda p: (p[1], p[0]))
    return (jnp.asarray(np.array([p[0] for p in pairs], np.int32)),
            jnp.asarray(np.array([p[1] for p in pairs], np.int32)))


def _scores(q, k, i, j, tq, tk, masked):
    z = lax.dot_general(q, k, (((1,), (1,)), ((), ())), preferred_element_type=F32) * (ATTN_SCALE * LOG2_E)
    if masked:
        row = i * tq + lax.broadcasted_iota(jnp.int32, (tq, tk), 0)
        col = j * tk + lax.broadcasted_iota(jnp.int32, (tq, tk), 1)
        z = jnp.where(col <= row, z, -jnp.inf)
    return z


def _attn_fwd_call(q, k, v, *, name):
    heads, seq, dh = q.shape
    tq, tk = _attn_tiles(seq)
    nq, nk = seq // tq, seq // tk
    qi, kj = _causal_pairs(nq, nk, tq, tk, key_major=False)

    def body(qi_ref, kj_ref, q_ref, k_ref, v_ref, o_ref, lse_ref, m_sc, l_sc, acc_sc):
        t = pl.program_id(1)
        i = qi_ref[t]
        j = kj_ref[t]

        @pl.when(j == 0)
        def _():
            m_sc[...] = jnp.full(m_sc.shape, -jnp.inf, F32)
            l_sc[...] = jnp.zeros_like(l_sc)
            acc_sc[...] = jnp.zeros_like(acc_sc)

        def step(masked):
            z = _scores(q_ref[0], k_ref[0], i, j, tq, tk, masked)
            m_prev = m_sc[...]
            m_new = jnp.maximum(m_prev, jnp.max(z, axis=1, keepdims=True))
            alpha = jnp.exp2(m_prev - m_new)
            p = jnp.exp2(z - _lane_tile(m_new, tk))
            l_sc[...] = alpha * l_sc[...] + jnp.sum(p, axis=1, keepdims=True)
            acc_sc[...] = alpha * acc_sc[...] + jnp.dot(p.astype(MXU), v_ref[0], preferred_element_type=F32)
            m_sc[...] = m_new

        crosses_diagonal = j * tk + tk - 1 > i * tq

        @pl.when(jnp.logical_not(crosses_diagonal))
        def _():
            step(False)

        @pl.when(crosses_diagonal)
        def _():
            step(True)

        @pl.when(j == (i * tq + tq - 1) // tk)
        def _():
            o_ref[0] = acc_sc[...] / l_sc[...]
            lse_ref[0] = m_sc[...] + jnp.log(l_sc[...]) * LOG2_E

    q_spec = pl.BlockSpec((1, tq, dh), lambda h, t, qi_ref, kj_ref: (h, qi_ref[t], 0))
    kv_spec = pl.BlockSpec((1, tk, dh), lambda h, t, qi_ref, kj_ref: (h, kj_ref[t], 0))
    return pl.pallas_call(
        body, name=name,
        grid_spec=pltpu.PrefetchScalarGridSpec(
            num_scalar_prefetch=2, grid=(heads, int(qi.shape[0])), in_specs=[q_spec, kv_spec, kv_spec],
            out_specs=[q_spec, q_spec],
            scratch_shapes=[pltpu.VMEM((tq, LANE), F32), pltpu.VMEM((tq, LANE), F32), pltpu.VMEM((tq, dh), F32)]),
        out_shape=[jax.ShapeDtypeStruct((heads, seq, dh), F32), jax.ShapeDtypeStruct((heads, seq, LANE), F32)],
        compiler_params=_params(2),
    )(qi, kj, q, k, v)


def _attn_bwd_call(q, k, v, o, lse, do, *, name):
    heads, seq, dh = q.shape
    tk, tq = _attn_tiles(seq)
    nq, nk = seq // tq, seq // tk
    qi, kj = _causal_pairs(nq, nk, tq, tk, key_major=True)

    def body(qi_ref, kj_ref, q_ref, k_ref, v_ref, o_ref, lse_ref, do_ref, dq_ref, dk_ref, dv_ref, dk_sc, dv_sc):
        t = pl.program_id(1)
        i = qi_ref[t]
        j = kj_ref[t]
        first_i = (j * tk) // tq

        @pl.when(t == 0)
        def _():
            dq_ref[...] = jnp.zeros_like(dq_ref)

        @pl.when(i == first_i)
        def _():
            dk_sc[...] = jnp.zeros_like(dk_sc)
            dv_sc[...] = jnp.zeros_like(dv_sc)

        def step(masked):
            qb = q_ref[0]
            kb = k_ref[0]
            do_f = do_ref[0]
            do_b = do_f.astype(MXU)
            p = jnp.exp2(_scores(qb, kb, i, j, tq, tk, masked) - _lane_tile(lse_ref[0], tk))
            dp = lax.dot_general(do_b, v_ref[0], (((1,), (1,)), ((), ())), preferred_element_type=F32)
            delta = jnp.sum(do_f * o_ref[0], axis=1, keepdims=True)
            ds = (p * (dp - delta) * ATTN_SCALE).astype(MXU)
            dv_sc[...] += lax.dot_general(p.astype(MXU), do_b, (((0,), (0,)), ((), ())), preferred_element_type=F32)
            dk_sc[...] += lax.dot_general(ds, qb, (((0,), (0,)), ((), ())), preferred_element_type=F32)
            rows = pl.ds(pl.multiple_of(i * tq, tq), tq)
            dq_ref[0, rows, :] += jnp.dot(ds, kb, preferred_element_type=F32)

        crosses_diagonal = j * tk + tk - 1 > i * tq

        @pl.when(jnp.logical_not(crosses_diagonal))
        def _():
            step(False)

        @pl.when(crosses_diagonal)
        def _():
            step(True)

        @pl.when(i == nq - 1)
        def _():
            dk_ref[0] = dk_sc[...]
            dv_ref[0] = dv_sc[...]

    def q_map(h, t, qi_ref, kj_ref):
        return (h, qi_ref[t], 0)

    def kv_map(h, t, qi_ref, kj_ref):
        return (h, kj_ref[t], 0)

    q_spec = pl.BlockSpec((1, tq, dh), q_map)
    lse_spec = pl.BlockSpec((1, tq, LANE), q_map)
    kv_spec = pl.BlockSpec((1, tk, dh), kv_map)
    head_spec = pl.BlockSpec((1, seq, dh), lambda h, t, qi_ref, kj_ref: (h, 0, 0))
    return pl.pallas_call(
        body, name=name,
        grid_spec=pltpu.PrefetchScalarGridSpec(
            num_scalar_prefetch=2, grid=(heads, int(qi.shape[0])),
            in_specs=[q_spec, kv_spec, kv_spec, q_spec, lse_spec, q_spec], out_specs=[head_spec, kv_spec, kv_spec],
            scratch_shapes=[pltpu.VMEM((tk, dh), F32), pltpu.VMEM((tk, dh), F32)]),
        out_shape=[jax.ShapeDtypeStruct((heads, seq, dh), F32)] * 3,
        compiler_params=_params(2),
    )(qi, kj, q, k, v, o, lse, do)


@functools.partial(jax.custom_vjp, nondiff_argnums=(3,))
def attention(q, k, v, name):
    return _attn_fwd_call(q.astype(MXU), k.astype(MXU), v.astype(MXU), name=name + "_f")[0]


def _attention_fwd(q, k, v, name):
    qb, kb, vb = q.astype(MXU), k.astype(MXU), v.astype(MXU)
    o, lse = _attn_fwd_call(qb, kb, vb, name=name + "_f")
    return o, (qb, kb, vb, o, lse)


def _attention_bwd(name, res, do):
    qb, kb, vb, o, lse = res
    return tuple(_attn_bwd_call(qb, kb, vb, o, lse, do, name=name + "_b"))


attention.defvjp(_attention_fwd, _attention_bwd)


def _merge_tiles(rows):
    return _pick(rows, 128, 8)


def _gate_merge_fwd_call(gates, ys, *, name):
    rows, d = ys[0].shape
    nb = len(ys)
    tr = _merge_tiles(rows)

    def body(g_ref, *refs):
        o_ref = refs[nb]
        acc = jax.nn.sigmoid(g_ref[:, 0:d]) * refs[0][...]
        for b in range(1, nb):
            acc = acc + jax.nn.sigmoid(g_ref[:, b * d:(b + 1) * d]) * refs[b][...]
        o_ref[...] = acc

    wide = pl.BlockSpec((tr, nb * d), lambda i: (i, 0))
    row = pl.BlockSpec((tr, d), lambda i: (i, 0))
    return pl.pallas_call(
        body, name=name, grid=(rows // tr,), in_specs=[wide] + [row] * nb, out_specs=row,
        out_shape=jax.ShapeDtypeStruct((rows, d), F32), compiler_params=_params(1),
    )(gates, *ys)


def _gate_merge_bwd_call(gates, ys, dm, *, name):
    rows, d = ys[0].shape
    nb = len(ys)
    tr = _merge_tiles(rows)

    def body(g_ref, *refs):
        y_refs = refs[:nb]
        dm_ref = refs[nb]
        dg_ref = refs[nb + 1]
        dy_refs = refs[nb + 2:]
        dm = dm_ref[...]
        for b in range(nb):
            gt = jax.nn.sigmoid(g_ref[:, b * d:(b + 1) * d])
            dy_refs[b][...] = gt * dm
            dg_ref[:, b * d:(b + 1) * d] = dm * y_refs[b][...] * gt * (1.0 - gt)

    wide = pl.BlockSpec((tr, nb * d), lambda i: (i, 0))
    row = pl.BlockSpec((tr, d), lambda i: (i, 0))
    outs = pl.pallas_call(
        body, name=name, grid=(rows // tr,), in_specs=[wide] + [row] * (nb + 1), out_specs=[wide] + [row] * nb,
        out_shape=[jax.ShapeDtypeStruct((rows, nb * d), F32)] + [jax.ShapeDtypeStruct((rows, d), F32)] * nb,
        compiler_params=_params(1),
    )(gates, *ys, dm)
    return outs[0], tuple(outs[1:])


@functools.partial(jax.custom_vjp, nondiff_argnums=(2,))
def gate_merge(gates, ys, name):
    return _gate_merge_fwd_call(gates, ys, name=name + "_f")


def _gate_merge_fwd(gates, ys, name):
    return _gate_merge_fwd_call(gates, ys, name=name + "_f"), (gates, ys)


def _gate_merge_bwd(name, res, dm):
    gates, ys = res
    return _gate_merge_bwd_call(gates, ys, dm, name=name + "_b")


gate_merge.defvjp(_gate_merge_fwd, _gate_merge_bwd)


def _loss_call(y, target):
    rows, d = y.shape
    tr = _pick(rows, 512, 8)

    def body(y_ref, t_ref, dy_ref, loss_ref):
        @pl.when(pl.program_id(0) == 0)
        def _():
            loss_ref[...] = jnp.zeros_like(loss_ref)

        err = y_ref[...] - t_ref[...]
        dy_ref[...] = err * (1.0 / d)
        loss_ref[...] += jnp.sum(err * err, axis=0, keepdims=True)

    row = pl.BlockSpec((tr, d), lambda i: (i, 0))
    vec = pl.BlockSpec((1, d), lambda i: (0, 0))
    dy, part = pl.pallas_call(
        body, name="loss_head", grid=(rows // tr,), in_specs=[row, row], out_specs=[row, vec],
        out_shape=[jax.ShapeDtypeStruct((rows, d), F32), jax.ShapeDtypeStruct((1, d), F32)],
        compiler_params=_params(1),
    )(y, target)
    return 0.5 * jnp.sum(part) / d, dy


ADAMW_BLOCK_ELEMS = 256 * 1024


def _adamw_call(w, g, m, v, *, name):
    shape = w.shape
    width = shape[-1]
    rows = w.size // width
    w, g, m, v = (a.reshape(rows, width) for a in (w, g, m, v))
    tr = _pick(rows, max(8, ADAMW_BLOCK_ELEMS // width), 8)
    c1 = 1.0 - ADAM_B1 ** ADAM_STEP
    c2 = 1.0 - ADAM_B2 ** ADAM_STEP

    def body(w_ref, g_ref, m_ref, v_ref, d_ref, nm_ref, nv_ref):
        g = g_ref[...]
        m = ADAM_B1 * m_ref[...] + (1.0 - ADAM_B1) * g
        v = ADAM_B2 * v_ref[...] + (1.0 - ADAM_B2) * (g * g)
        m_hat = m / c1
        v_hat = v / c2
        d_ref[...] = -ADAM_LR * (m_hat / (jnp.sqrt(v_hat) + ADAM_EPS) + ADAM_WD * w_ref[...])
        nm_ref[...] = m
        nv_ref[...] = v

    row = pl.BlockSpec((tr, width), lambda i: (i, 0))
    outs = pl.pallas_call(
        body, name=name, grid=(rows // tr,), in_specs=[row] * 4, out_specs=[row] * 3,
        out_shape=[jax.ShapeDtypeStruct((rows, width), F32)] * 3, compiler_params=_params(1),
    )(w, g, m, v)
    return [o.reshape(shape) for o in outs]


def _ordered_sum(y, *, name):
    n, rows, width = y.shape
    tr = _pick(rows, 256, 8)

    def body(y_ref, o_ref):
        acc = y_ref[0]
        for s in range(1, n):
            acc = acc + y_ref[s]
        o_ref[...] = acc

    return pl.pallas_call(
        body, name=name, grid=(rows // tr,), in_specs=[pl.BlockSpec((n, tr, width), lambda i: (0, i, 0))],
        out_specs=pl.BlockSpec((tr, width), lambda i: (i, 0)), out_shape=jax.ShapeDtypeStruct((rows, width), y.dtype),
        compiler_params=_params(1),
    )(y)


_GROUP_FLIPS = {
    "chips": ((1, 0, 0), (0, 1, 0), (1, 1, 0)),
    "cores": ((0, 0, 1),),
    "all": ((0, 0, 1), (0, 1, 0), (0, 1, 1), (1, 0, 0), (1, 0, 1), (1, 1, 0), (1, 1, 1)),
}


def _exchange(x, *, group, broadcast, name):
    flips = _GROUP_FLIPS[group]
    n = len(flips) + 1
    block = x.shape if broadcast else x.shape[1:]
    if not broadcast:
        assert x.shape[0] == n

    def body(x_ref, o_ref, send_sems, recv_sems, local_sem):
        mx, my, mc = lax.axis_index("x"), lax.axis_index("y"), lax.axis_index("c")

        def index(px, py, pc):
            return {"chips": 2 * px + py, "cores": pc, "all": 4 * px + 2 * py + pc}[group]

        def block_for(d):
            return x_ref if broadcast else x_ref.at[d]

        me = index(mx, my, mc)
        mine = pltpu.make_async_copy(block_for(me), o_ref.at[me], local_sem)
        mine.start()
        sends, recvs = [], []
        for k, (fx, fy, fc) in enumerate(flips):
            px = 1 - mx if fx else mx
            py = 1 - my if fy else my
            pc = 1 - mc if fc else mc
            peer = index(px, py, pc)
            sends.append(pltpu.make_async_remote_copy(
                src_ref=block_for(peer), dst_ref=o_ref.at[me], send_sem=send_sems.at[k], recv_sem=recv_sems.at[k],
                device_id=(px, py, pc), device_id_type=MESH_T))
            recvs.append(pltpu.make_async_remote_copy(
                src_ref=block_for(peer), dst_ref=o_ref.at[peer], send_sem=send_sems.at[k], recv_sem=recv_sems.at[k],
                device_id=(px, py, pc), device_id_type=MESH_T))
        for cp in sends:
            cp.start()
        for cp in recvs:
            cp.wait_recv()
        for cp in sends:
            cp.wait_send()
        mine.wait()

    any_spec = pl.BlockSpec(memory_space=pl.ANY)
    return pl.pallas_call(
        body, name=name, in_specs=[any_spec], out_specs=any_spec,
        out_shape=jax.ShapeDtypeStruct((n,) + tuple(block), x.dtype),
        scratch_shapes=[pltpu.SemaphoreType.DMA((n - 1,)), pltpu.SemaphoreType.DMA((n - 1,)), pltpu.SemaphoreType.DMA],
    )(x)


D2D_CHUNKS = 4


def _sibling():
    return (lax.axis_index("x"), lax.axis_index("y"), 1 - lax.axis_index("c"))


def _pair_send(g, *, name):
    n, rows, width = g.shape
    half = rows // 2
    piece = half // D2D_CHUNKS
    assert piece * D2D_CHUNKS == half and piece % 8 == 0

    def body(g_ref, o_ref, send_sems, recv_sems):
        theirs = (1 - lax.axis_index("c")) * half
        copies = []
        for s in range(n):
            for q in range(D2D_CHUNKS):
                k = s * D2D_CHUNKS + q
                copies.append(pltpu.make_async_remote_copy(
                    src_ref=g_ref.at[s, pl.ds(theirs + q * piece, piece), :],
                    dst_ref=o_ref.at[s, pl.ds(q * piece, piece), :],
                    send_sem=send_sems.at[k], recv_sem=recv_sems.at[k], device_id=_sibling(), device_id_type=MESH_T))
        for cp in copies:
            cp.start()
        for cp in copies:
            cp.wait_recv()
        for cp in copies:
            cp.wait_send()

    any_spec = pl.BlockSpec(memory_space=pl.ANY)
    n_copies = n * D2D_CHUNKS
    return pl.pallas_call(
        body, name=name, in_specs=[any_spec], out_specs=any_spec,
        out_shape=jax.ShapeDtypeStruct((n, half, width), g.dtype),
        scratch_shapes=[pltpu.SemaphoreType.DMA((n_copies,)), pltpu.SemaphoreType.DMA((n_copies,))],
    )(g)


def _quad_send(p, *, name):
    flips = _GROUP_FLIPS["chips"]
    n, rows, width = p.shape

    def body(p_ref, o_ref, send_sems, recv_sems):
        mx, my, mc = lax.axis_index("x"), lax.axis_index("y"), lax.axis_index("c")
        copies = []
        for k, (fx, fy, _) in enumerate(flips):
            px = 1 - mx if fx else mx
            py = 1 - my if fy else my
            copies.append(pltpu.make_async_remote_copy(
                src_ref=p_ref.at[2 * px + py], dst_ref=o_ref.at[k], send_sem=send_sems.at[k],
                recv_sem=recv_sems.at[k], device_id=(px, py, mc), device_id_type=MESH_T))
        for cp in copies:
            cp.start()
        for cp in copies:
            cp.wait_recv()
        for cp in copies:
            cp.wait_send()

    any_spec = pl.BlockSpec(memory_space=pl.ANY)
    return pl.pallas_call(
        body, name=name, in_specs=[any_spec], out_specs=any_spec,
        out_shape=jax.ShapeDtypeStruct((n - 1, rows, width), p.dtype),
        scratch_shapes=[pltpu.SemaphoreType.DMA((n - 1,)), pltpu.SemaphoreType.DMA((n - 1,))],
    )(p)


def _share_halves(mine, *, name):
    half, width = mine.shape
    piece = half // D2D_CHUNKS
    assert piece * D2D_CHUNKS == half and piece % 8 == 0

    def body(m_ref, o_ref, send_sems, recv_sems, local_sems):
        mc = lax.axis_index("c")
        remote, local = [], []
        for q in range(D2D_CHUNKS):
            src = m_ref.at[pl.ds(q * piece, piece), :]
            dst = o_ref.at[pl.ds(mc * half + q * piece, piece), :]
            local.append(pltpu.make_async_copy(src, dst, local_sems.at[q]))
            remote.append(pltpu.make_async_remote_copy(
                src_ref=src, dst_ref=dst, send_sem=send_sems.at[q], recv_sem=recv_sems.at[q],
                device_id=_sibling(), device_id_type=MESH_T))
        for cp in remote + local:
            cp.start()
        for q in range(D2D_CHUNKS):
            pltpu.make_async_remote_copy(
                src_ref=m_ref.at[pl.ds(q * piece, piece), :],
                dst_ref=o_ref.at[pl.ds((1 - mc) * half + q * piece, piece), :],
                send_sem=send_sems.at[q], recv_sem=recv_sems.at[q], device_id=_sibling(),
                device_id_type=MESH_T).wait_recv()
        for cp in remote:
            cp.wait_send()
        for cp in local:
            cp.wait()

    any_spec = pl.BlockSpec(memory_space=pl.ANY)
    return pl.pallas_call(
        body, name=name, in_specs=[any_spec], out_specs=any_spec,
        out_shape=jax.ShapeDtypeStruct((2 * half, width), mine.dtype),
        scratch_shapes=[pltpu.SemaphoreType.DMA((D2D_CHUNKS,)), pltpu.SemaphoreType.DMA((D2D_CHUNKS,)),
                        pltpu.SemaphoreType.DMA((D2D_CHUNKS,))],
    )(mine)


def _add_call(terms, out_dtype, *, name):
    rows, width = terms[0].shape
    tr = _pick(rows, 512, 16)

    def body(*refs):
        acc = refs[0][...].astype(F32)
        for r in refs[1:-1]:
            acc = acc + r[...].astype(F32)
        refs[-1][...] = acc.astype(out_dtype)

    row = pl.BlockSpec((tr, width), lambda i: (i, 0))
    return pl.pallas_call(
        body, name=name, grid=(rows // tr,), in_specs=[row] * len(terms), out_specs=row,
        out_shape=jax.ShapeDtypeStruct((rows, width), out_dtype), compiler_params=_params(1),
    )(*terms)


def _reduce_scatter(g, *, tag):
    n, rows, width = g.shape
    half = rows // 2
    mc = lax.axis_index("c")
    chip = 2 * lax.axis_index("x") + lax.axis_index("y")
    from_sibling = _pair_send(g, name=tag + "_pair")
    own = lax.dynamic_slice_in_dim(g, mc * half, half, axis=1)
    pair = _add_call([own.reshape(n * half, width), from_sibling.reshape(n * half, width)], BF16,
                     name=tag + "_pair_sum").reshape(n, half, width)
    others = _quad_send(pair, name=tag + "_quad")
    mine = _add_call([lax.dynamic_index_in_dim(pair, chip, 0, keepdims=False), others[0], others[1], others[2]],
                     F32, name=tag + "_quad_sum")
    return _share_halves(mine, name=tag + "_share")


def _shard_shape(shape, axis):
    s = list(shape)
    s[axis] //= 4
    return tuple(s)


def _round_up(n, q):
    return -(-n // q) * q


def _pack_rows(flat, quantum_rows):
    n = flat.shape[-1]
    total = _round_up(n, PACK_W * quantum_rows)
    pad = [(0, 0)] * (flat.ndim - 1) + [(0, total - n)]
    return jnp.pad(flat, pad).reshape(flat.shape[:-1] + (total // PACK_W, PACK_W))


def _pack_weights_bf16(blocks):
    parts = []
    for name, _, _, as_bf16 in SHARDED:
        a = blocks[name]
        if as_bf16:
            parts.append(a.astype(BF16).reshape(-1))
        else:
            parts.append(lax.bitcast_convert_type(a, BF16).reshape(-1))
    return _pack_rows(jnp.concatenate(parts), 16)


def _unpack_weights(gathered):
    flat = gathered.reshape(4, -1)
    out, off = {}, 0
    for name, shape, axis, as_bf16 in SHARDED:
        sshape = _shard_shape(shape, axis)
        n = sshape[0] * sshape[1]
        if as_bf16:
            seg = flat[:, off:off + n].reshape((4,) + sshape).astype(F32)
            off += n
        else:
            seg = lax.bitcast_convert_type(flat[:, off:off + 2 * n].reshape((4,) + sshape + (2,)), F32)
            off += 2 * n
        out[name] = jnp.concatenate([seg[s] for s in range(4)], axis=axis)
    return out


def _pack_grads(full):
    per_chip = []
    for s in range(4):
        parts = []
        for name, shape, axis, _ in SHARDED:
            width = shape[axis] // 4
            parts.append(lax.slice_in_dim(full[name], s * width, (s + 1) * width, axis=axis).reshape(-1))
        per_chip.append(jnp.concatenate(parts))
    return _pack_rows(jnp.stack(per_chip), SHARD_ROW_QUANTUM)


def _unpack_shards(packed):
    flat = packed.reshape(-1)
    out, off = {}, 0
    for name, shape, axis, _ in SHARDED:
        sshape = _shard_shape(shape, axis)
        n = sshape[0] * sshape[1]
        out[name] = flat[off:off + n].reshape(sshape)
        off += n
    return out


def _pack_replicated(tree):
    return _pack_rows(jnp.concatenate([tree[name].reshape(-1) for name in REPLICATED]), 8)


def _unpack_replicated(packed, like):
    flat = packed.reshape(-1)
    out, off = {}, 0
    for name in REPLICATED:
        n = like[name].size
        out[name] = flat[off:off + n].reshape(like[name].shape)
        off += n
    return out


def _rope(t, cos, sin):
    half = QK_ROPE // 2
    t1, t2 = t[..., :half], t[..., half:]
    return jnp.concatenate([t1 * cos - t2 * sin, t2 * cos + t1 * sin], axis=-1)


def _pad_in_cols(a):
    z = jnp.zeros(a.shape[:-1] + (IN_PAD,), a.dtype)
    return jnp.concatenate([a[..., :IN_PAD_AT], z, a[..., IN_PAD_AT:]], axis=-1)


def _pool_constants(seq):
    taps = max(POOL_WINDOWS)
    win = jnp.repeat(jnp.asarray(POOL_WINDOWS, jnp.int32), 512 // len(POOL_WINDOWS))
    lag = taps - 1 - jnp.arange(taps, dtype=jnp.int32)
    mask = (lag[:, None] < win[None, :]).astype(F32)
    cnt = jnp.minimum(jnp.arange(seq, dtype=jnp.int32)[:, None] + 1, win[None, :]).astype(F32)
    return mask, 1.0 / cnt


def _block_diag(w):
    g, n, _ = w.shape
    out = jnp.zeros((g * n, g * n), w.dtype)
    for i in range(g):
        out = lax.dynamic_update_slice(out, w[i], (i * n, i * n))
    return out


def _layer(x, c_act, p, cos, sin, pool_mask, pool_inv_cnt, tag):
    seq = x.shape[0]
    d = D_MODEL
    mod = linear_bias(c_act, p["w_ada"], p["b_ada"], tag + "ada")[0]
    sh1, sc1, g1, sh2, sc2, g2 = [mod[i * d:(i + 1) * d][None, :] for i in range(6)]

    h = x * (1.0 + sc1) + sh1
    proj = linear_bias(h, _pad_in_cols(p["w_in"]), _pad_in_cols(p["b_in"]), tag + "in")
    conv_a, conv_b = proj[:, 0:512], proj[:, 512:1024]
    sc_bg, sc_cg, sc_x = proj[:, 1024:1536], proj[:, 1536:2048], proj[:, 2048:2560]
    q_lat, kv_lat, k_rope = proj[:, 2560:2816], proj[:, 2816:2944], proj[:, 2944:2976]
    pool_u, gates = proj[:, 3072:3584], proj[:, 3584:7680]

    ya = conv_a * jax.nn.sigmoid(conv_b)
    ya = dwconv(ya, p["conv_dw"], tag + "convA")
    ya = jax.nn.silu(layer_norm(ya, p["conv_ln_g"], p["conv_ln_b"], tag + "convA_ln"))
    ya = linear(ya, p["w_conv_out"], tag + "convA_out")

    yb = linear(sc_bg * dwconv(sc_cg * sc_x, p["sc_dw"], tag + "sc"), p["w_sc_out"], tag + "sc_out")

    q = linear(rms_norm(q_lat, p["q_norm_g"], tag + "q_rms"), p["w_uq"], tag + "uq")
    q = q.reshape(seq, N_HEADS, QK_NOPE + QK_ROPE)
    kv = linear(rms_norm(kv_lat, p["kv_norm_g"], tag + "kv_rms"), p["w_ukv"], tag + "ukv")
    kv = kv.reshape(seq, N_HEADS, QK_NOPE + V_DIM)
    q_rope = _rope(q[..., QK_NOPE:], cos[:, None, :], sin[:, None, :])
    k_rope_r = jnp.broadcast_to(_rope(k_rope, cos, sin)[:, None, :], (seq, N_HEADS, QK_ROPE))
    zq = jnp.zeros((seq, N_HEADS, HEAD_PAD - QK_NOPE - QK_ROPE), F32)
    zv = jnp.zeros((seq, N_HEADS, HEAD_PAD - V_DIM), F32)
    qh = jnp.concatenate([q[..., :QK_NOPE], q_rope, zq], axis=-1).transpose(1, 0, 2)
    kh = jnp.concatenate([kv[..., :QK_NOPE], k_rope_r, zq], axis=-1).transpose(1, 0, 2)
    vh = jnp.concatenate([kv[..., QK_NOPE:], zv], axis=-1).transpose(1, 0, 2)
    att = attention(qh, kh, vh, tag + "attn")
    att = att[:, :, :V_DIM].transpose(1, 0, 2).reshape(seq, N_HEADS * V_DIM)
    yc = linear(att, p["w_mla_out"], tag + "mla_out")

    pd = dwconv(pool_u, pool_mask, tag + "pool") * pool_inv_cnt - pool_u
    yd = linear(pd, _block_diag(p["w_pool"]), tag + "pool_mix") * p["pool_scale"][None, :]
    yd = linear(yd, p["w_pool_out"], tag + "pool_out")

    merged = gate_merge(gates, (ya, yb, yc, yd), tag + "merge")
    mix = linear(merged, p["w_o"], tag + "o")
    x = layer_norm(ALPHA * x + (1.0 + g1) * mix, p["ln1_g"], p["ln1_b"], tag + "ln1")

    h = x * (1.0 + sc2) + sh2
    up = dwconv(linear(h, p["w_up"], tag + "up"), p["ffn_dw"], tag + "ffn_conv")
    d_ff = up.shape[1] // 2
    val, gate = up[:, :d_ff], up[:, d_ff:]
    ffn = linear(jax.nn.silu(gate) * val, p["w_down"], tag + "down")
    return layer_norm(ALPHA * x + (1.0 + g2) * ffn, p["ln2_g"], p["ln2_b"], tag + "ln2")


def _forward(x, layers, c_act, cos, sin, pool_mask, pool_inv_cnt):
    for l, p in enumerate(layers):
        x = _layer(x, c_act, p, cos, sin, pool_mask, pool_inv_cnt, "")
    return x


def kernel(x, c, positions, w_ada, b_ada, w_in, b_in, conv_dw, conv_ln_g, conv_ln_b, w_conv_out, sc_dw, w_sc_out, q_norm_g, w_uq, kv_norm_g, w_ukv, w_mla_out, w_pool, pool_scale, w_pool_out, w_o, ln1_g, ln1_b, w_up, ffn_dw, w_down, ln2_g, ln2_b, loss_target, m_w_ada, m_b_ada, m_w_in, m_b_in, m_conv_dw, m_conv_ln_g, m_conv_ln_b, m_w_conv_out, m_sc_dw, m_w_sc_out, m_q_norm_g, m_w_uq, m_kv_norm_g, m_w_ukv, m_w_mla_out, m_w_pool, m_pool_scale, m_w_pool_out, m_w_o, m_ln1_g, m_ln1_b, m_w_up, m_ffn_dw, m_w_down, m_ln2_g, m_ln2_b, v_w_ada, v_b_ada, v_w_in, v_b_in, v_conv_dw, v_conv_ln_g, v_conv_ln_b, v_w_conv_out, v_sc_dw, v_w_sc_out, v_q_norm_g, v_w_uq, v_kv_norm_g, v_w_ukv, v_w_mla_out, v_w_pool, v_pool_scale, v_w_pool_out, v_w_o, v_ln1_g, v_ln1_b, v_w_up, v_ffn_dw, v_w_down, v_ln2_g, v_ln2_b):
    given = dict(locals())
    weights = {n: given[n] for n in WEIGHT_ORDER}
    mom_m = {n: given["m_" + n] for n in WEIGHT_ORDER}
    mom_v = {n: given["v_" + n] for n in WEIGHT_ORDER}
    depth = w_ada.shape[0]
    seq = x.shape[1]
    sharded_names = [s[0] for s in SHARDED]

    layers = []
    for l in range(depth):
        packed = _pack_weights_bf16({n: weights[n][l] for n in sharded_names})
        gathered = _exchange(packed, group="chips", broadcast=True, name="gather_weights")
        full = _unpack_weights(gathered)
        for n in REPLICATED:
            full[n] = weights[n][l]
        layers.append(full)

    inv = 1.0 / (ROPE_THETA ** (jnp.arange(0, QK_ROPE, 2, dtype=F32) / QK_ROPE))
    ang = positions[0].astype(F32)[:, None] * inv
    cos, sin = jnp.cos(ang), jnp.sin(ang)
    c_act = jnp.pad(jax.nn.silu(c), ((0, 15), (0, 0)))
    pool_mask, pool_inv_cnt = _pool_constants(seq)

    y, vjp_fn = jax.vjp(lambda xx, ll: _forward(xx, ll, c_act, cos, sin, pool_mask, pool_inv_cnt), x[0], layers)
    loss_local, dy = _loss_call(y, loss_target[0])
    grad_x, grad_layers = vjp_fn(dy)
    loss = lax.psum(loss_local, ("x", "y", "c"))

    grad_sh = []
    for l in range(depth):
        reduced = _reduce_scatter(_pack_grads(grad_layers[l]), tag="rs")
        grad_sh.append(reduced)
    rep_like = {n: weights[n] for n in REPLICATED}
    rep_local = _pack_replicated({n: jnp.stack([grad_layers[l][n] for l in range(depth)]) for n in REPLICATED})
    rep_all = _exchange(rep_local, group="all", broadcast=True, name="gather_small_grads")
    grad_rep = _ordered_sum(rep_all, name="small_grads_sum")

    outs = {"grad": {}, "delta": {}, "m": {}, "v": {}}
    shards = [_unpack_shards(grad_sh[l]) for l in range(depth)]
    for n in sharded_names:
        outs["grad"][n] = jnp.stack([shards[l][n] for l in range(depth)])
    outs["grad"].update(_unpack_replicated(grad_rep, rep_like))
    for n in WEIGHT_ORDER:
        outs["delta"][n], outs["m"][n], outs["v"][n] = _adamw_call(
            weights[n], outs["grad"][n], mom_m[n], mom_v[n], name="adamw_" + n)

    result = [loss, grad_x[None]]
    for key in ("grad", "delta", "m", "v"):
        result.extend(outs[key][n] for n in WEIGHT_ORDER)
    return tuple(result)
```

```python
import functools
import math

import jax
import jax.numpy as jnp
import numpy as np
from jax import lax
from jax.experimental import pallas as pl
from jax.experimental.pallas import tpu as pltpu

F32 = jnp.float32
BF16 = jnp.bfloat16
MXU = jnp.bfloat16

D_MODEL = 1024
N_HEADS = 8
QK_NOPE = 64
QK_ROPE = 32
V_DIM = 64
HEAD_PAD = 128
ROPE_THETA = 10000.0
POOL_WINDOWS = (2, 4, 8, 16)
LN_EPS = 1e-5
RMS_EPS = 1e-6
DEPTH_FOR_DEEPNORM = 4
ALPHA = (2.0 * DEPTH_FOR_DEEPNORM) ** 0.25
ATTN_SCALE = (QK_NOPE + QK_ROPE) ** -0.5
LOG2_E = math.log2(math.e)
IN_COLS = 7584
IN_PAD_AT = 2976
IN_PAD = 96
ADAM_LR, ADAM_B1, ADAM_B2, ADAM_EPS, ADAM_WD, ADAM_STEP = 0.001, 0.9, 0.999, 1e-08, 0.01, 10

VMEM_LIMIT = 48 * 1024 * 1024
LANE = 128
CONV_HALO = 32
PACK_W = 512
MESH_T = pl.DeviceIdType.MESH

SHARDED = (
    ("w_ada", (1024, 6144), 1, True),
    ("w_in", (1024, IN_COLS), 1, True),
    ("conv_dw", (31, 512), 1, False),
    ("w_conv_out", (512, 1024), 1, True),
    ("sc_dw", (3, 512), 1, False),
    ("w_sc_out", (512, 1024), 1, True),
    ("w_uq", (256, 768), 1, True),
    ("w_ukv", (128, 1024), 1, True),
    ("w_mla_out", (512, 1024), 1, True),
    ("w_pool_out", (512, 1024), 1, True),
    ("w_o", (1024, 1024), 0, True),
    ("w_up", (1024, 5632), 1, True),
    ("ffn_dw", (3, 5632), 1, False),
    ("w_down", (2816, 1024), 0, True),
)
REPLICATED = ("b_ada", "b_in", "conv_ln_g", "conv_ln_b", "q_norm_g", "kv_norm_g", "w_pool", "pool_scale",
              "ln1_g", "ln1_b", "ln2_g", "ln2_b")
WEIGHT_ORDER = ("w_ada", "b_ada", "w_in", "b_in", "conv_dw", "conv_ln_g", "conv_ln_b", "w_conv_out", "sc_dw",
                "w_sc_out", "q_norm_g", "w_uq", "kv_norm_g", "w_ukv", "w_mla_out", "w_pool", "pool_scale",
                "w_pool_out", "w_o", "ln1_g", "ln1_b", "w_up", "ffn_dw", "w_down", "ln2_g", "ln2_b")


def _pick(dim, cap, quantum=LANE):
    best = None
    t = quantum
    while t <= min(dim, cap):
        if dim % t == 0:
            best = t
        t += quantum
    return dim if best is None else best


def _params(n_grid):
    return pltpu.CompilerParams(dimension_semantics=("arbitrary",) * n_grid, vmem_limit_bytes=VMEM_LIMIT)


def _mm(a, b, *, ta=False, tb=False, bias=None, name):
    if ta:
        k_dim, m_dim = a.shape
    else:
        m_dim, k_dim = a.shape
    if tb:
        n_dim, k2 = b.shape
    else:
        k2, n_dim = b.shape
    assert k_dim == k2, (a.shape, b.shape, ta, tb)
    tm = _pick(m_dim, 1024)
    tn = _pick(n_dim, 1024)
    tk = _pick(k_dim, 1024)
    nk = k_dim // tk
    a_spec = pl.BlockSpec((tk, tm), lambda i, j, k: (k, i)) if ta else pl.BlockSpec((tm, tk), lambda i, j, k: (i, k))
    b_spec = pl.BlockSpec((tn, tk), lambda i, j, k: (j, k)) if tb else pl.BlockSpec((tk, tn), lambda i, j, k: (k, j))
    dims = (((0 if ta else 1,), (1 if tb else 0,)), ((), ()))
    has_bias = bias is not None

    def body(*refs):
        if has_bias:
            a_ref, b_ref, bias_ref, o_ref, acc_ref = refs
        else:
            a_ref, b_ref, o_ref, acc_ref = refs
        k = pl.program_id(2)

        @pl.when(k == 0)
        def _():
            acc_ref[...] = jnp.zeros_like(acc_ref)

        acc_ref[...] += lax.dot_general(a_ref[...].astype(MXU), b_ref[...].astype(MXU), dims,
                                        preferred_element_type=F32)

        @pl.when(k == nk - 1)
        def _():
            out = acc_ref[...]
            if has_bias:
                out = out + bias_ref[...]
            o_ref[...] = out

    in_specs = [a_spec, b_spec]
    args = [a, b]
    if has_bias:
        in_specs.append(pl.BlockSpec((1, tn), lambda i, j, k: (0, j)))
        args.append(bias.reshape(1, n_dim))
    return pl.pallas_call(
        body, name=name, grid=(m_dim // tm, n_dim // tn, nk), in_specs=in_specs,
        out_specs=pl.BlockSpec((tm, tn), lambda i, j, k: (i, j)),
        out_shape=jax.ShapeDtypeStruct((m_dim, n_dim), F32),
        scratch_shapes=[pltpu.VMEM((tm, tn), F32)], compiler_params=_params(3),
    )(*args)


def _colsum(x, *, name):
    rows, n = x.shape
    tr = _pick(rows, 1024, 8)
    tn = _pick(n, 1024)

    def body(x_ref, o_ref):
        @pl.when(pl.program_id(1) == 0)
        def _():
            o_ref[...] = jnp.zeros_like(o_ref)

        o_ref[...] += jnp.sum(x_ref[...], axis=0, keepdims=True)

    out = pl.pallas_call(
        body, name=name, grid=(n // tn, rows // tr), in_specs=[pl.BlockSpec((tr, tn), lambda j, i: (i, j))],
        out_specs=pl.BlockSpec((1, tn), lambda j, i: (0, j)), out_shape=jax.ShapeDtypeStruct((1, n), F32),
        compiler_params=_params(2),
    )(x)
    return out[0]


@functools.partial(jax.custom_vjp, nondiff_argnums=(2,))
def linear(x, w, name):
    return _mm(x, w, name=name + "_f")


def _linear_fwd(x, w, name):
    return _mm(x, w, name=name + "_f"), (x, w)


def _linear_bwd(name, res, dy):
    x, w = res
    return _mm(dy, w, tb=True, name=name + "_dx"), _mm(x, dy, ta=True, name=name + "_dw")


linear.defvjp(_linear_fwd, _linear_bwd)


@functools.partial(jax.custom_vjp, nondiff_argnums=(3,))
def linear_bias(x, w, b, name):
    return _mm(x, w, bias=b, name=name + "_f")


def _linear_bias_fwd(x, w, b, name):
    return _mm(x, w, bias=b, name=name + "_f"), (x, w)


def _linear_bias_bwd(name, res, dy):
    x, w = res
    return (_mm(dy, w, tb=True, name=name + "_dx"), _mm(x, dy, ta=True, name=name + "_dw"),
            _colsum(dy, name=name + "_db"))


linear_bias.defvjp(_linear_bias_fwd, _linear_bias_bwd)


def _norm_stats(x, center, eps):
    if center:
        mu = jnp.mean(x, axis=-1, keepdims=True)
        xc = x - mu
    else:
        xc = x
    rstd = lax.rsqrt(jnp.mean(xc * xc, axis=-1, keepdims=True) + eps)
    return xc * rstd


def _norm_fwd_call(x, g, b, *, center, name):
    rows, d = x.shape
    tr = _pick(rows, 512, 8)
    eps = LN_EPS if center else RMS_EPS

    def body(x_ref, g_ref, b_ref, o_ref):
        xhat = _norm_stats(x_ref[...], center, eps)
        y = xhat * g_ref[...]
        if center:
            y = y + b_ref[...]
        o_ref[...] = y

    vec = pl.BlockSpec((1, d), lambda i: (0, 0))
    return pl.pallas_call(
        body, name=name, grid=(rows // tr,), in_specs=[pl.BlockSpec((tr, d), lambda i: (i, 0)), vec, vec],
        out_specs=pl.BlockSpec((tr, d), lambda i: (i, 0)), out_shape=jax.ShapeDtypeStruct((rows, d), F32),
        compiler_params=_params(1),
    )(x, g.reshape(1, d), b.reshape(1, d))


def _norm_bwd_call(x, g, dy, *, center, name):
    rows, d = x.shape
    tr = _pick(rows, 512, 8)
    eps = LN_EPS if center else RMS_EPS

    def body(x_ref, g_ref, dy_ref, dx_ref, dg_ref, db_ref):
        @pl.when(pl.program_id(0) == 0)
        def _():
            dg_ref[...] = jnp.zeros_like(dg_ref)
            db_ref[...] = jnp.zeros_like(db_ref)

        x = x_ref[...]
        dy = dy_ref[...]
        if center:
            mu = jnp.mean(x, axis=-1, keepdims=True)
            xc = x - mu
        else:
            xc = x
        rstd = lax.rsqrt(jnp.mean(xc * xc, axis=-1, keepdims=True) + eps)
        xhat = xc * rstd
        dyg = dy * g_ref[...]
        proj = jnp.mean(dyg * xhat, axis=-1, keepdims=True)
        dx = dyg - xhat * proj
        if center:
            dx = dx - jnp.mean(dyg, axis=-1, keepdims=True)
        dx_ref[...] = dx * rstd
        dg_ref[...] += jnp.sum(dy * xhat, axis=0, keepdims=True)
        db_ref[...] += jnp.sum(dy, axis=0, keepdims=True)

    vec = pl.BlockSpec((1, d), lambda i: (0, 0))
    row = pl.BlockSpec((tr, d), lambda i: (i, 0))
    dx, dg, db = pl.pallas_call(
        body, name=name, grid=(rows // tr,), in_specs=[row, vec, row], out_specs=[row, vec, vec],
        out_shape=[jax.ShapeDtypeStruct((rows, d), F32), jax.ShapeDtypeStruct((1, d), F32),
                   jax.ShapeDtypeStruct((1, d), F32)],
        compiler_params=_params(1),
    )(x, g.reshape(1, d), dy)
    return dx, dg[0], db[0]


@functools.partial(jax.custom_vjp, nondiff_argnums=(3,))
def layer_norm(x, g, b, name):
    return _norm_fwd_call(x, g, b, center=True, name=name + "_f")


def _layer_norm_fwd(x, g, b, name):
    return _norm_fwd_call(x, g, b, center=True, name=name + "_f"), (x, g)


def _layer_norm_bwd(name, res, dy):
    x, g = res
    return _norm_bwd_call(x, g, dy, center=True, name=name + "_b")


layer_norm.defvjp(_layer_norm_fwd, _layer_norm_bwd)


@functools.partial(jax.custom_vjp, nondiff_argnums=(2,))
def rms_norm(x, g, name):
    return _norm_fwd_call(x, g, jnp.zeros_like(g), center=False, name=name + "_f")


def _rms_norm_fwd(x, g, name):
    return _norm_fwd_call(x, g, jnp.zeros_like(g), center=False, name=name + "_f"), (x, g)


def _rms_norm_bwd(name, res, dy):
    x, g = res
    dx, dg, _ = _norm_bwd_call(x, g, dy, center=False, name=name + "_b")
    return dx, dg


rms_norm.defvjp(_rms_norm_fwd, _rms_norm_bwd)


def _conv_tiles(rows, ch):
    tr = _pick(rows, 512, CONV_HALO)
    assert tr >= CONV_HALO and rows % tr == 0
    return tr, _pick(ch, 512)


def _pad_taps(w):
    k = w.shape[0]
    kp = -(-k // 8) * 8
    return jnp.pad(w, ((0, kp - k), (0, 0))), k, kp


def _dwconv_fwd_call(x, w, *, name):
    rows, ch = x.shape
    tr, tc = _conv_tiles(rows, ch)
    wp, taps, kp = _pad_taps(w)
    assert taps - 1 <= CONV_HALO
    halo_per_tile = tr // CONV_HALO

    def body(x_ref, xprev_ref, w_ref, o_ref):
        i = pl.program_id(1)
        halo = xprev_ref[...]
        halo = jnp.where(i > 0, halo, jnp.zeros_like(halo))
        xx = jnp.concatenate([halo, x_ref[...]], axis=0)
        acc = jnp.zeros((tr, tc), F32)
        for k in range(taps):
            shift = taps - 1 - k
            term = xx if shift == 0 else pltpu.roll(xx, shift, 0)
            acc = acc + w_ref[k:k + 1, :] * term[CONV_HALO:, :]
        o_ref[...] = acc

    return pl.pallas_call(
        body, name=name, grid=(ch // tc, rows // tr),
        in_specs=[pl.BlockSpec((tr, tc), lambda j, i: (i, j)),
                  pl.BlockSpec((CONV_HALO, tc), lambda j, i: (jnp.maximum(i * halo_per_tile - 1, 0), j)),
                  pl.BlockSpec((kp, tc), lambda j, i: (0, j))],
        out_specs=pl.BlockSpec((tr, tc), lambda j, i: (i, j)), out_shape=jax.ShapeDtypeStruct((rows, ch), F32),
        compiler_params=_params(2),
    )(x, x, wp)


def _dwconv_bwd_call(x, w, dy, *, name):
    rows, ch = x.shape
    tr, tc = _conv_tiles(rows, ch)
    wp, taps, kp = _pad_taps(w)
    n_row_tiles = rows // tr
    halo_per_tile = tr // CONV_HALO
    n_halo_blocks = rows // CONV_HALO
    ext = tr + CONV_HALO

    def body(x_ref, xprev_ref, dy_ref, dynext_ref, w_ref, dx_ref, dw_ref):
        i = pl.program_id(1)

        @pl.when(i == 0)
        def _():
            dw_ref[...] = jnp.zeros_like(dw_ref)

        halo = xprev_ref[...]
        halo = jnp.where(i > 0, halo, jnp.zeros_like(halo))
        xx = jnp.concatenate([halo, x_ref[...]], axis=0)
        dy = dy_ref[...]
        ahead = dynext_ref[...]
        ahead = jnp.where(i < n_row_tiles - 1, ahead, jnp.zeros_like(ahead))
        yy = jnp.concatenate([dy, ahead], axis=0)
        dx = jnp.zeros((tr, tc), F32)
        for k in range(taps):
            shift = taps - 1 - k
            fwd = yy if shift == 0 else pltpu.roll(yy, ext - shift, 0)
            dx = dx + w_ref[k:k + 1, :] * fwd[:tr, :]
            back = xx if shift == 0 else pltpu.roll(xx, shift, 0)
            dw_ref[k:k + 1, :] += jnp.sum(dy * back[CONV_HALO:, :], axis=0, keepdims=True)
        dx_ref[...] = dx

    cur = pl.BlockSpec((tr, tc), lambda j, i: (i, j))
    dx, dw = pl.pallas_call(
        body, name=name, grid=(ch // tc, n_row_tiles),
        in_specs=[cur, pl.BlockSpec((CONV_HALO, tc), lambda j, i: (jnp.maximum(i * halo_per_tile - 1, 0), j)), cur,
                  pl.BlockSpec((CONV_HALO, tc),
                               lambda j, i: (jnp.minimum((i + 1) * halo_per_tile, n_halo_blocks - 1), j)),
                  pl.BlockSpec((kp, tc), lambda j, i: (0, j))],
        out_specs=[cur, pl.BlockSpec((kp, tc), lambda j, i: (0, j))],
        out_shape=[jax.ShapeDtypeStruct((rows, ch), F32), jax.ShapeDtypeStruct((kp, ch), F32)],
        compiler_params=_params(2),
    )(x, x, dy, dy, wp)
    return dx, dw[:taps]


@functools.partial(jax.custom_vjp, nondiff_argnums=(2,))
def dwconv(x, w, name):
    return _dwconv_fwd_call(x, w, name=name + "_f")


def _dwconv_fwd(x, w, name):
    return _dwconv_fwd_call(x, w, name=name + "_f"), (x, w)


def _dwconv_bwd(name, res, dy):
    x, w = res
    return _dwconv_bwd_call(x, w, dy, name=name + "_b")


dwconv.defvjp(_dwconv_fwd, _dwconv_bwd)


def _attn_tiles(seq):
    return _pick(seq, 1024), _pick(seq, 512)


def _lane_tile(v, width):
    return v if width == LANE else jnp.tile(v, (1, width // LANE))


def _causal_pairs(nq, nk, tq, tk, key_major):
    pairs = [(i, j) for i in range(nq) for j in range(nk) if j * tk <= i * tq + tq - 1]
    if key_major:
        pairs.sort(key=lambda p: (p[1], p[0]))
    return (jnp.asarray(np.array([p[0] for p in pairs], np.int32)),
            jnp.asarray(np.array([p[1] for p in pairs], np.int32)))


def _scores(q, k, i, j, tq, tk, masked):
    z = lax.dot_general(q, k, (((1,), (1,)), ((), ())), preferred_element_type=F32) * (ATTN_SCALE * LOG2_E)
    if masked:
        row = i * tq + lax.broadcasted_iota(jnp.int32, (tq, tk), 0)
        col = j * tk + lax.broadcasted_iota(jnp.int32, (tq, tk), 1)
        z = jnp.where(col <= row, z, -jnp.inf)
    return z


def _attn_fwd_call(q, k, v, *, name):
    heads, seq, dh = q.shape
    tq, tk = _attn_tiles(seq)
    nq, nk = seq // tq, seq // tk
    qi, kj = _causal_pairs(nq, nk, tq, tk, key_major=False)

    def body(qi_ref, kj_ref, q_ref, k_ref, v_ref, o_ref, lse_ref, m_sc, l_sc, acc_sc):
        t = pl.program_id(1)
        i = qi_ref[t]
        j = kj_ref[t]

        @pl.when(j == 0)
        def _():
            m_sc[...] = jnp.full(m_sc.shape, -jnp.inf, F32)
            l_sc[...] = jnp.zeros_like(l_sc)
            acc_sc[...] = jnp.zeros_like(acc_sc)

        def step(masked):
            z = _scores(q_ref[0], k_ref[0], i, j, tq, tk, masked)
            m_prev = m_sc[...]
            m_new = jnp.maximum(m_prev, jnp.max(z, axis=1, keepdims=True))
            alpha = jnp.exp2(m_prev - m_new)
            p = jnp.exp2(z - _lane_tile(m_new, tk))
            l_sc[...] = alpha * l_sc[...] + jnp.sum(p, axis=1, keepdims=True)
            acc_sc[...] = alpha * acc_sc[...] + jnp.dot(p.astype(MXU), v_ref[0], preferred_element_type=F32)
            m_sc[...] = m_new

        crosses_diagonal = j * tk + tk - 1 > i * tq

        @pl.when(jnp.logical_not(crosses_diagonal))
        def _():
            step(False)

        @pl.when(crosses_diagonal)
        def _():
            step(True)

        @pl.when(j == (i * tq + tq - 1) // tk)
        def _():
            o_ref[0] = acc_sc[...] / l_sc[...]
            lse_ref[0] = m_sc[...] + jnp.log(l_sc[...]) * LOG2_E

    q_spec = pl.BlockSpec((1, tq, dh), lambda h, t, qi_ref, kj_ref: (h, qi_ref[t], 0))
    kv_spec = pl.BlockSpec((1, tk, dh), lambda h, t, qi_ref, kj_ref: (h, kj_ref[t], 0))
    return pl.pallas_call(
        body, name=name,
        grid_spec=pltpu.PrefetchScalarGridSpec(
            num_scalar_prefetch=2, grid=(heads, int(qi.shape[0])), in_specs=[q_spec, kv_spec, kv_spec],
            out_specs=[q_spec, q_spec],
            scratch_shapes=[pltpu.VMEM((tq, LANE), F32), pltpu.VMEM((tq, LANE), F32), pltpu.VMEM((tq, dh), F32)]),
        out_shape=[jax.ShapeDtypeStruct((heads, seq, dh), F32), jax.ShapeDtypeStruct((heads, seq, LANE), F32)],
        compiler_params=_params(2),
    )(qi, kj, q, k, v)


def _attn_bwd_call(q, k, v, o, lse, do, *, name):
    heads, seq, dh = q.shape
    tk, tq = _attn_tiles(seq)
    nq, nk = seq // tq, seq // tk
    qi, kj = _causal_pairs(nq, nk, tq, tk, key_major=True)

    def body(qi_ref, kj_ref, q_ref, k_ref, v_ref, o_ref, lse_ref, do_ref, dq_ref, dk_ref, dv_ref, dk_sc, dv_sc):
        t = pl.program_id(1)
        i = qi_ref[t]
        j = kj_ref[t]
        first_i = (j * tk) // tq

        @pl.when(t == 0)
        def _():
            dq_ref[...] = jnp.zeros_like(dq_ref)

        @pl.when(i == first_i)
        def _():
            dk_sc[...] = jnp.zeros_like(dk_sc)
            dv_sc[...] = jnp.zeros_like(dv_sc)

        def step(masked):
            qb = q_ref[0]
            kb = k_ref[0]
            do_f = do_ref[0]
            do_b = do_f.astype(MXU)
            p = jnp.exp2(_scores(qb, kb, i, j, tq, tk, masked) - _lane_tile(lse_ref[0], tk))
            dp = lax.dot_general(do_b, v_ref[0], (((1,), (1,)), ((), ())), preferred_element_type=F32)
            delta = jnp.sum(do_f * o_ref[0], axis=1, keepdims=True)
            ds = (p * (dp - delta) * ATTN_SCALE).astype(MXU)
            dv_sc[...] += lax.dot_general(p.astype(MXU), do_b, (((0,), (0,)), ((), ())), preferred_element_type=F32)
            dk_sc[...] += lax.dot_general(ds, qb, (((0,), (0,)), ((), ())), preferred_element_type=F32)
            rows = pl.ds(pl.multiple_of(i * tq, tq), tq)
            dq_ref[0, rows, :] += jnp.dot(ds, kb, preferred_element_type=F32)

        crosses_diagonal = j * tk + tk - 1 > i * tq

        @pl.when(jnp.logical_not(crosses_diagonal))
        def _():
            step(False)

        @pl.when(crosses_diagonal)
        def _():
            step(True)

        @pl.when(i == nq - 1)
        def _():
            dk_ref[0] = dk_sc[...]
            dv_ref[0] = dv_sc[...]

    def q_map(h, t, qi_ref, kj_ref):
        return (h, qi_ref[t], 0)

    def kv_map(h, t, qi_ref, kj_ref):
        return (h, kj_ref[t], 0)

    q_spec = pl.BlockSpec((1, tq, dh), q_map)
    lse_spec = pl.BlockSpec((1, tq, LANE), q_map)
    kv_spec = pl.BlockSpec((1, tk, dh), kv_map)
    head_spec = pl.BlockSpec((1, seq, dh), lambda h, t, qi_ref, kj_ref: (h, 0, 0))
    return pl.pallas_call(
        body, name=name,
        grid_spec=pltpu.PrefetchScalarGridSpec(
            num_scalar_prefetch=2, grid=(heads, int(qi.shape[0])),
            in_specs=[q_spec, kv_spec, kv_spec, q_spec, lse_spec, q_spec], out_specs=[head_spec, kv_spec, kv_spec],
            scratch_shapes=[pltpu.VMEM((tk, dh), F32), pltpu.VMEM((tk, dh), F32)]),
        out_shape=[jax.ShapeDtypeStruct((heads, seq, dh), F32)] * 3,
        compiler_params=_params(2),
    )(qi, kj, q, k, v, o, lse, do)


@functools.partial(jax.custom_vjp, nondiff_argnums=(3,))
def attention(q, k, v, name):
    return _attn_fwd_call(q.astype(MXU), k.astype(MXU), v.astype(MXU), name=name + "_f")[0]


def _attention_fwd(q, k, v, name):
    qb, kb, vb = q.astype(MXU), k.astype(MXU), v.astype(MXU)
    o, lse = _attn_fwd_call(qb, kb, vb, name=name + "_f")
    return o, (qb, kb, vb, o, lse)


def _attention_bwd(name, res, do):
    qb, kb, vb, o, lse = res
    return tuple(_attn_bwd_call(qb, kb, vb, o, lse, do, name=name + "_b"))


attention.defvjp(_attention_fwd, _attention_bwd)


def _merge_tiles(rows):
    return _pick(rows, 128, 8)


def _gate_merge_fwd_call(gates, ys, *, name):
    rows, d = ys[0].shape
    nb = len(ys)
    tr = _merge_tiles(rows)

    def body(g_ref, *refs):
        o_ref = refs[nb]
        acc = jax.nn.sigmoid(g_ref[:, 0:d]) * refs[0][...]
        for b in range(1, nb):
            acc = acc + jax.nn.sigmoid(g_ref[:, b * d:(b + 1) * d]) * refs[b][...]
        o_ref[...] = acc

    wide = pl.BlockSpec((tr, nb * d), lambda i: (i, 0))
    row = pl.BlockSpec((tr, d), lambda i: (i, 0))
    return pl.pallas_call(
        body, name=name, grid=(rows // tr,), in_specs=[wide] + [row] * nb, out_specs=row,
        out_shape=jax.ShapeDtypeStruct((rows, d), F32), compiler_params=_params(1),
    )(gates, *ys)


def _gate_merge_bwd_call(gates, ys, dm, *, name):
    rows, d = ys[0].shape
    nb = len(ys)
    tr = _merge_tiles(rows)

    def body(g_ref, *refs):
        y_refs = refs[:nb]
        dm_ref = refs[nb]
        dg_ref = refs[nb + 1]
        dy_refs = refs[nb + 2:]
        dm = dm_ref[...]
        for b in range(nb):
            gt = jax.nn.sigmoid(g_ref[:, b * d:(b + 1) * d])
            dy_refs[b][...] = gt * dm
            dg_ref[:, b * d:(b + 1) * d] = dm * y_refs[b][...] * gt * (1.0 - gt)

    wide = pl.BlockSpec((tr, nb * d), lambda i: (i, 0))
    row = pl.BlockSpec((tr, d), lambda i: (i, 0))
    outs = pl.pallas_call(
        body, name=name, grid=(rows // tr,), in_specs=[wide] + [row] * (nb + 1), out_specs=[wide] + [row] * nb,
        out_shape=[jax.ShapeDtypeStruct((rows, nb * d), F32)] + [jax.ShapeDtypeStruct((rows, d), F32)] * nb,
        compiler_params=_params(1),
    )(gates, *ys, dm)
    return outs[0], tuple(outs[1:])


@functools.partial(jax.custom_vjp, nondiff_argnums=(2,))
def gate_merge(gates, ys, name):
    return _gate_merge_fwd_call(gates, ys, name=name + "_f")


def _gate_merge_fwd(gates, ys, name):
    return _gate_merge_fwd_call(gates, ys, name=name + "_f"), (gates, ys)


def _gate_merge_bwd(name, res, dm):
    gates, ys = res
    return _gate_merge_bwd_call(gates, ys, dm, name=name + "_b")


gate_merge.defvjp(_gate_merge_fwd, _gate_merge_bwd)


def _loss_call(y, target):
    rows, d = y.shape
    tr = _pick(rows, 512, 8)

    def body(y_ref, t_ref, dy_ref, loss_ref):
        @pl.when(pl.program_id(0) == 0)
        def _():
            loss_ref[...] = jnp.zeros_like(loss_ref)

        err = y_ref[...] - t_ref[...]
        dy_ref[...] = err * (1.0 / d)
        loss_ref[...] += jnp.sum(err * err, axis=0, keepdims=True)

    row = pl.BlockSpec((tr, d), lambda i: (i, 0))
    vec = pl.BlockSpec((1, d), lambda i: (0, 0))
    dy, part = pl.pallas_call(
        body, name="loss_head", grid=(rows // tr,), in_specs=[row, row], out_specs=[row, vec],
        out_shape=[jax.ShapeDtypeStruct((rows, d), F32), jax.ShapeDtypeStruct((1, d), F32)],
        compiler_params=_params(1),
    )(y, target)
    return 0.5 * jnp.sum(part) / d, dy


ADAMW_BLOCK_ELEMS = 256 * 1024


def _adamw_call(w, g, m, v, *, name):
    shape = w.shape
    width = shape[-1]
    rows = w.size // width
    w, g, m, v = (a.reshape(rows, width) for a in (w, g, m, v))
    tr = _pick(rows, max(8, ADAMW_BLOCK_ELEMS // width), 8)
    c1 = 1.0 - ADAM_B1 ** ADAM_STEP
    c2 = 1.0 - ADAM_B2 ** ADAM_STEP

    def body(w_ref, g_ref, m_ref, v_ref, d_ref, nm_ref, nv_ref):
        g = g_ref[...]
        m = ADAM_B1 * m_ref[...] + (1.0 - ADAM_B1) * g
        v = ADAM_B2 * v_ref[...] + (1.0 - ADAM_B2) * (g * g)
        m_hat = m / c1
        v_hat = v / c2
        d_ref[...] = -ADAM_LR * (m_hat / (jnp.sqrt(v_hat) + ADAM_EPS) + ADAM_WD * w_ref[...])
        nm_ref[...] = m
        nv_ref[...] = v

    row = pl.BlockSpec((tr, width), lambda i: (i, 0))
    outs = pl.pallas_call(
        body, name=name, grid=(rows // tr,), in_specs=[row] * 4, out_specs=[row] * 3,
        out_shape=[jax.ShapeDtypeStruct((rows, width), F32)] * 3, compiler_params=_params(1),
    )(w, g, m, v)
    return [o.reshape(shape) for o in outs]


def _ordered_sum(y, *, name):
    n, rows, width = y.shape
    tr = _pick(rows, 256, 8)

    def body(y_ref, o_ref):
        acc = y_ref[0]
        for s in range(1, n):
            acc = acc + y_ref[s]
        o_ref[...] = acc

    return pl.pallas_call(
        body, name=name, grid=(rows // tr,), in_specs=[pl.BlockSpec((n, tr, width), lambda i: (0, i, 0))],
        out_specs=pl.BlockSpec((tr, width), lambda i: (i, 0)), out_shape=jax.ShapeDtypeStruct((rows, width), y.dtype),
        compiler_params=_params(1),
    )(y)


_GROUP_FLIPS = {
    "chips": ((1, 0, 0), (0, 1, 0), (1, 1, 0)),
    "cores": ((0, 0, 1),),
    "all": ((0, 0, 1), (0, 1, 0), (0, 1, 1), (1, 0, 0), (1, 0, 1), (1, 1, 0), (1, 1, 1)),
}


def _exchange(x, *, group, broadcast, name):
    flips = _GROUP_FLIPS[group]
    n = len(flips) + 1
    block = x.shape if broadcast else x.shape[1:]
    if not broadcast:
        assert x.shape[0] == n

    def body(x_ref, o_ref, send_sems, recv_sems, local_sem):
        mx, my, mc = lax.axis_index("x"), lax.axis_index("y"), lax.axis_index("c")

        def index(px, py, pc):
            return {"chips": 2 * px + py, "cores": pc, "all": 4 * px + 2 * py + pc}[group]

        def block_for(d):
            return x_ref if broadcast else x_ref.at[d]

        me = index(mx, my, mc)
        mine = pltpu.make_async_copy(block_for(me), o_ref.at[me], local_sem)
        mine.start()
        sends, recvs = [], []
        for k, (fx, fy, fc) in enumerate(flips):
            px = 1 - mx if fx else mx
            py = 1 - my if fy else my
            pc = 1 - mc if fc else mc
            peer = index(px, py, pc)
            sends.append(pltpu.make_async_remote_copy(
                src_ref=block_for(peer), dst_ref=o_ref.at[me], send_sem=send_sems.at[k], recv_sem=recv_sems.at[k],
                device_id=(px, py, pc), device_id_type=MESH_T))
            recvs.append(pltpu.make_async_remote_copy(
                src_ref=block_for(peer), dst_ref=o_ref.at[peer], send_sem=send_sems.at[k], recv_sem=recv_sems.at[k],
                device_id=(px, py, pc), device_id_type=MESH_T))
        for cp in sends:
            cp.start()
        for cp in recvs:
            cp.wait_recv()
        for cp in sends:
            cp.wait_send()
        mine.wait()

    any_spec = pl.BlockSpec(memory_space=pl.ANY)
    return pl.pallas_call(
        body, name=name, in_specs=[any_spec], out_specs=any_spec,
        out_shape=jax.ShapeDtypeStruct((n,) + tuple(block), x.dtype),
        scratch_shapes=[pltpu.SemaphoreType.DMA((n - 1,)), pltpu.SemaphoreType.DMA((n - 1,)), pltpu.SemaphoreType.DMA],
    )(x)


def _sibling():
    return (lax.axis_index("x"), lax.axis_index("y"), 1 - lax.axis_index("c"))


def _run_copies(copies):
    for cp in copies:
        cp.start()
    for cp in copies:
        cp.wait_recv()
    for cp in copies:
        cp.wait_send()


def _dma_call(body, arrays, out_shapes, n_copies, *, name, extra_scratch=()):
    any_spec = pl.BlockSpec(memory_space=pl.ANY)
    return pl.pallas_call(
        body, name=name, in_specs=[any_spec] * len(arrays), out_specs=[any_spec] * len(out_shapes),
        out_shape=out_shapes,
        scratch_shapes=[pltpu.SemaphoreType.DMA((n_copies,)), pltpu.SemaphoreType.DMA((n_copies,))]
        + list(extra_scratch),
    )(*arrays)


def _gather_shards(shards, *, name):
    flips = _GROUP_FLIPS["chips"]
    n = len(shards)

    def body(*refs):
        ins, outs = refs[:n], refs[n:2 * n]
        send_sems, recv_sems, local_sems = refs[2 * n:]
        mx, my, mc = lax.axis_index("x"), lax.axis_index("y"), lax.axis_index("c")
        me = 2 * mx + my
        local = [pltpu.make_async_copy(ins[t], outs[t].at[me], local_sems.at[t]) for t in range(n)]
        for cp in local:
            cp.start()
        sends, recvs = [], []
        for t in range(n):
            for k, (fx, fy, _) in enumerate(flips):
                px = 1 - mx if fx else mx
                py = 1 - my if fy else my
                sems = dict(send_sem=send_sems.at[t * len(flips) + k], recv_sem=recv_sems.at[t * len(flips) + k],
                            device_id=(px, py, mc), device_id_type=MESH_T)
                sends.append(pltpu.make_async_remote_copy(src_ref=ins[t], dst_ref=outs[t].at[me], **sems))
                recvs.append(pltpu.make_async_remote_copy(src_ref=ins[t], dst_ref=outs[t].at[2 * px + py], **sems))
        for cp in sends:
            cp.start()
        for cp in recvs:
            cp.wait_recv()
        for cp in sends:
            cp.wait_send()
        for cp in local:
            cp.wait()

    return _dma_call(body, shards, [jax.ShapeDtypeStruct((4,) + a.shape, a.dtype) for a in shards], n * len(flips),
                     name=name, extra_scratch=[pltpu.SemaphoreType.DMA((n,))])


def _pair_send(gs, *, name):
    n = len(gs)

    def body(*refs):
        ins, outs = refs[:n], refs[n:2 * n]
        send_sems, recv_sems = refs[2 * n:]
        copies = []
        for t in range(n):
            half = gs[t].shape[1] // 2
            theirs = (1 - lax.axis_index("c")) * half
            for s in range(4):
                copies.append(pltpu.make_async_remote_copy(
                    src_ref=ins[t].at[s, pl.ds(theirs, half), :], dst_ref=outs[t].at[s],
                    send_sem=send_sems.at[4 * t + s], recv_sem=recv_sems.at[4 * t + s],
                    device_id=_sibling(), device_id_type=MESH_T))
        _run_copies(copies)

    return _dma_call(body, gs, [jax.ShapeDtypeStruct((4, a.shape[1] // 2, a.shape[2]), a.dtype) for a in gs], 4 * n,
                     name=name)


def _quad_send(ps, *, name):
    flips = _GROUP_FLIPS["chips"]
    n = len(ps)

    def body(*refs):
        ins, outs = refs[:n], refs[n:2 * n]
        send_sems, recv_sems = refs[2 * n:]
        mx, my, mc = lax.axis_index("x"), lax.axis_index("y"), lax.axis_index("c")
        copies = []
        for t in range(n):
            for k, (fx, fy, _) in enumerate(flips):
                px = 1 - mx if fx else mx
                py = 1 - my if fy else my
                copies.append(pltpu.make_async_remote_copy(
                    src_ref=ins[t].at[2 * px + py], dst_ref=outs[t].at[k], send_sem=send_sems.at[3 * t + k],
                    recv_sem=recv_sems.at[3 * t + k], device_id=(px, py, mc), device_id_type=MESH_T))
        _run_copies(copies)

    return _dma_call(body, ps, [jax.ShapeDtypeStruct((3,) + a.shape[1:], a.dtype) for a in ps], 3 * n, name=name)


def _sibling_send(ms, *, name):
    n = len(ms)

    def body(*refs):
        ins, outs = refs[:n], refs[n:2 * n]
        send_sems, recv_sems = refs[2 * n:]
        _run_copies([pltpu.make_async_remote_copy(
            src_ref=ins[t], dst_ref=outs[t], send_sem=send_sems.at[t], recv_sem=recv_sems.at[t],
            device_id=_sibling(), device_id_type=MESH_T) for t in range(n)])

    return _dma_call(body, ms, [jax.ShapeDtypeStruct(a.shape, a.dtype) for a in ms], n, name=name)


SUM_BLOCK_ELEMS = 256 * 1024


def _sum_tile(rows, width):
    return _pick(rows, max(16, SUM_BLOCK_ELEMS // width), 16)


def _pair_sum(g, got, core, *, name):
    _, rows, width = g.shape
    half = rows // 2
    tr = _sum_tile(half, width)
    tiles = half // tr

    def body(core_ref, g_ref, got_ref, o_ref):
        o_ref[...] = (g_ref[...] + got_ref[...]).astype(BF16)

    blk = pl.BlockSpec((1, tr, width), lambda s, i, core_ref: (s, i, 0))
    return pl.pallas_call(
        body, name=name,
        grid_spec=pltpu.PrefetchScalarGridSpec(
            num_scalar_prefetch=1, grid=(4, tiles),
            in_specs=[pl.BlockSpec((1, tr, width), lambda s, i, core_ref: (s, core_ref[0] * tiles + i, 0)), blk],
            out_specs=blk),
        out_shape=jax.ShapeDtypeStruct((4, half, width), BF16), compiler_params=_params(2),
    )(core, g, got)


def _quad_sum(pair, others, chip, *, name):
    _, rows, width = pair.shape
    tr = _sum_tile(rows, width)

    def body(chip_ref, p_ref, o3_ref, o_ref):
        acc = p_ref[0].astype(F32)
        for k in range(3):
            acc = acc + o3_ref[k].astype(F32)
        o_ref[...] = acc

    return pl.pallas_call(
        body, name=name,
        grid_spec=pltpu.PrefetchScalarGridSpec(
            num_scalar_prefetch=1, grid=(rows // tr,),
            in_specs=[pl.BlockSpec((1, tr, width), lambda i, chip_ref: (chip_ref[0], i, 0)),
                      pl.BlockSpec((3, tr, width), lambda i, chip_ref: (0, i, 0))],
            out_specs=pl.BlockSpec((tr, width), lambda i, chip_ref: (i, 0))),
        out_shape=jax.ShapeDtypeStruct((rows, width), F32), compiler_params=_params(1),
    )(chip, pair, others)


def _reduce_scatter(gs, *, tag):
    mc = lax.axis_index("c")
    core = mc.reshape(1).astype(jnp.int32)
    chip = (2 * lax.axis_index("x") + lax.axis_index("y")).reshape(1).astype(jnp.int32)
    got = _pair_send(gs, name=tag + "_pair")
    pairs = [_pair_sum(g, r, core, name=tag + "_pair_sum") for g, r in zip(gs, got)]
    others = _quad_send(pairs, name=tag + "_quad")
    mine = [_quad_sum(p, o, chip, name=tag + "_quad_sum") for p, o in zip(pairs, others)]
    theirs = _sibling_send(mine, name=tag + "_share")
    return [jnp.where(mc == 0, jnp.concatenate([m, t], axis=0), jnp.concatenate([t, m], axis=0))
            for m, t in zip(mine, theirs)]


def _round_up(n, q):
    return -(-n // q) * q


def _pack_rows(flat, quantum_rows):
    n = flat.shape[-1]
    total = _round_up(n, PACK_W * quantum_rows)
    pad = [(0, 0)] * (flat.ndim - 1) + [(0, total - n)]
    return jnp.pad(flat, pad).reshape(flat.shape[:-1] + (total // PACK_W, PACK_W))


def _split_shards(full, axis):
    if axis == 0:
        return full.reshape((4, full.shape[0] // 4, full.shape[1]))
    width = full.shape[1] // 4
    return jnp.stack([full[:, s * width:(s + 1) * width] for s in range(4)])


def _join_shards(blocks, axis):
    if axis == 0:
        return blocks.reshape((4 * blocks.shape[1], blocks.shape[2]))
    return jnp.concatenate([blocks[s] for s in range(4)], axis=1)


def _pack_small(tree, names):
    return _pack_rows(jnp.concatenate([tree[name].reshape(-1) for name in names]), 8)


def _unpack_small(packed, shapes, names):
    lead = packed.shape[:-2]
    flat = packed.reshape(lead + (-1,))
    out, off = {}, 0
    for name in names:
        n = int(np.prod(shapes[name]))
        out[name] = flat[..., off:off + n].reshape(lead + tuple(shapes[name]))
        off += n
    return out


def _rope(t, cos, sin):
    half = QK_ROPE // 2
    t1, t2 = t[..., :half], t[..., half:]
    return jnp.concatenate([t1 * cos - t2 * sin, t2 * cos + t1 * sin], axis=-1)


def _pad_in_cols(a):
    z = jnp.zeros(a.shape[:-1] + (IN_PAD,), a.dtype)
    return jnp.concatenate([a[..., :IN_PAD_AT], z, a[..., IN_PAD_AT:]], axis=-1)


def _pool_constants(seq):
    taps = max(POOL_WINDOWS)
    win = jnp.repeat(jnp.asarray(POOL_WINDOWS, jnp.int32), 512 // len(POOL_WINDOWS))
    lag = taps - 1 - jnp.arange(taps, dtype=jnp.int32)
    mask = (lag[:, None] < win[None, :]).astype(F32)
    cnt = jnp.minimum(jnp.arange(seq, dtype=jnp.int32)[:, None] + 1, win[None, :]).astype(F32)
    return mask, 1.0 / cnt


def _block_diag(w):
    g, n, _ = w.shape
    out = jnp.zeros((g * n, g * n), w.dtype)
    for i in range(g):
        out = lax.dynamic_update_slice(out, w[i], (i * n, i * n))
    return out


def _layer(x, c_act, p, cos, sin, pool_mask, pool_inv_cnt, tag):
    seq = x.shape[0]
    d = D_MODEL
    mod = linear_bias(c_act, p["w_ada"], p["b_ada"], tag + "ada")[0]
    sh1, sc1, g1, sh2, sc2, g2 = [mod[i * d:(i + 1) * d][None, :] for i in range(6)]

    h = x * (1.0 + sc1) + sh1
    proj = linear_bias(h, _pad_in_cols(p["w_in"]), _pad_in_cols(p["b_in"]), tag + "in")
    conv_a, conv_b = proj[:, 0:512], proj[:, 512:1024]
    sc_bg, sc_cg, sc_x = proj[:, 1024:1536], proj[:, 1536:2048], proj[:, 2048:2560]
    q_lat, kv_lat, k_rope = proj[:, 2560:2816], proj[:, 2816:2944], proj[:, 2944:2976]
    pool_u, gates = proj[:, 3072:3584], proj[:, 3584:7680]

    ya = conv_a * jax.nn.sigmoid(conv_b)
    ya = dwconv(ya, p["conv_dw"], tag + "convA")
    ya = jax.nn.silu(layer_norm(ya, p["conv_ln_g"], p["conv_ln_b"], tag + "convA_ln"))
    ya = linear(ya, p["w_conv_out"], tag + "convA_out")

    yb = linear(sc_bg * dwconv(sc_cg * sc_x, p["sc_dw"], tag + "sc"), p["w_sc_out"], tag + "sc_out")

    q = linear(rms_norm(q_lat, p["q_norm_g"], tag + "q_rms"), p["w_uq"], tag + "uq")
    q = q.reshape(seq, N_HEADS, QK_NOPE + QK_ROPE)
    kv = linear(rms_norm(kv_lat, p["kv_norm_g"], tag + "kv_rms"), p["w_ukv"], tag + "ukv")
    kv = kv.reshape(seq, N_HEADS, QK_NOPE + V_DIM)
    q_rope = _rope(q[..., QK_NOPE:], cos[:, None, :], sin[:, None, :])
    k_rope_r = jnp.broadcast_to(_rope(k_rope, cos, sin)[:, None, :], (seq, N_HEADS, QK_ROPE))
    zq = jnp.zeros((seq, N_HEADS, HEAD_PAD - QK_NOPE - QK_ROPE), F32)
    zv = jnp.zeros((seq, N_HEADS, HEAD_PAD - V_DIM), F32)
    qh = jnp.concatenate([q[..., :QK_NOPE], q_rope, zq], axis=-1).transpose(1, 0, 2)
    kh = jnp.concatenate([kv[..., :QK_NOPE], k_rope_r, zq], axis=-1).transpose(1, 0, 2)
    vh = jnp.concatenate([kv[..., QK_NOPE:], zv], axis=-1).transpose(1, 0, 2)
    att = attention(qh, kh, vh, tag + "attn")
    att = att[:, :, :V_DIM].transpose(1, 0, 2).reshape(seq, N_HEADS * V_DIM)
    yc = linear(att, p["w_mla_out"], tag + "mla_out")

    pd = dwconv(pool_u, pool_mask, tag + "pool") * pool_inv_cnt - pool_u
    yd = linear(pd, _block_diag(p["w_pool"]), tag + "pool_mix") * p["pool_scale"][None, :]
    yd = linear(yd, p["w_pool_out"], tag + "pool_out")

    merged = gate_merge(gates, (ya, yb, yc, yd), tag + "merge")
    mix = linear(merged, p["w_o"], tag + "o")
    x = layer_norm(ALPHA * x + (1.0 + g1) * mix, p["ln1_g"], p["ln1_b"], tag + "ln1")

    h = x * (1.0 + sc2) + sh2
    up = dwconv(linear(h, p["w_up"], tag + "up"), p["ffn_dw"], tag + "ffn_conv")
    d_ff = up.shape[1] // 2
    val, gate = up[:, :d_ff], up[:, d_ff:]
    ffn = linear(jax.nn.silu(gate) * val, p["w_down"], tag + "down")
    return layer_norm(ALPHA * x + (1.0 + g2) * ffn, p["ln2_g"], p["ln2_b"], tag + "ln2")


def _forward(x, layers, c_act, cos, sin, pool_mask, pool_inv_cnt):
    for l, p in enumerate(layers):
        x = _layer(x, c_act, p, cos, sin, pool_mask, pool_inv_cnt, "")
    return x


def kernel(x, c, positions, w_ada, b_ada, w_in, b_in, conv_dw, conv_ln_g, conv_ln_b, w_conv_out, sc_dw, w_sc_out, q_norm_g, w_uq, kv_norm_g, w_ukv, w_mla_out, w_pool, pool_scale, w_pool_out, w_o, ln1_g, ln1_b, w_up, ffn_dw, w_down, ln2_g, ln2_b, loss_target, m_w_ada, m_b_ada, m_w_in, m_b_in, m_conv_dw, m_conv_ln_g, m_conv_ln_b, m_w_conv_out, m_sc_dw, m_w_sc_out, m_q_norm_g, m_w_uq, m_kv_norm_g, m_w_ukv, m_w_mla_out, m_w_pool, m_pool_scale, m_w_pool_out, m_w_o, m_ln1_g, m_ln1_b, m_w_up, m_ffn_dw, m_w_down, m_ln2_g, m_ln2_b, v_w_ada, v_b_ada, v_w_in, v_b_in, v_conv_dw, v_conv_ln_g, v_conv_ln_b, v_w_conv_out, v_sc_dw, v_w_sc_out, v_q_norm_g, v_w_uq, v_kv_norm_g, v_w_ukv, v_w_mla_out, v_w_pool, v_pool_scale, v_w_pool_out, v_w_o, v_ln1_g, v_ln1_b, v_w_up, v_ffn_dw, v_w_down, v_ln2_g, v_ln2_b):
    given = dict(locals())
    weights = {n: given[n] for n in WEIGHT_ORDER}
    mom_m = {n: given["m_" + n] for n in WEIGHT_ORDER}
    mom_v = {n: given["v_" + n] for n in WEIGHT_ORDER}
    depth = w_ada.shape[0]
    seq = x.shape[1]
    big = [(s[0], s[2]) for s in SHARDED if s[3]]
    tiny = [s[0] for s in SHARDED if not s[3]]
    chip = 2 * lax.axis_index("x") + lax.axis_index("y")

    tiny_shapes = {n: weights[n].shape for n in tiny}
    tiny_all = _exchange(_pack_small(weights, tiny), group="chips", broadcast=True, name="gather_taps")
    tiny_full = {n: jnp.concatenate([a[s] for s in range(4)], axis=-1)
                 for n, a in _unpack_small(tiny_all, tiny_shapes, tiny).items()}
    layers = []
    for l in range(depth):
        gathered = _gather_shards([weights[n][l].astype(BF16) for n, _ in big], name="gather_weights")
        full = {n: _join_shards(a, axis).astype(F32) for (n, axis), a in zip(big, gathered)}
        for n in tiny:
            full[n] = tiny_full[n][l]
        for n in REPLICATED:
            full[n] = weights[n][l]
        layers.append(full)

    inv = 1.0 / (ROPE_THETA ** (jnp.arange(0, QK_ROPE, 2, dtype=F32) / QK_ROPE))
    ang = positions[0].astype(F32)[:, None] * inv
    cos, sin = jnp.cos(ang), jnp.sin(ang)
    c_act = jnp.pad(jax.nn.silu(c), ((0, 15), (0, 0)))
    pool_mask, pool_inv_cnt = _pool_constants(seq)

    y, vjp_fn = jax.vjp(lambda xx, ll: _forward(xx, ll, c_act, cos, sin, pool_mask, pool_inv_cnt), x[0], layers)
    loss_local, dy = _loss_call(y, loss_target[0])
    grad_x, grad_layers = vjp_fn(dy)
    loss = lax.psum(loss_local, ("x", "y", "c"))

    outs = {"grad": {}, "delta": {}, "m": {}, "v": {}}
    reduced = [_reduce_scatter([_split_shards(grad_layers[l][n], axis) for n, axis in big], tag="rs")
               for l in range(depth)]
    for t, (n, _) in enumerate(big):
        outs["grad"][n] = jnp.stack([reduced[l][t] for l in range(depth)])
    small = list(REPLICATED) + tiny
    small_local = {n: jnp.stack([grad_layers[l][n] for l in range(depth)]) for n in small}
    small_shapes = {n: small_local[n].shape for n in small}
    small_all = _exchange(_pack_small(small_local, small), group="all", broadcast=True, name="gather_small_grads")
    small_sum = _unpack_small(_ordered_sum(small_all, name="small_grads_sum"), small_shapes, small)
    for n in REPLICATED:
        outs["grad"][n] = small_sum[n]
    for n in tiny:
        width = weights[n].shape[-1]
        outs["grad"][n] = lax.dynamic_slice_in_dim(small_sum[n], chip * width, width, axis=2)

    for n in WEIGHT_ORDER:
        outs["delta"][n], outs["m"][n], outs["v"][n] = _adamw_call(
            weights[n], outs["grad"][n], mom_m[n], mom_v[n], name="adamw_" + n)

    result = [loss, grad_x[None]]
    for key in ("grad", "delta", "m", "v"):
        result.extend(outs[key][n] for n in WEIGHT_ORDER)
    return tuple(result)
```

```python
import functools
import math

import jax
import jax.numpy as jnp
import numpy as np
from jax import lax
from jax.experimental import pallas as pl
from jax.experimental.pallas import tpu as pltpu

F32 = jnp.float32
BF16 = jnp.bfloat16
MXU = jnp.bfloat16

D_MODEL = 1024
N_HEADS = 8
QK_NOPE = 64
QK_ROPE = 32
V_DIM = 64
HEAD_PAD = 128
ROPE_THETA = 10000.0
POOL_WINDOWS = (2, 4, 8, 16)
LN_EPS = 1e-5
RMS_EPS = 1e-6
DEPTH_FOR_DEEPNORM = 4
ALPHA = (2.0 * DEPTH_FOR_DEEPNORM) ** 0.25
ATTN_SCALE = (QK_NOPE + QK_ROPE) ** -0.5
LOG2_E = math.log2(math.e)
IN_COLS = 7584
IN_PAD_AT = 2976
IN_PAD = 96
IN_SPLITS = ((0, 512), (512, 1024), (1024, 1536), (1536, 2048), (2048, 2560), (2560, 2816), (2816, 2944),
             (2944, 2976), (3072, 3584), (3584, 7680))
ADAM_LR, ADAM_B1, ADAM_B2, ADAM_EPS, ADAM_WD, ADAM_STEP = 0.001, 0.9, 0.999, 1e-08, 0.01, 10

VMEM_LIMIT = 48 * 1024 * 1024
LANE = 128
CONV_HALO = 32
PACK_W = 512
MESH_T = pl.DeviceIdType.MESH

SHARDED = (
    ("w_ada", (1024, 6144), 1, True),
    ("w_in", (1024, IN_COLS), 1, True),
    ("conv_dw", (31, 512), 1, False),
    ("w_conv_out", (512, 1024), 1, True),
    ("sc_dw", (3, 512), 1, False),
    ("w_sc_out", (512, 1024), 1, True),
    ("w_uq", (256, 768), 1, True),
    ("w_ukv", (128, 1024), 1, True),
    ("w_mla_out", (512, 1024), 1, True),
    ("w_pool_out", (512, 1024), 1, True),
    ("w_o", (1024, 1024), 0, True),
    ("w_up", (1024, 5632), 1, True),
    ("ffn_dw", (3, 5632), 1, False),
    ("w_down", (2816, 1024), 0, True),
)
REPLICATED = ("b_ada", "b_in", "conv_ln_g", "conv_ln_b", "q_norm_g", "kv_norm_g", "w_pool", "pool_scale",
              "ln1_g", "ln1_b", "ln2_g", "ln2_b")
WEIGHT_ORDER = ("w_ada", "b_ada", "w_in", "b_in", "conv_dw", "conv_ln_g", "conv_ln_b", "w_conv_out", "sc_dw",
                "w_sc_out", "q_norm_g", "w_uq", "kv_norm_g", "w_ukv", "w_mla_out", "w_pool", "pool_scale",
                "w_pool_out", "w_o", "ln1_g", "ln1_b", "w_up", "ffn_dw", "w_down", "ln2_g", "ln2_b")


def _pick(dim, cap, quantum=LANE):
    best = None
    t = quantum
    while t <= min(dim, cap):
        if dim % t == 0:
            best = t
        t += quantum
    return dim if best is None else best


def _params(n_grid):
    return pltpu.CompilerParams(dimension_semantics=("arbitrary",) * n_grid, vmem_limit_bytes=VMEM_LIMIT)


def _mm_tile(dim):
    t = _pick(dim, 1024)
    return t if t >= 512 or t == dim else _pick(dim, 1536)


def _mm(a, b, *, ta=False, tb=False, bias=None, name):
    if ta:
        k_dim, m_dim = a.shape
    else:
        m_dim, k_dim = a.shape
    if tb:
        n_dim, k2 = b.shape
    else:
        k2, n_dim = b.shape
    assert k_dim == k2, (a.shape, b.shape, ta, tb)
    tm, tn, tk = _mm_tile(m_dim), _mm_tile(n_dim), _mm_tile(k_dim)
    nk = k_dim // tk
    a_spec = pl.BlockSpec((tk, tm), lambda i, j, k: (k, i)) if ta else pl.BlockSpec((tm, tk), lambda i, j, k: (i, k))
    b_spec = pl.BlockSpec((tn, tk), lambda i, j, k: (j, k)) if tb else pl.BlockSpec((tk, tn), lambda i, j, k: (k, j))
    dims = (((0 if ta else 1,), (1 if tb else 0,)), ((), ()))
    has_bias = bias is not None

    def body(*refs):
        a_ref, b_ref = refs[:2]
        bias_ref = refs[2] if has_bias else None
        o_ref = refs[3] if has_bias else refs[2]
        part = lax.dot_general(a_ref[...].astype(MXU), b_ref[...].astype(MXU), dims, preferred_element_type=F32)
        if nk == 1:
            o_ref[...] = part + bias_ref[...] if has_bias else part
            return
        acc_ref = refs[-1]
        k = pl.program_id(2)

        @pl.when(k == 0)
        def _():
            acc_ref[...] = part

        @pl.when(k > 0)
        def _():
            acc_ref[...] += part

        @pl.when(k == nk - 1)
        def _():
            out = acc_ref[...]
            if has_bias:
                out = out + bias_ref[...]
            o_ref[...] = out

    in_specs = [a_spec, b_spec]
    args = [a, b]
    if has_bias:
        in_specs.append(pl.BlockSpec((1, tn), lambda i, j, k: (0, j)))
        args.append(bias.reshape(1, n_dim))
    return pl.pallas_call(
        body, name=name, grid=(m_dim // tm, n_dim // tn, nk), in_specs=in_specs,
        out_specs=pl.BlockSpec((tm, tn), lambda i, j, k: (i, j)),
        out_shape=jax.ShapeDtypeStruct((m_dim, n_dim), F32),
        scratch_shapes=[] if nk == 1 else [pltpu.VMEM((tm, tn), F32)], compiler_params=_params(3),
    )(*args)


def _colsum(x, *, name):
    rows, n = x.shape
    tr = _pick(rows, 1024, 8)
    tn = _pick(n, 1024)

    def body(x_ref, o_ref):
        @pl.when(pl.program_id(1) == 0)
        def _():
            o_ref[...] = jnp.zeros_like(o_ref)

        o_ref[...] += jnp.sum(x_ref[...], axis=0, keepdims=True)

    out = pl.pallas_call(
        body, name=name, grid=(n // tn, rows // tr), in_specs=[pl.BlockSpec((tr, tn), lambda j, i: (i, j))],
        out_specs=pl.BlockSpec((1, tn), lambda j, i: (0, j)), out_shape=jax.ShapeDtypeStruct((1, n), F32),
        compiler_params=_params(2),
    )(x)
    return out[0]


@functools.partial(jax.custom_vjp, nondiff_argnums=(3,))
def linear(x, slot, w, name):
    return _mm(x, w, name=name + "_f")


def _linear_fwd(x, slot, w, name):
    return _mm(x, w, name=name + "_f"), (x, w)


def _linear_bwd(name, res, dy):
    x, w = res
    return _mm(dy, w, tb=True, name=name + "_dx"), _mm(x, dy, ta=True, name=name + "_dw"), jnp.zeros_like(w)


linear.defvjp(_linear_fwd, _linear_bwd)


@functools.partial(jax.custom_vjp, nondiff_argnums=(4,))
def linear_bias(x, slot, w, b, name):
    return _mm(x, w, bias=b, name=name + "_f")


def _linear_bias_fwd(x, slot, w, b, name):
    return _mm(x, w, bias=b, name=name + "_f"), (x, w)


def _linear_bias_bwd(name, res, dy):
    x, w = res
    return (_mm(dy, w, tb=True, name=name + "_dx"), _mm(x, dy, ta=True, name=name + "_dw"), jnp.zeros_like(w),
            _colsum(dy, name=name + "_db"))


linear_bias.defvjp(_linear_bias_fwd, _linear_bias_bwd)


@functools.partial(jax.custom_vjp, nondiff_argnums=(1,))
def split_cols(x, bounds):
    return tuple(x[:, a:b] for a, b in bounds)


def _split_cols_fwd(x, bounds):
    return split_cols(x, bounds), x.shape[1]


def _split_cols_bwd(bounds, width, cts):
    rows = cts[0].shape[0]
    parts, at = [], 0
    for (a, b), ct in zip(bounds, cts):
        if a > at:
            parts.append(jnp.zeros((rows, a - at), ct.dtype))
        parts.append(ct)
        at = b
    if at < width:
        parts.append(jnp.zeros((rows, width - at), cts[0].dtype))
    return (jnp.concatenate(parts, axis=1),)


split_cols.defvjp(_split_cols_fwd, _split_cols_bwd)


def _norm_stats(x, center, eps):
    if center:
        mu = jnp.mean(x, axis=-1, keepdims=True)
        xc = x - mu
    else:
        xc = x
    rstd = lax.rsqrt(jnp.mean(xc * xc, axis=-1, keepdims=True) + eps)
    return xc * rstd


def _norm_fwd_call(x, g, b, *, center, name):
    rows, d = x.shape
    tr = _pick(rows, 512, 8)
    eps = LN_EPS if center else RMS_EPS

    def body(x_ref, g_ref, b_ref, o_ref):
        xhat = _norm_stats(x_ref[...], center, eps)
        y = xhat * g_ref[...]
        if center:
            y = y + b_ref[...]
        o_ref[...] = y

    vec = pl.BlockSpec((1, d), lambda i: (0, 0))
    return pl.pallas_call(
        body, name=name, grid=(rows // tr,), in_specs=[pl.BlockSpec((tr, d), lambda i: (i, 0)), vec, vec],
        out_specs=pl.BlockSpec((tr, d), lambda i: (i, 0)), out_shape=jax.ShapeDtypeStruct((rows, d), F32),
        compiler_params=_params(1),
    )(x, g.reshape(1, d), b.reshape(1, d))


def _norm_bwd_call(x, g, dy, *, center, name):
    rows, d = x.shape
    tr = _pick(rows, 512, 8)
    eps = LN_EPS if center else RMS_EPS

    def body(x_ref, g_ref, dy_ref, dx_ref, dg_ref, db_ref):
        @pl.when(pl.program_id(0) == 0)
        def _():
            dg_ref[...] = jnp.zeros_like(dg_ref)
            db_ref[...] = jnp.zeros_like(db_ref)

        x = x_ref[...]
        dy = dy_ref[...]
        if center:
            mu = jnp.mean(x, axis=-1, keepdims=True)
            xc = x - mu
        else:
            xc = x
        rstd = lax.rsqrt(jnp.mean(xc * xc, axis=-1, keepdims=True) + eps)
        xhat = xc * rstd
        dyg = dy * g_ref[...]
        proj = jnp.mean(dyg * xhat, axis=-1, keepdims=True)
        dx = dyg - xhat * proj
        if center:
            dx = dx - jnp.mean(dyg, axis=-1, keepdims=True)
        dx_ref[...] = dx * rstd
        dg_ref[...] += jnp.sum(dy * xhat, axis=0, keepdims=True)
        db_ref[...] += jnp.sum(dy, axis=0, keepdims=True)

    vec = pl.BlockSpec((1, d), lambda i: (0, 0))
    row = pl.BlockSpec((tr, d), lambda i: (i, 0))
    dx, dg, db = pl.pallas_call(
        body, name=name, grid=(rows // tr,), in_specs=[row, vec, row], out_specs=[row, vec, vec],
        out_shape=[jax.ShapeDtypeStruct((rows, d), F32), jax.ShapeDtypeStruct((1, d), F32),
                   jax.ShapeDtypeStruct((1, d), F32)],
        compiler_params=_params(1),
    )(x, g.reshape(1, d), dy)
    return dx, dg[0], db[0]


@functools.partial(jax.custom_vjp, nondiff_argnums=(3,))
def layer_norm(x, g, b, name):
    return _norm_fwd_call(x, g, b, center=True, name=name + "_f")


def _layer_norm_fwd(x, g, b, name):
    return _norm_fwd_call(x, g, b, center=True, name=name + "_f"), (x, g)


def _layer_norm_bwd(name, res, dy):
    x, g = res
    return _norm_bwd_call(x, g, dy, center=True, name=name + "_b")


layer_norm.defvjp(_layer_norm_fwd, _layer_norm_bwd)


@functools.partial(jax.custom_vjp, nondiff_argnums=(2,))
def rms_norm(x, g, name):
    return _norm_fwd_call(x, g, jnp.zeros_like(g), center=False, name=name + "_f")


def _rms_norm_fwd(x, g, name):
    return _norm_fwd_call(x, g, jnp.zeros_like(g), center=False, name=name + "_f"), (x, g)


def _rms_norm_bwd(name, res, dy):
    x, g = res
    dx, dg, _ = _norm_bwd_call(x, g, dy, center=False, name=name + "_b")
    return dx, dg


rms_norm.defvjp(_rms_norm_fwd, _rms_norm_bwd)


def _conv_tiles(rows, ch):
    tr = _pick(rows, 512, CONV_HALO)
    assert tr >= CONV_HALO and rows % tr == 0
    return tr, _pick(ch, 512)


def _pad_taps(w):
    k = w.shape[0]
    kp = -(-k // 8) * 8
    return jnp.pad(w, ((0, kp - k), (0, 0))), k, kp


def _dwconv_fwd_call(x, w, *, name):
    rows, ch = x.shape
    tr, tc = _conv_tiles(rows, ch)
    wp, taps, kp = _pad_taps(w)
    assert taps - 1 <= CONV_HALO
    halo_per_tile = tr // CONV_HALO

    def body(x_ref, xprev_ref, w_ref, o_ref):
        i = pl.program_id(1)
        halo = xprev_ref[...]
        halo = jnp.where(i > 0, halo, jnp.zeros_like(halo))
        xx = jnp.concatenate([halo, x_ref[...]], axis=0)
        acc = jnp.zeros((tr, tc), F32)
        for k in range(taps):
            shift = taps - 1 - k
            term = xx if shift == 0 else pltpu.roll(xx, shift, 0)
            acc = acc + w_ref[k:k + 1, :] * term[CONV_HALO:, :]
        o_ref[...] = acc

    return pl.pallas_call(
        body, name=name, grid=(ch // tc, rows // tr),
        in_specs=[pl.BlockSpec((tr, tc), lambda j, i: (i, j)),
                  pl.BlockSpec((CONV_HALO, tc), lambda j, i: (jnp.maximum(i * halo_per_tile - 1, 0), j)),
                  pl.BlockSpec((kp, tc), lambda j, i: (0, j))],
        out_specs=pl.BlockSpec((tr, tc), lambda j, i: (i, j)), out_shape=jax.ShapeDtypeStruct((rows, ch), F32),
        compiler_params=_params(2),
    )(x, x, wp)


def _dwconv_bwd_call(x, w, dy, *, name):
    rows, ch = x.shape
    tr, tc = _conv_tiles(rows, ch)
    wp, taps, kp = _pad_taps(w)
    n_row_tiles = rows // tr
    halo_per_tile = tr // CONV_HALO
    n_halo_blocks = rows // CONV_HALO
    ext = tr + CONV_HALO

    def body(x_ref, xprev_ref, dy_ref, dynext_ref, w_ref, dx_ref, dw_ref):
        i = pl.program_id(1)

        @pl.when(i == 0)
        def _():
            dw_ref[...] = jnp.zeros_like(dw_ref)

        halo = xprev_ref[...]
        halo = jnp.where(i > 0, halo, jnp.zeros_like(halo))
        xx = jnp.concatenate([halo, x_ref[...]], axis=0)
        dy = dy_ref[...]
        ahead = dynext_ref[...]
        ahead = jnp.where(i < n_row_tiles - 1, ahead, jnp.zeros_like(ahead))
        yy = jnp.concatenate([dy, ahead], axis=0)
        dx = jnp.zeros((tr, tc), F32)
        for k in range(taps):
            shift = taps - 1 - k
            fwd = yy if shift == 0 else pltpu.roll(yy, ext - shift, 0)
            dx = dx + w_ref[k:k + 1, :] * fwd[:tr, :]
            back = xx if shift == 0 else pltpu.roll(xx, shift, 0)
            dw_ref[k:k + 1, :] += jnp.sum(dy * back[CONV_HALO:, :], axis=0, keepdims=True)
        dx_ref[...] = dx

    cur = pl.BlockSpec((tr, tc), lambda j, i: (i, j))
    dx, dw = pl.pallas_call(
        body, name=name, grid=(ch // tc, n_row_tiles),
        in_specs=[cur, pl.BlockSpec((CONV_HALO, tc), lambda j, i: (jnp.maximum(i * halo_per_tile - 1, 0), j)), cur,
                  pl.BlockSpec((CONV_HALO, tc),
                               lambda j, i: (jnp.minimum((i + 1) * halo_per_tile, n_halo_blocks - 1), j)),
                  pl.BlockSpec((kp, tc), lambda j, i: (0, j))],
        out_specs=[cur, pl.BlockSpec((kp, tc), lambda j, i: (0, j))],
        out_shape=[jax.ShapeDtypeStruct((rows, ch), F32), jax.ShapeDtypeStruct((kp, ch), F32)],
        compiler_params=_params(2),
    )(x, x, dy, dy, wp)
    return dx, dw[:taps]


@functools.partial(jax.custom_vjp, nondiff_argnums=(2,))
def dwconv(x, w, name):
    return _dwconv_fwd_call(x, w, name=name + "_f")


def _dwconv_fwd(x, w, name):
    return _dwconv_fwd_call(x, w, name=name + "_f"), (x, w)


def _dwconv_bwd(name, res, dy):
    x, w = res
    return _dwconv_bwd_call(x, w, dy, name=name + "_b")


dwconv.defvjp(_dwconv_fwd, _dwconv_bwd)


def _attn_tiles(seq):
    return _pick(seq, 1024), _pick(seq, 512)


def _lane_tile(v, width):
    return v if width == LANE else jnp.tile(v, (1, width // LANE))


def _causal_pairs(nq, nk, tq, tk, key_major):
    pairs = [(i, j) for i in range(nq) for j in range(nk) if j * tk <= i * tq + tq - 1]
    if key_major:
        pairs.sort(key=lambda p: (p[1], p[0]))
    return (jnp.asarray(np.array([p[0] for p in pairs], np.int32)),
            jnp.asarray(np.array([p[1] for p in pairs], np.int32)))


def _scores(q, k, i, j, tq, tk, masked):
    z = lax.dot_general(q, k, (((1,), (1,)), ((), ())), preferred_element_type=F32) * (ATTN_SCALE * LOG2_E)
    if masked:
        row = i * tq + lax.broadcasted_iota(jnp.int32, (tq, tk), 0)
        col = j * tk + lax.broadcasted_iota(jnp.int32, (tq, tk), 1)
        z = jnp.where(col <= row, z, -jnp.inf)
    return z


def _attn_fwd_call(q, k, v, *, name):
    heads, seq, dh = q.shape
    tq, tk = _attn_tiles(seq)
    nq, nk = seq // tq, seq // tk
    qi, kj = _causal_pairs(nq, nk, tq, tk, key_major=False)

    def body(qi_ref, kj_ref, q_ref, k_ref, v_ref, o_ref, lse_ref, m_sc, l_sc, acc_sc):
        t = pl.program_id(1)
        i = qi_ref[t]
        j = kj_ref[t]

        @pl.when(j == 0)
        def _():
            m_sc[...] = jnp.full(m_sc.shape, -jnp.inf, F32)
            l_sc[...] = jnp.zeros_like(l_sc)
            acc_sc[...] = jnp.zeros_like(acc_sc)

        def step(masked):
            z = _scores(q_ref[0], k_ref[0], i, j, tq, tk, masked)
            m_prev = m_sc[...]
            m_new = jnp.maximum(m_prev, jnp.max(z, axis=1, keepdims=True))
            alpha = jnp.exp2(m_prev - m_new)
            p = jnp.exp2(z - _lane_tile(m_new, tk))
            l_sc[...] = alpha * l_sc[...] + jnp.sum(p, axis=1, keepdims=True)
            acc_sc[...] = alpha * acc_sc[...] + jnp.dot(p.astype(MXU), v_ref[0], preferred_element_type=F32)
            m_sc[...] = m_new

        crosses_diagonal = j * tk + tk - 1 > i * tq

        @pl.when(jnp.logical_not(crosses_diagonal))
        def _():
            step(False)

        @pl.when(crosses_diagonal)
        def _():
            step(True)

        @pl.when(j == (i * tq + tq - 1) // tk)
        def _():
            o_ref[0] = acc_sc[...] / l_sc[...]
            lse_ref[0] = m_sc[...] + jnp.log(l_sc[...]) * LOG2_E

    q_spec = pl.BlockSpec((1, tq, dh), lambda h, t, qi_ref, kj_ref: (h, qi_ref[t], 0))
    kv_spec = pl.BlockSpec((1, tk, dh), lambda h, t, qi_ref, kj_ref: (h, kj_ref[t], 0))
    return pl.pallas_call(
        body, name=name,
        grid_spec=pltpu.PrefetchScalarGridSpec(
            num_scalar_prefetch=2, grid=(heads, int(qi.shape[0])), in_specs=[q_spec, kv_spec, kv_spec],
            out_specs=[q_spec, q_spec],
            scratch_shapes=[pltpu.VMEM((tq, LANE), F32), pltpu.VMEM((tq, LANE), F32), pltpu.VMEM((tq, dh), F32)]),
        out_shape=[jax.ShapeDtypeStruct((heads, seq, dh), F32), jax.ShapeDtypeStruct((heads, seq, LANE), F32)],
        compiler_params=_params(2),
    )(qi, kj, q, k, v)


def _attn_bwd_call(q, k, v, o, lse, do, *, name):
    heads, seq, dh = q.shape
    tk, tq = _attn_tiles(seq)
    nq, nk = seq // tq, seq // tk
    qi, kj = _causal_pairs(nq, nk, tq, tk, key_major=True)

    def body(qi_ref, kj_ref, q_ref, k_ref, v_ref, o_ref, lse_ref, do_ref, dq_ref, dk_ref, dv_ref, dk_sc, dv_sc):
        t = pl.program_id(1)
        i = qi_ref[t]
        j = kj_ref[t]
        first_i = (j * tk) // tq

        @pl.when(t == 0)
        def _():
            dq_ref[...] = jnp.zeros_like(dq_ref)

        @pl.when(i == first_i)
        def _():
            dk_sc[...] = jnp.zeros_like(dk_sc)
            dv_sc[...] = jnp.zeros_like(dv_sc)

        def step(masked):
            qb = q_ref[0]
            kb = k_ref[0]
            do_f = do_ref[0]
            do_b = do_f.astype(MXU)
            p = jnp.exp2(_scores(qb, kb, i, j, tq, tk, masked) - _lane_tile(lse_ref[0], tk))
            dp = lax.dot_general(do_b, v_ref[0], (((1,), (1,)), ((), ())), preferred_element_type=F32)
            delta = jnp.sum(do_f * o_ref[0], axis=1, keepdims=True)
            ds = (p * (dp - delta) * ATTN_SCALE).astype(MXU)
            dv_sc[...] += lax.dot_general(p.astype(MXU), do_b, (((0,), (0,)), ((), ())), preferred_element_type=F32)
            dk_sc[...] += lax.dot_general(ds, qb, (((0,), (0,)), ((), ())), preferred_element_type=F32)
            rows = pl.ds(pl.multiple_of(i * tq, tq), tq)
            dq_ref[0, rows, :] += jnp.dot(ds, kb, preferred_element_type=F32)

        crosses_diagonal = j * tk + tk - 1 > i * tq

        @pl.when(jnp.logical_not(crosses_diagonal))
        def _():
            step(False)

        @pl.when(crosses_diagonal)
        def _():
            step(True)

        @pl.when(i == nq - 1)
        def _():
            dk_ref[0] = dk_sc[...]
            dv_ref[0] = dv_sc[...]

    def q_map(h, t, qi_ref, kj_ref):
        return (h, qi_ref[t], 0)

    def kv_map(h, t, qi_ref, kj_ref):
        return (h, kj_ref[t], 0)

    q_spec = pl.BlockSpec((1, tq, dh), q_map)
    lse_spec = pl.BlockSpec((1, tq, LANE), q_map)
    kv_spec = pl.BlockSpec((1, tk, dh), kv_map)
    head_spec = pl.BlockSpec((1, seq, dh), lambda h, t, qi_ref, kj_ref: (h, 0, 0))
    return pl.pallas_call(
        body, name=name,
        grid_spec=pltpu.PrefetchScalarGridSpec(
            num_scalar_prefetch=2, grid=(heads, int(qi.shape[0])),
            in_specs=[q_spec, kv_spec, kv_spec, q_spec, lse_spec, q_spec], out_specs=[head_spec, kv_spec, kv_spec],
            scratch_shapes=[pltpu.VMEM((tk, dh), F32), pltpu.VMEM((tk, dh), F32)]),
        out_shape=[jax.ShapeDtypeStruct((heads, seq, dh), F32)] * 3,
        compiler_params=_params(2),
    )(qi, kj, q, k, v, o, lse, do)


@functools.partial(jax.custom_vjp, nondiff_argnums=(3,))
def attention(q, k, v, name):
    return _attn_fwd_call(q.astype(MXU), k.astype(MXU), v.astype(MXU), name=name + "_f")[0]


def _attention_fwd(q, k, v, name):
    qb, kb, vb = q.astype(MXU), k.astype(MXU), v.astype(MXU)
    o, lse = _attn_fwd_call(qb, kb, vb, name=name + "_f")
    return o, (qb, kb, vb, o, lse)


def _attention_bwd(name, res, do):
    qb, kb, vb, o, lse = res
    return tuple(_attn_bwd_call(qb, kb, vb, o, lse, do, name=name + "_b"))


attention.defvjp(_attention_fwd, _attention_bwd)


def _merge_tiles(rows):
    return _pick(rows, 128, 8)


def _gate_merge_fwd_call(gates, ys, *, name):
    rows, d = ys[0].shape
    nb = len(ys)
    tr = _merge_tiles(rows)

    def body(g_ref, *refs):
        o_ref = refs[nb]
        acc = jax.nn.sigmoid(g_ref[:, 0:d]) * refs[0][...]
        for b in range(1, nb):
            acc = acc + jax.nn.sigmoid(g_ref[:, b * d:(b + 1) * d]) * refs[b][...]
        o_ref[...] = acc

    wide = pl.BlockSpec((tr, nb * d), lambda i: (i, 0))
    row = pl.BlockSpec((tr, d), lambda i: (i, 0))
    return pl.pallas_call(
        body, name=name, grid=(rows // tr,), in_specs=[wide] + [row] * nb, out_specs=row,
        out_shape=jax.ShapeDtypeStruct((rows, d), F32), compiler_params=_params(1),
    )(gates, *ys)


def _gate_merge_bwd_call(gates, ys, dm, *, name):
    rows, d = ys[0].shape
    nb = len(ys)
    tr = _merge_tiles(rows)

    def body(g_ref, *refs):
        y_refs = refs[:nb]
        dm_ref = refs[nb]
        dg_ref = refs[nb + 1]
        dy_refs = refs[nb + 2:]
        dm = dm_ref[...]
        for b in range(nb):
            gt = jax.nn.sigmoid(g_ref[:, b * d:(b + 1) * d])
            dy_refs[b][...] = gt * dm
            dg_ref[:, b * d:(b + 1) * d] = dm * y_refs[b][...] * gt * (1.0 - gt)

    wide = pl.BlockSpec((tr, nb * d), lambda i: (i, 0))
    row = pl.BlockSpec((tr, d), lambda i: (i, 0))
    outs = pl.pallas_call(
        body, name=name, grid=(rows // tr,), in_specs=[wide] + [row] * (nb + 1), out_specs=[wide] + [row] * nb,
        out_shape=[jax.ShapeDtypeStruct((rows, nb * d), F32)] + [jax.ShapeDtypeStruct((rows, d), F32)] * nb,
        compiler_params=_params(1),
    )(gates, *ys, dm)
    return outs[0], tuple(outs[1:])


@functools.partial(jax.custom_vjp, nondiff_argnums=(2,))
def gate_merge(gates, ys, name):
    return _gate_merge_fwd_call(gates, ys, name=name + "_f")


def _gate_merge_fwd(gates, ys, name):
    return _gate_merge_fwd_call(gates, ys, name=name + "_f"), (gates, ys)


def _gate_merge_bwd(name, res, dm):
    gates, ys = res
    return _gate_merge_bwd_call(gates, ys, dm, name=name + "_b")


gate_merge.defvjp(_gate_merge_fwd, _gate_merge_bwd)


def _loss_call(y, target):
    rows, d = y.shape
    tr = _pick(rows, 512, 8)

    def body(y_ref, t_ref, dy_ref, loss_ref):
        @pl.when(pl.program_id(0) == 0)
        def _():
            loss_ref[...] = jnp.zeros_like(loss_ref)

        err = y_ref[...] - t_ref[...]
        dy_ref[...] = err * (1.0 / d)
        loss_ref[...] += jnp.sum(err * err, axis=0, keepdims=True)

    row = pl.BlockSpec((tr, d), lambda i: (i, 0))
    vec = pl.BlockSpec((1, d), lambda i: (0, 0))
    dy, part = pl.pallas_call(
        body, name="loss_head", grid=(rows // tr,), in_specs=[row, row], out_specs=[row, vec],
        out_shape=[jax.ShapeDtypeStruct((rows, d), F32), jax.ShapeDtypeStruct((1, d), F32)],
        compiler_params=_params(1),
    )(y, target)
    return 0.5 * jnp.sum(part) / d, dy


ADAMW_BLOCK_ELEMS = 256 * 1024


def _adamw_call(w, g, m, v, *, name):
    shape = w.shape
    width = shape[-1]
    rows = w.size // width
    w, g, m, v = (a.reshape(rows, width) for a in (w, g, m, v))
    tr = _pick(rows, max(8, ADAMW_BLOCK_ELEMS // width), 8)
    c1 = 1.0 - ADAM_B1 ** ADAM_STEP
    c2 = 1.0 - ADAM_B2 ** ADAM_STEP

    def body(w_ref, g_ref, m_ref, v_ref, d_ref, nm_ref, nv_ref):
        g = g_ref[...]
        m = ADAM_B1 * m_ref[...] + (1.0 - ADAM_B1) * g
        v = ADAM_B2 * v_ref[...] + (1.0 - ADAM_B2) * (g * g)
        m_hat = m / c1
        v_hat = v / c2
        d_ref[...] = -ADAM_LR * (m_hat / (jnp.sqrt(v_hat) + ADAM_EPS) + ADAM_WD * w_ref[...])
        nm_ref[...] = m
        nv_ref[...] = v

    row = pl.BlockSpec((tr, width), lambda i: (i, 0))
    outs = pl.pallas_call(
        body, name=name, grid=(rows // tr,), in_specs=[row] * 4, out_specs=[row] * 3,
        out_shape=[jax.ShapeDtypeStruct((rows, width), F32)] * 3, compiler_params=_params(1),
    )(w, g, m, v)
    return [o.reshape(shape) for o in outs]


def _ordered_sum(y, *, name):
    n, rows, width = y.shape
    tr = _pick(rows, 256, 8)

    def body(y_ref, o_ref):
        acc = y_ref[0]
        for s in range(1, n):
            acc = acc + y_ref[s]
        o_ref[...] = acc

    return pl.pallas_call(
        body, name=name, grid=(rows // tr,), in_specs=[pl.BlockSpec((n, tr, width), lambda i: (0, i, 0))],
        out_specs=pl.BlockSpec((tr, width), lambda i: (i, 0)), out_shape=jax.ShapeDtypeStruct((rows, width), y.dtype),
        compiler_params=_params(1),
    )(y)


_GROUP_FLIPS = {
    "chips": ((1, 0, 0), (0, 1, 0), (1, 1, 0)),
    "cores": ((0, 0, 1),),
    "all": ((0, 0, 1), (0, 1, 0), (0, 1, 1), (1, 0, 0), (1, 0, 1), (1, 1, 0), (1, 1, 1)),
}


def _exchange(x, *, group, broadcast, name):
    flips = _GROUP_FLIPS[group]
    n = len(flips) + 1
    block = x.shape if broadcast else x.shape[1:]
    if not broadcast:
        assert x.shape[0] == n

    def body(x_ref, o_ref, send_sems, recv_sems, local_sem):
        mx, my, mc = lax.axis_index("x"), lax.axis_index("y"), lax.axis_index("c")

        def index(px, py, pc):
            return {"chips": 2 * px + py, "cores": pc, "all": 4 * px + 2 * py + pc}[group]

        def block_for(d):
            return x_ref if broadcast else x_ref.at[d]

        me = index(mx, my, mc)
        mine = pltpu.make_async_copy(block_for(me), o_ref.at[me], local_sem)
        mine.start()
        sends, recvs = [], []
        for k, (fx, fy, fc) in enumerate(flips):
            px = 1 - mx if fx else mx
            py = 1 - my if fy else my
            pc = 1 - mc if fc else mc
            peer = index(px, py, pc)
            sends.append(pltpu.make_async_remote_copy(
                src_ref=block_for(peer), dst_ref=o_ref.at[me], send_sem=send_sems.at[k], recv_sem=recv_sems.at[k],
                device_id=(px, py, pc), device_id_type=MESH_T))
            recvs.append(pltpu.make_async_remote_copy(
                src_ref=block_for(peer), dst_ref=o_ref.at[peer], send_sem=send_sems.at[k], recv_sem=recv_sems.at[k],
                device_id=(px, py, pc), device_id_type=MESH_T))
        for cp in sends:
            cp.start()
        for cp in recvs:
            cp.wait_recv()
        for cp in sends:
            cp.wait_send()
        mine.wait()

    any_spec = pl.BlockSpec(memory_space=pl.ANY)
    return pl.pallas_call(
        body, name=name, in_specs=[any_spec], out_specs=any_spec,
        out_shape=jax.ShapeDtypeStruct((n,) + tuple(block), x.dtype),
        scratch_shapes=[pltpu.SemaphoreType.DMA((n - 1,)), pltpu.SemaphoreType.DMA((n - 1,)), pltpu.SemaphoreType.DMA],
    )(x)


def _sibling():
    return (lax.axis_index("x"), lax.axis_index("y"), 1 - lax.axis_index("c"))


def _run_copies(copies):
    for cp in copies:
        cp.start()
    for cp in copies:
        cp.wait_recv()
    for cp in copies:
        cp.wait_send()


def _dma_call(body, arrays, out_shapes, n_copies, *, name, extra_scratch=()):
    any_spec = pl.BlockSpec(memory_space=pl.ANY)
    return pl.pallas_call(
        body, name=name, in_specs=[any_spec] * len(arrays), out_specs=[any_spec] * len(out_shapes),
        out_shape=out_shapes,
        scratch_shapes=[pltpu.SemaphoreType.DMA((n_copies,)), pltpu.SemaphoreType.DMA((n_copies,))]
        + list(extra_scratch),
    )(*arrays)


def _gather_shards(shards, *, name):
    flips = _GROUP_FLIPS["chips"]
    n = len(shards)

    def body(*refs):
        ins, outs = refs[:n], refs[n:2 * n]
        send_sems, recv_sems, local_sems = refs[2 * n:]
        mx, my, mc = lax.axis_index("x"), lax.axis_index("y"), lax.axis_index("c")
        me = 2 * mx + my
        local = [pltpu.make_async_copy(ins[t], outs[t].at[me], local_sems.at[t]) for t in range(n)]
        for cp in local:
            cp.start()
        sends, recvs = [], []
        for t in range(n):
            for k, (fx, fy, _) in enumerate(flips):
                px = 1 - mx if fx else mx
                py = 1 - my if fy else my
                sems = dict(send_sem=send_sems.at[t * len(flips) + k], recv_sem=recv_sems.at[t * len(flips) + k],
                            device_id=(px, py, mc), device_id_type=MESH_T)
                sends.append(pltpu.make_async_remote_copy(src_ref=ins[t], dst_ref=outs[t].at[me], **sems))
                recvs.append(pltpu.make_async_remote_copy(src_ref=ins[t], dst_ref=outs[t].at[2 * px + py], **sems))
        for cp in sends:
            cp.start()
        for cp in recvs:
            cp.wait_recv()
        for cp in sends:
            cp.wait_send()
        for cp in local:
            cp.wait()

    return _dma_call(body, shards, [jax.ShapeDtypeStruct((4,) + a.shape, a.dtype) for a in shards], n * len(flips),
                     name=name, extra_scratch=[pltpu.SemaphoreType.DMA((n,))])


def _pair_send(gs, *, name):
    n = len(gs)

    def body(*refs):
        ins, outs = refs[:n], refs[n:2 * n]
        send_sems, recv_sems = refs[2 * n:]
        copies = []
        for t in range(n):
            half = gs[t].shape[1] // 2
            theirs = (1 - lax.axis_index("c")) * half
            for s in range(4):
                copies.append(pltpu.make_async_remote_copy(
                    src_ref=ins[t].at[s, pl.ds(theirs, half), :], dst_ref=outs[t].at[s],
                    send_sem=send_sems.at[4 * t + s], recv_sem=recv_sems.at[4 * t + s],
                    device_id=_sibling(), device_id_type=MESH_T))
        _run_copies(copies)

    return _dma_call(body, gs, [jax.ShapeDtypeStruct((4, a.shape[1] // 2, a.shape[2]), a.dtype) for a in gs], 4 * n,
                     name=name)


def _quad_send(ps, *, name):
    flips = _GROUP_FLIPS["chips"]
    n = len(ps)

    def body(*refs):
        ins, outs = refs[:n], refs[n:2 * n]
        send_sems, recv_sems = refs[2 * n:]
        mx, my, mc = lax.axis_index("x"), lax.axis_index("y"), lax.axis_index("c")
        copies = []
        for t in range(n):
            for k, (fx, fy, _) in enumerate(flips):
                px = 1 - mx if fx else mx
                py = 1 - my if fy else my
                copies.append(pltpu.make_async_remote_copy(
                    src_ref=ins[t].at[2 * px + py], dst_ref=outs[t].at[k], send_sem=send_sems.at[3 * t + k],
                    recv_sem=recv_sems.at[3 * t + k], device_id=(px, py, mc), device_id_type=MESH_T))
        _run_copies(copies)

    return _dma_call(body, ps, [jax.ShapeDtypeStruct((3,) + a.shape[1:], a.dtype) for a in ps], 3 * n, name=name)


def _sibling_send(ms, *, name):
    n = len(ms)

    def body(*refs):
        ins, outs = refs[:n], refs[n:2 * n]
        send_sems, recv_sems = refs[2 * n:]
        _run_copies([pltpu.make_async_remote_copy(
            src_ref=ins[t], dst_ref=outs[t], send_sem=send_sems.at[t], recv_sem=recv_sems.at[t],
            device_id=_sibling(), device_id_type=MESH_T) for t in range(n)])

    return _dma_call(body, ms, [jax.ShapeDtypeStruct(a.shape, a.dtype) for a in ms], n, name=name)


SUM_BLOCK_ELEMS = 256 * 1024


def _sum_tile(rows, width):
    return _pick(rows, max(16, SUM_BLOCK_ELEMS // width), 16)


def _pair_sum(g, got, core, *, name):
    _, rows, width = g.shape
    half = rows // 2
    tr = _sum_tile(half, width)
    tiles = half // tr

    def body(core_ref, g_ref, got_ref, o_ref):
        o_ref[...] = (g_ref[...] + got_ref[...]).astype(BF16)

    blk = pl.BlockSpec((1, tr, width), lambda s, i, core_ref: (s, i, 0))
    return pl.pallas_call(
        body, name=name,
        grid_spec=pltpu.PrefetchScalarGridSpec(
            num_scalar_prefetch=1, grid=(4, tiles),
            in_specs=[pl.BlockSpec((1, tr, width), lambda s, i, core_ref: (s, core_ref[0] * tiles + i, 0)), blk],
            out_specs=blk),
        out_shape=jax.ShapeDtypeStruct((4, half, width), BF16), compiler_params=_params(2),
    )(core, g, got)


def _quad_sum(pair, others, chip, *, name):
    _, rows, width = pair.shape
    tr = _sum_tile(rows, width)

    def body(chip_ref, p_ref, o3_ref, o_ref):
        acc = p_ref[0].astype(F32)
        for k in range(3):
            acc = acc + o3_ref[k].astype(F32)
        o_ref[...] = acc

    return pl.pallas_call(
        body, name=name,
        grid_spec=pltpu.PrefetchScalarGridSpec(
            num_scalar_prefetch=1, grid=(rows // tr,),
            in_specs=[pl.BlockSpec((1, tr, width), lambda i, chip_ref: (chip_ref[0], i, 0)),
                      pl.BlockSpec((3, tr, width), lambda i, chip_ref: (0, i, 0))],
            out_specs=pl.BlockSpec((tr, width), lambda i, chip_ref: (i, 0))),
        out_shape=jax.ShapeDtypeStruct((rows, width), F32), compiler_params=_params(1),
    )(chip, pair, others)


def _reduce_scatter(gs, *, tag):
    mc = lax.axis_index("c")
    core = mc.reshape(1).astype(jnp.int32)
    chip = (2 * lax.axis_index("x") + lax.axis_index("y")).reshape(1).astype(jnp.int32)
    got = _pair_send(gs, name=tag + "_pair")
    pairs = [_pair_sum(g, r, core, name=tag + "_pair_sum") for g, r in zip(gs, got)]
    others = _quad_send(pairs, name=tag + "_quad")
    mine = [_quad_sum(p, o, chip, name=tag + "_quad_sum") for p, o in zip(pairs, others)]
    theirs = _sibling_send(mine, name=tag + "_share")
    return [jnp.where(mc == 0, jnp.concatenate([m, t], axis=0), jnp.concatenate([t, m], axis=0))
            for m, t in zip(mine, theirs)]


def _round_up(n, q):
    return -(-n // q) * q


def _pack_rows(flat, quantum_rows):
    n = flat.shape[-1]
    total = _round_up(n, PACK_W * quantum_rows)
    pad = [(0, 0)] * (flat.ndim - 1) + [(0, total - n)]
    return jnp.pad(flat, pad).reshape(flat.shape[:-1] + (total // PACK_W, PACK_W))


def _split_shards(full, axis):
    if axis == 0:
        return full.reshape((4, full.shape[0] // 4, full.shape[1]))
    width = full.shape[1] // 4
    return jnp.stack([full[:, s * width:(s + 1) * width] for s in range(4)])


def _join_shards(blocks, axis):
    if axis == 0:
        return blocks.reshape((4 * blocks.shape[1], blocks.shape[2]))
    return jnp.concatenate([blocks[s] for s in range(4)], axis=1)


def _pack_small(tree, names):
    return _pack_rows(jnp.concatenate([tree[name].reshape(-1) for name in names]), 8)


def _unpack_small(packed, shapes, names):
    lead = packed.shape[:-2]
    flat = packed.reshape(lead + (-1,))
    out, off = {}, 0
    for name in names:
        n = int(np.prod(shapes[name]))
        out[name] = flat[..., off:off + n].reshape(lead + tuple(shapes[name]))
        off += n
    return out


def _rope(t, cos, sin):
    half = QK_ROPE // 2
    t1, t2 = t[..., :half], t[..., half:]
    return jnp.concatenate([t1 * cos - t2 * sin, t2 * cos + t1 * sin], axis=-1)


def _pad_in_cols(a):
    z = jnp.zeros(a.shape[:-1] + (IN_PAD,), a.dtype)
    return jnp.concatenate([a[..., :IN_PAD_AT], z, a[..., IN_PAD_AT:]], axis=-1)


def _pool_constants(seq):
    taps = max(POOL_WINDOWS)
    win = jnp.repeat(jnp.asarray(POOL_WINDOWS, jnp.int32), 512 // len(POOL_WINDOWS))
    lag = taps - 1 - jnp.arange(taps, dtype=jnp.int32)
    mask = (lag[:, None] < win[None, :]).astype(F32)
    cnt = jnp.minimum(jnp.arange(seq, dtype=jnp.int32)[:, None] + 1, win[None, :]).astype(F32)
    return mask, 1.0 / cnt


def _block_diag(w):
    g, n, _ = w.shape
    out = jnp.zeros((g * n, g * n), w.dtype)
    for i in range(g):
        out = lax.dynamic_update_slice(out, w[i], (i * n, i * n))
    return out


def _layer(x, c_act, p, wb, cos, sin, pool_mask, pool_inv_cnt, tag):
    seq = x.shape[0]
    d = D_MODEL
    mod = linear_bias(c_act, p["w_ada"], wb["w_ada"], p["b_ada"], tag + "ada")[0]
    sh1, sc1, g1, sh2, sc2, g2 = [mod[i * d:(i + 1) * d][None, :] for i in range(6)]

    h = x * (1.0 + sc1) + sh1
    proj = linear_bias(h, _pad_in_cols(p["w_in"]), _pad_in_cols(wb["w_in"]), _pad_in_cols(p["b_in"]),
                       tag + "in")
    conv_a, conv_b, sc_bg, sc_cg, sc_x, q_lat, kv_lat, k_rope, pool_u, gates = split_cols(proj, IN_SPLITS)

    ya = conv_a * jax.nn.sigmoid(conv_b)
    ya = dwconv(ya, p["conv_dw"], tag + "convA")
    ya = jax.nn.silu(layer_norm(ya, p["conv_ln_g"], p["conv_ln_b"], tag + "convA_ln"))
    ya = linear(ya, p["w_conv_out"], wb["w_conv_out"], tag + "convA_out")

    yb = linear(sc_bg * dwconv(sc_cg * sc_x, p["sc_dw"], tag + "sc"), p["w_sc_out"], wb["w_sc_out"],
                tag + "sc_out")

    q = linear(rms_norm(q_lat, p["q_norm_g"], tag + "q_rms"), p["w_uq"], wb["w_uq"], tag + "uq")
    q = q.reshape(seq, N_HEADS, QK_NOPE + QK_ROPE)
    kv = linear(rms_norm(kv_lat, p["kv_norm_g"], tag + "kv_rms"), p["w_ukv"], wb["w_ukv"], tag + "ukv")
    kv = kv.reshape(seq, N_HEADS, QK_NOPE + V_DIM)
    q_rope = _rope(q[..., QK_NOPE:], cos[:, None, :], sin[:, None, :])
    k_rope_r = jnp.broadcast_to(_rope(k_rope, cos, sin)[:, None, :], (seq, N_HEADS, QK_ROPE))
    zq = jnp.zeros((seq, N_HEADS, HEAD_PAD - QK_NOPE - QK_ROPE), F32)
    zv = jnp.zeros((seq, N_HEADS, HEAD_PAD - V_DIM), F32)
    qh = jnp.concatenate([q[..., :QK_NOPE], q_rope, zq], axis=-1).transpose(1, 0, 2)
    kh = jnp.concatenate([kv[..., :QK_NOPE], k_rope_r, zq], axis=-1).transpose(1, 0, 2)
    vh = jnp.concatenate([kv[..., QK_NOPE:], zv], axis=-1).transpose(1, 0, 2)
    att = attention(qh, kh, vh, tag + "attn")
    att = att[:, :, :V_DIM].transpose(1, 0, 2).reshape(seq, N_HEADS * V_DIM)
    yc = linear(att, p["w_mla_out"], wb["w_mla_out"], tag + "mla_out")

    pd = dwconv(pool_u, pool_mask, tag + "pool") * pool_inv_cnt - pool_u
    w_mix = _block_diag(p["w_pool"])
    yd = linear(pd, w_mix, w_mix, tag + "pool_mix") * p["pool_scale"][None, :]
    yd = linear(yd, p["w_pool_out"], wb["w_pool_out"], tag + "pool_out")

    merged = gate_merge(gates, (ya, yb, yc, yd), tag + "merge")
    mix = linear(merged, p["w_o"], wb["w_o"], tag + "o")
    x = layer_norm(ALPHA * x + (1.0 + g1) * mix, p["ln1_g"], p["ln1_b"], tag + "ln1")

    h = x * (1.0 + sc2) + sh2
    up = dwconv(linear(h, p["w_up"], wb["w_up"], tag + "up"), p["ffn_dw"], tag + "ffn_conv")
    d_ff = up.shape[1] // 2
    val, gate = split_cols(up, ((0, d_ff), (d_ff, 2 * d_ff)))
    ffn = linear(jax.nn.silu(gate) * val, p["w_down"], wb["w_down"], tag + "down")
    return layer_norm(ALPHA * x + (1.0 + g2) * ffn, p["ln2_g"], p["ln2_b"], tag + "ln2")


def _forward(x, layers, gathered, c_act, cos, sin, pool_mask, pool_inv_cnt):
    for p, wb in zip(layers, gathered):
        x = _layer(x, c_act, p, wb, cos, sin, pool_mask, pool_inv_cnt, "")
    return x


def kernel(x, c, positions, w_ada, b_ada, w_in, b_in, conv_dw, conv_ln_g, conv_ln_b, w_conv_out, sc_dw, w_sc_out, q_norm_g, w_uq, kv_norm_g, w_ukv, w_mla_out, w_pool, pool_scale, w_pool_out, w_o, ln1_g, ln1_b, w_up, ffn_dw, w_down, ln2_g, ln2_b, loss_target, m_w_ada, m_b_ada, m_w_in, m_b_in, m_conv_dw, m_conv_ln_g, m_conv_ln_b, m_w_conv_out, m_sc_dw, m_w_sc_out, m_q_norm_g, m_w_uq, m_kv_norm_g, m_w_ukv, m_w_mla_out, m_w_pool, m_pool_scale, m_w_pool_out, m_w_o, m_ln1_g, m_ln1_b, m_w_up, m_ffn_dw, m_w_down, m_ln2_g, m_ln2_b, v_w_ada, v_b_ada, v_w_in, v_b_in, v_conv_dw, v_conv_ln_g, v_conv_ln_b, v_w_conv_out, v_sc_dw, v_w_sc_out, v_q_norm_g, v_w_uq, v_kv_norm_g, v_w_ukv, v_w_mla_out, v_w_pool, v_pool_scale, v_w_pool_out, v_w_o, v_ln1_g, v_ln1_b, v_w_up, v_ffn_dw, v_w_down, v_ln2_g, v_ln2_b):
    given = dict(locals())
    weights = {n: given[n] for n in WEIGHT_ORDER}
    mom_m = {n: given["m_" + n] for n in WEIGHT_ORDER}
    mom_v = {n: given["v_" + n] for n in WEIGHT_ORDER}
    depth = w_ada.shape[0]
    seq = x.shape[1]
    big = [(s[0], s[2]) for s in SHARDED if s[3]]
    tiny = [s[0] for s in SHARDED if not s[3]]
    chip = 2 * lax.axis_index("x") + lax.axis_index("y")

    tiny_shapes = {n: weights[n].shape for n in tiny}
    tiny_all = _exchange(_pack_small(weights, tiny), group="chips", broadcast=True, name="gather_taps")
    tiny_full = {n: jnp.concatenate([a[s] for s in range(4)], axis=-1)
                 for n, a in _unpack_small(tiny_all, tiny_shapes, tiny).items()}
    layers, gathered = [], []
    for l in range(depth):
        blocks = _gather_shards([weights[n][l].astype(BF16) for n, _ in big], name="gather_weights")
        gathered.append({n: _join_shards(a, axis) for (n, axis), a in zip(big, blocks)})
        p = {n: jnp.zeros(gathered[l][n].shape, F32) for n, _ in big}
        for n in tiny:
            p[n] = tiny_full[n][l]
        for n in REPLICATED:
            p[n] = weights[n][l]
        layers.append(p)

    inv = 1.0 / (ROPE_THETA ** (jnp.arange(0, QK_ROPE, 2, dtype=F32) / QK_ROPE))
    ang = positions[0].astype(F32)[:, None] * inv
    cos, sin = jnp.cos(ang), jnp.sin(ang)
    c_act = jnp.pad(jax.nn.silu(c), ((0, 15), (0, 0)))
    pool_mask, pool_inv_cnt = _pool_constants(seq)

    y, vjp_fn = jax.vjp(lambda xx, ll: _forward(xx, ll, gathered, c_act, cos, sin, pool_mask, pool_inv_cnt),
                        x[0], layers)
    loss_local, dy = _loss_call(y, loss_target[0])
    grad_x, grad_layers = vjp_fn(dy)
    loss = lax.psum(loss_local, ("x", "y", "c"))

    outs = {"grad": {}, "delta": {}, "m": {}, "v": {}}
    reduced = [_reduce_scatter([_split_shards(grad_layers[l][n], axis) for n, axis in big], tag="rs")
               for l in range(depth)]
    for t, (n, _) in enumerate(big):
        outs["grad"][n] = jnp.stack([reduced[l][t] for l in range(depth)])
    small = list(REPLICATED) + tiny
    small_local = {n: jnp.stack([grad_layers[l][n] for l in range(depth)]) for n in small}
    small_shapes = {n: small_local[n].shape for n in small}
    small_all = _exchange(_pack_small(small_local, small), group="all", broadcast=True, name="gather_small_grads")
    small_sum = _unpack_small(_ordered_sum(small_all, name="small_grads_sum"), small_shapes, small)
    for n in REPLICATED:
        outs["grad"][n] = small_sum[n]
    for n in tiny:
        width = weights[n].shape[-1]
        outs["grad"][n] = lax.dynamic_slice_in_dim(small_sum[n], chip * width, width, axis=2)

    for n in WEIGHT_ORDER:
        outs["delta"][n], outs["m"][n], outs["v"][n] = _adamw_call(
            weights[n], outs["grad"][n], mom_m[n], mom_v[n], name="adamw_" + n)

    result = [loss, grad_x[None]]
    for key in ("grad", "delta", "m", "v"):
        result.extend(outs[key][n] for n in WEIGHT_ORDER)
    return tuple(result)
```

```python
import functools
import math

import jax
import jax.numpy as jnp
import numpy as np
from jax import lax
from jax.experimental import pallas as pl
from jax.experimental.pallas import tpu as pltpu

F32 = jnp.float32
BF16 = jnp.bfloat16
MXU = jnp.bfloat16

D_MODEL = 1024
N_HEADS = 8
QK_NOPE = 64
QK_ROPE = 32
V_DIM = 64
HEAD_PAD = 128
DEN_COL = V_DIM
ROPE_THETA = 10000.0
POOL_WINDOWS = (2, 4, 8, 16)
LN_EPS = 1e-5
RMS_EPS = 1e-6
DEPTH_FOR_DEEPNORM = 4
ALPHA = (2.0 * DEPTH_FOR_DEEPNORM) ** 0.25
ATTN_SCALE = (QK_NOPE + QK_ROPE) ** -0.5
LOG2_E = math.log2(math.e)
IN_COLS = 7584
IN_PAD_AT = 2976
IN_PAD = 96
IN_SPLITS = ((0, 512), (512, 1024), (1024, 1536), (1536, 2048), (2048, 2560), (2560, 2816), (2816, 2944),
             (2944, 2976), (3072, 3584), (3584, 7680))
ADAM_LR, ADAM_B1, ADAM_B2, ADAM_EPS, ADAM_WD, ADAM_STEP = 0.001, 0.9, 0.999, 1e-08, 0.01, 10

VMEM_LIMIT = 48 * 1024 * 1024
LANE = 128
CONV_HALO = 32
PACK_W = 512
MESH_T = pl.DeviceIdType.MESH

SHARDED = (
    ("w_ada", (1024, 6144), 1, True),
    ("w_in", (1024, IN_COLS), 1, True),
    ("conv_dw", (31, 512), 1, False),
    ("w_conv_out", (512, 1024), 1, True),
    ("sc_dw", (3, 512), 1, False),
    ("w_sc_out", (512, 1024), 1, True),
    ("w_uq", (256, 768), 1, True),
    ("w_ukv", (128, 1024), 1, True),
    ("w_mla_out", (512, 1024), 1, True),
    ("w_pool_out", (512, 1024), 1, True),
    ("w_o", (1024, 1024), 0, True),
    ("w_up", (1024, 5632), 1, True),
    ("ffn_dw", (3, 5632), 1, False),
    ("w_down", (2816, 1024), 0, True),
)
REPLICATED = ("b_ada", "b_in", "conv_ln_g", "conv_ln_b", "q_norm_g", "kv_norm_g", "w_pool", "pool_scale",
              "ln1_g", "ln1_b", "ln2_g", "ln2_b")
WEIGHT_ORDER = ("w_ada", "b_ada", "w_in", "b_in", "conv_dw", "conv_ln_g", "conv_ln_b", "w_conv_out", "sc_dw",
                "w_sc_out", "q_norm_g", "w_uq", "kv_norm_g", "w_ukv", "w_mla_out", "w_pool", "pool_scale",
                "w_pool_out", "w_o", "ln1_g", "ln1_b", "w_up", "ffn_dw", "w_down", "ln2_g", "ln2_b")


def _pick(dim, cap, quantum=LANE):
    best = None
    t = quantum
    while t <= min(dim, cap):
        if dim % t == 0:
            best = t
        t += quantum
    return dim if best is None else best


def _params(n_grid):
    return pltpu.CompilerParams(dimension_semantics=("arbitrary",) * n_grid, vmem_limit_bytes=VMEM_LIMIT)


def _mm_tile(dim):
    t = _pick(dim, 1024)
    return t if t >= 512 or t == dim else _pick(dim, 1536)


def _mm(a, b, *, ta=False, tb=False, bias=None, name):
    if ta:
        k_dim, m_dim = a.shape
    else:
        m_dim, k_dim = a.shape
    if tb:
        n_dim, k2 = b.shape
    else:
        k2, n_dim = b.shape
    assert k_dim == k2, (a.shape, b.shape, ta, tb)
    tm, tn, tk = _mm_tile(m_dim), _mm_tile(n_dim), _mm_tile(k_dim)
    nk = k_dim // tk
    a_spec = pl.BlockSpec((tk, tm), lambda i, j, k: (k, i)) if ta else pl.BlockSpec((tm, tk), lambda i, j, k: (i, k))
    b_spec = pl.BlockSpec((tn, tk), lambda i, j, k: (j, k)) if tb else pl.BlockSpec((tk, tn), lambda i, j, k: (k, j))
    dims = (((0 if ta else 1,), (1 if tb else 0,)), ((), ()))
    has_bias = bias is not None

    def body(*refs):
        a_ref, b_ref = refs[:2]
        bias_ref = refs[2] if has_bias else None
        o_ref = refs[3] if has_bias else refs[2]
        part = lax.dot_general(a_ref[...].astype(MXU), b_ref[...].astype(MXU), dims, preferred_element_type=F32)
        if nk == 1:
            o_ref[...] = part + bias_ref[...] if has_bias else part
            return
        acc_ref = refs[-1]
        k = pl.program_id(2)

        @pl.when(k == 0)
        def _():
            acc_ref[...] = part

        @pl.when(k > 0)
        def _():
            acc_ref[...] += part

        @pl.when(k == nk - 1)
        def _():
            out = acc_ref[...]
            if has_bias:
                out = out + bias_ref[...]
            o_ref[...] = out

    in_specs = [a_spec, b_spec]
    args = [a, b]
    if has_bias:
        in_specs.append(pl.BlockSpec((1, tn), lambda i, j, k: (0, j)))
        args.append(bias.reshape(1, n_dim))
    return pl.pallas_call(
        body, name=name, grid=(m_dim // tm, n_dim // tn, nk), in_specs=in_specs,
        out_specs=pl.BlockSpec((tm, tn), lambda i, j, k: (i, j)),
        out_shape=jax.ShapeDtypeStruct((m_dim, n_dim), F32),
        scratch_shapes=[] if nk == 1 else [pltpu.VMEM((tm, tn), F32)], compiler_params=_params(3),
    )(*args)


def _colsum(x, *, name):
    rows, n = x.shape
    tr = _pick(rows, 1024, 8)
    tn = _pick(n, 1024)

    def body(x_ref, o_ref):
        @pl.when(pl.program_id(1) == 0)
        def _():
            o_ref[...] = jnp.zeros_like(o_ref)

        o_ref[...] += jnp.sum(x_ref[...], axis=0, keepdims=True)

    out = pl.pallas_call(
        body, name=name, grid=(n // tn, rows // tr), in_specs=[pl.BlockSpec((tr, tn), lambda j, i: (i, j))],
        out_specs=pl.BlockSpec((1, tn), lambda j, i: (0, j)), out_shape=jax.ShapeDtypeStruct((1, n), F32),
        compiler_params=_params(2),
    )(x)
    return out[0]


@functools.partial(jax.custom_vjp, nondiff_argnums=(3,))
def linear(x, slot, w, name):
    return _mm(x, w, name=name + "_f")


def _linear_fwd(x, slot, w, name):
    return _mm(x, w, name=name + "_f"), (x, w)


def _linear_bwd(name, res, dy):
    x, w = res
    return _mm(dy, w, tb=True, name=name + "_dx"), _mm(x, dy, ta=True, name=name + "_dw"), jnp.zeros_like(w)


linear.defvjp(_linear_fwd, _linear_bwd)


@functools.partial(jax.custom_vjp, nondiff_argnums=(4,))
def linear_bias(x, slot, w, b, name):
    return _mm(x, w, bias=b, name=name + "_f")


def _linear_bias_fwd(x, slot, w, b, name):
    return _mm(x, w, bias=b, name=name + "_f"), (x, w)


def _linear_bias_bwd(name, res, dy):
    x, w = res
    return (_mm(dy, w, tb=True, name=name + "_dx"), _mm(x, dy, ta=True, name=name + "_dw"), jnp.zeros_like(w),
            _colsum(dy, name=name + "_db"))


linear_bias.defvjp(_linear_bias_fwd, _linear_bias_bwd)


@functools.partial(jax.custom_vjp, nondiff_argnums=(1,))
def split_cols(x, bounds):
    return tuple(x[:, a:b] for a, b in bounds)


def _split_cols_fwd(x, bounds):
    return split_cols(x, bounds), x.shape[1]


def _split_cols_bwd(bounds, width, cts):
    rows = cts[0].shape[0]
    parts, at = [], 0
    for (a, b), ct in zip(bounds, cts):
        if a > at:
            parts.append(jnp.zeros((rows, a - at), ct.dtype))
        parts.append(ct)
        at = b
    if at < width:
        parts.append(jnp.zeros((rows, width - at), cts[0].dtype))
    return (jnp.concatenate(parts, axis=1),)


split_cols.defvjp(_split_cols_fwd, _split_cols_bwd)


def _norm_stats(x, center, eps):
    if center:
        mu = jnp.mean(x, axis=-1, keepdims=True)
        xc = x - mu
    else:
        xc = x
    rstd = lax.rsqrt(jnp.mean(xc * xc, axis=-1, keepdims=True) + eps)
    return xc * rstd


def _norm_fwd_call(x, g, b, *, center, name):
    rows, d = x.shape
    tr = _pick(rows, 512, 8)
    eps = LN_EPS if center else RMS_EPS

    def body(x_ref, g_ref, b_ref, o_ref):
        xhat = _norm_stats(x_ref[...], center, eps)
        y = xhat * g_ref[...]
        if center:
            y = y + b_ref[...]
        o_ref[...] = y

    vec = pl.BlockSpec((1, d), lambda i: (0, 0))
    return pl.pallas_call(
        body, name=name, grid=(rows // tr,), in_specs=[pl.BlockSpec((tr, d), lambda i: (i, 0)), vec, vec],
        out_specs=pl.BlockSpec((tr, d), lambda i: (i, 0)), out_shape=jax.ShapeDtypeStruct((rows, d), F32),
        compiler_params=_params(1),
    )(x, g.reshape(1, d), b.reshape(1, d))


def _norm_bwd_call(x, g, dy, *, center, name):
    rows, d = x.shape
    tr = _pick(rows, 512, 8)
    eps = LN_EPS if center else RMS_EPS

    def body(x_ref, g_ref, dy_ref, dx_ref, dg_ref, db_ref):
        @pl.when(pl.program_id(0) == 0)
        def _():
            dg_ref[...] = jnp.zeros_like(dg_ref)
            db_ref[...] = jnp.zeros_like(db_ref)

        x = x_ref[...]
        dy = dy_ref[...]
        if center:
            mu = jnp.mean(x, axis=-1, keepdims=True)
            xc = x - mu
        else:
            xc = x
        rstd = lax.rsqrt(jnp.mean(xc * xc, axis=-1, keepdims=True) + eps)
        xhat = xc * rstd
        dyg = dy * g_ref[...]
        proj = jnp.mean(dyg * xhat, axis=-1, keepdims=True)
        dx = dyg - xhat * proj
        if center:
            dx = dx - jnp.mean(dyg, axis=-1, keepdims=True)
        dx_ref[...] = dx * rstd
        dg_ref[...] += jnp.sum(dy * xhat, axis=0, keepdims=True)
        db_ref[...] += jnp.sum(dy, axis=0, keepdims=True)

    vec = pl.BlockSpec((1, d), lambda i: (0, 0))
    row = pl.BlockSpec((tr, d), lambda i: (i, 0))
    dx, dg, db = pl.pallas_call(
        body, name=name, grid=(rows // tr,), in_specs=[row, vec, row], out_specs=[row, vec, vec],
        out_shape=[jax.ShapeDtypeStruct((rows, d), F32), jax.ShapeDtypeStruct((1, d), F32),
                   jax.ShapeDtypeStruct((1, d), F32)],
        compiler_params=_params(1),
    )(x, g.reshape(1, d), dy)
    return dx, dg[0], db[0]


@functools.partial(jax.custom_vjp, nondiff_argnums=(3,))
def layer_norm(x, g, b, name):
    return _norm_fwd_call(x, g, b, center=True, name=name + "_f")


def _layer_norm_fwd(x, g, b, name):
    return _norm_fwd_call(x, g, b, center=True, name=name + "_f"), (x, g)


def _layer_norm_bwd(name, res, dy):
    x, g = res
    return _norm_bwd_call(x, g, dy, center=True, name=name + "_b")


layer_norm.defvjp(_layer_norm_fwd, _layer_norm_bwd)


@functools.partial(jax.custom_vjp, nondiff_argnums=(2,))
def rms_norm(x, g, name):
    return _norm_fwd_call(x, g, jnp.zeros_like(g), center=False, name=name + "_f")


def _rms_norm_fwd(x, g, name):
    return _norm_fwd_call(x, g, jnp.zeros_like(g), center=False, name=name + "_f"), (x, g)


def _rms_norm_bwd(name, res, dy):
    x, g = res
    dx, dg, _ = _norm_bwd_call(x, g, dy, center=False, name=name + "_b")
    return dx, dg


rms_norm.defvjp(_rms_norm_fwd, _rms_norm_bwd)


def _conv_tiles(rows, ch):
    tr = _pick(rows, 512, CONV_HALO)
    assert tr >= CONV_HALO and rows % tr == 0
    return tr, _pick(ch, 512)


def _pad_taps(w):
    k = w.shape[0]
    kp = -(-k // 8) * 8
    return jnp.pad(w, ((0, kp - k), (0, 0))), k, kp


def _dwconv_fwd_call(x, w, *, name):
    rows, ch = x.shape
    tr, tc = _conv_tiles(rows, ch)
    wp, taps, kp = _pad_taps(w)
    assert taps - 1 <= CONV_HALO
    halo_per_tile = tr // CONV_HALO

    def body(x_ref, xprev_ref, w_ref, o_ref):
        i = pl.program_id(1)
        halo = xprev_ref[...]
        halo = jnp.where(i > 0, halo, jnp.zeros_like(halo))
        xx = jnp.concatenate([halo, x_ref[...]], axis=0)
        acc = jnp.zeros((tr, tc), F32)
        for k in range(taps):
            shift = taps - 1 - k
            term = xx if shift == 0 else pltpu.roll(xx, shift, 0)
            acc = acc + w_ref[k:k + 1, :] * term[CONV_HALO:, :]
        o_ref[...] = acc

    return pl.pallas_call(
        body, name=name, grid=(ch // tc, rows // tr),
        in_specs=[pl.BlockSpec((tr, tc), lambda j, i: (i, j)),
                  pl.BlockSpec((CONV_HALO, tc), lambda j, i: (jnp.maximum(i * halo_per_tile - 1, 0), j)),
                  pl.BlockSpec((kp, tc), lambda j, i: (0, j))],
        out_specs=pl.BlockSpec((tr, tc), lambda j, i: (i, j)), out_shape=jax.ShapeDtypeStruct((rows, ch), F32),
        compiler_params=_params(2),
    )(x, x, wp)


def _dwconv_bwd_call(x, w, dy, *, name):
    rows, ch = x.shape
    tr, tc = _conv_tiles(rows, ch)
    wp, taps, kp = _pad_taps(w)
    n_row_tiles = rows // tr
    halo_per_tile = tr // CONV_HALO
    n_halo_blocks = rows // CONV_HALO
    ext = tr + CONV_HALO

    def body(x_ref, xprev_ref, dy_ref, dynext_ref, w_ref, dx_ref, dw_ref):
        i = pl.program_id(1)

        @pl.when(i == 0)
        def _():
            dw_ref[...] = jnp.zeros_like(dw_ref)

        halo = xprev_ref[...]
        halo = jnp.where(i > 0, halo, jnp.zeros_like(halo))
        xx = jnp.concatenate([halo, x_ref[...]], axis=0)
        dy = dy_ref[...]
        ahead = dynext_ref[...]
        ahead = jnp.where(i < n_row_tiles - 1, ahead, jnp.zeros_like(ahead))
        yy = jnp.concatenate([dy, ahead], axis=0)
        dx = jnp.zeros((tr, tc), F32)
        for k in range(taps):
            shift = taps - 1 - k
            fwd = yy if shift == 0 else pltpu.roll(yy, ext - shift, 0)
            dx = dx + w_ref[k:k + 1, :] * fwd[:tr, :]
            back = xx if shift == 0 else pltpu.roll(xx, shift, 0)
            dw_ref[k:k + 1, :] += jnp.sum(dy * back[CONV_HALO:, :], axis=0, keepdims=True)
        dx_ref[...] = dx

    cur = pl.BlockSpec((tr, tc), lambda j, i: (i, j))
    dx, dw = pl.pallas_call(
        body, name=name, grid=(ch // tc, n_row_tiles),
        in_specs=[cur, pl.BlockSpec((CONV_HALO, tc), lambda j, i: (jnp.maximum(i * halo_per_tile - 1, 0), j)), cur,
                  pl.BlockSpec((CONV_HALO, tc),
                               lambda j, i: (jnp.minimum((i + 1) * halo_per_tile, n_halo_blocks - 1), j)),
                  pl.BlockSpec((kp, tc), lambda j, i: (0, j))],
        out_specs=[cur, pl.BlockSpec((kp, tc), lambda j, i: (0, j))],
        out_shape=[jax.ShapeDtypeStruct((rows, ch), F32), jax.ShapeDtypeStruct((kp, ch), F32)],
        compiler_params=_params(2),
    )(x, x, dy, dy, wp)
    return dx, dw[:taps]


@functools.partial(jax.custom_vjp, nondiff_argnums=(2,))
def dwconv(x, w, name):
    return _dwconv_fwd_call(x, w, name=name + "_f")


def _dwconv_fwd(x, w, name):
    return _dwconv_fwd_call(x, w, name=name + "_f"), (x, w)


def _dwconv_bwd(name, res, dy):
    x, w = res
    return _dwconv_bwd_call(x, w, dy, name=name + "_b")


dwconv.defvjp(_dwconv_fwd, _dwconv_bwd)


def _attn_tiles(seq):
    return _pick(seq, 1024), _pick(seq, 512)


def _lane_tile(v, width):
    return v if width == LANE else jnp.tile(v, (1, width // LANE))


def _causal_pairs(nq, nk, tq, tk, key_major):
    pairs = [(i, j) for i in range(nq) for j in range(nk) if j * tk <= i * tq + tq - 1]
    if key_major:
        pairs.sort(key=lambda p: (p[1], p[0]))
    return (jnp.asarray(np.array([p[0] for p in pairs], np.int32)),
            jnp.asarray(np.array([p[1] for p in pairs], np.int32)))


def _scores(q, k, i, j, tq, tk, masked):
    z = lax.dot_general(q, k, (((1,), (1,)), ((), ())), preferred_element_type=F32)
    if masked:
        row = i * tq + lax.broadcasted_iota(jnp.int32, (tq, tk), 0)
        col = j * tk + lax.broadcasted_iota(jnp.int32, (tq, tk), 1)
        z = jnp.where(col <= row, z, -jnp.inf)
    return z


def _attn_fwd_call(q, k, v, *, name):
    heads, seq, dh = q.shape
    tq, tk = _attn_tiles(seq)
    nq, nk = seq // tq, seq // tk
    qi, kj = _causal_pairs(nq, nk, tq, tk, key_major=False)

    def body(qi_ref, kj_ref, q_ref, k_ref, v_ref, o_ref, lse_ref, m_sc, acc_sc):
        t = pl.program_id(1)
        i = qi_ref[t]
        j = kj_ref[t]

        @pl.when(j == 0)
        def _():
            m_sc[...] = jnp.full(m_sc.shape, -jnp.inf, F32)
            acc_sc[...] = jnp.zeros_like(acc_sc)

        def step(masked):
            z = _scores(q_ref[0], k_ref[0], i, j, tq, tk, masked)
            m_prev = m_sc[...]
            m_new = jnp.maximum(m_prev, jnp.max(z, axis=1, keepdims=True))
            alpha = jnp.exp2(m_prev - m_new)
            p = jnp.exp2(z - _lane_tile(m_new, tk))
            acc_sc[...] = alpha * acc_sc[...] + jnp.dot(p.astype(MXU), v_ref[0], preferred_element_type=F32)
            m_sc[...] = m_new

        crosses_diagonal = j * tk + tk - 1 > i * tq

        @pl.when(jnp.logical_not(crosses_diagonal))
        def _():
            step(False)

        @pl.when(crosses_diagonal)
        def _():
            step(True)

        @pl.when(j == (i * tq + tq - 1) // tk)
        def _():
            acc = acc_sc[...]
            lane = lax.broadcasted_iota(jnp.int32, acc.shape, 1)
            den = jnp.sum(jnp.where(lane == DEN_COL, acc, 0.0), axis=1, keepdims=True)
            o_ref[0] = acc / den
            lse_ref[0] = m_sc[...] + jnp.log(den) * LOG2_E

    q_spec = pl.BlockSpec((1, tq, dh), lambda h, t, qi_ref, kj_ref: (h, qi_ref[t], 0))
    kv_spec = pl.BlockSpec((1, tk, dh), lambda h, t, qi_ref, kj_ref: (h, kj_ref[t], 0))
    return pl.pallas_call(
        body, name=name,
        grid_spec=pltpu.PrefetchScalarGridSpec(
            num_scalar_prefetch=2, grid=(heads, int(qi.shape[0])), in_specs=[q_spec, kv_spec, kv_spec],
            out_specs=[q_spec, q_spec],
            scratch_shapes=[pltpu.VMEM((tq, LANE), F32), pltpu.VMEM((tq, dh), F32)]),
        out_shape=[jax.ShapeDtypeStruct((heads, seq, dh), F32), jax.ShapeDtypeStruct((heads, seq, LANE), F32)],
        compiler_params=_params(2),
    )(qi, kj, q, k, v)


def _attn_bwd_call(q, k, v, o, lse, do, *, name):
    heads, seq, dh = q.shape
    tk, tq = _attn_tiles(seq)
    nq, nk = seq // tq, seq // tk
    qi, kj = _causal_pairs(nq, nk, tq, tk, key_major=True)

    def body(qi_ref, kj_ref, q_ref, k_ref, v_ref, o_ref, lse_ref, do_ref, dq_ref, dk_ref, dv_ref, dk_sc, dv_sc):
        t = pl.program_id(1)
        i = qi_ref[t]
        j = kj_ref[t]
        first_i = (j * tk) // tq

        @pl.when(t == 0)
        def _():
            dq_ref[...] = jnp.zeros_like(dq_ref)

        @pl.when(i == first_i)
        def _():
            dk_sc[...] = jnp.zeros_like(dk_sc)
            dv_sc[...] = jnp.zeros_like(dv_sc)

        def step(masked):
            qb = q_ref[0]
            kb = k_ref[0]
            do_f = do_ref[0]
            do_b = do_f.astype(MXU)
            p = jnp.exp2(_scores(qb, kb, i, j, tq, tk, masked) - _lane_tile(lse_ref[0], tk))
            dp = lax.dot_general(do_b, v_ref[0], (((1,), (1,)), ((), ())), preferred_element_type=F32)
            delta = jnp.sum(do_f * o_ref[0], axis=1, keepdims=True)
            ds = (p * (dp - delta)).astype(MXU)
            dv_sc[...] += lax.dot_general(p.astype(MXU), do_b, (((0,), (0,)), ((), ())), preferred_element_type=F32)
            dk_sc[...] += lax.dot_general(ds, qb, (((0,), (0,)), ((), ())), preferred_element_type=F32)
            rows = pl.ds(pl.multiple_of(i * tq, tq), tq)
            dq_ref[0, rows, :] += jnp.dot(ds, kb, preferred_element_type=F32) * ATTN_SCALE

        crosses_diagonal = j * tk + tk - 1 > i * tq

        @pl.when(jnp.logical_not(crosses_diagonal))
        def _():
            step(False)

        @pl.when(crosses_diagonal)
        def _():
            step(True)

        @pl.when(i == nq - 1)
        def _():
            dk_ref[0] = dk_sc[...] * (1.0 / LOG2_E)
            dv_ref[0] = dv_sc[...]

    def q_map(h, t, qi_ref, kj_ref):
        return (h, qi_ref[t], 0)

    def kv_map(h, t, qi_ref, kj_ref):
        return (h, kj_ref[t], 0)

    q_spec = pl.BlockSpec((1, tq, dh), q_map)
    lse_spec = pl.BlockSpec((1, tq, LANE), q_map)
    kv_spec = pl.BlockSpec((1, tk, dh), kv_map)
    head_spec = pl.BlockSpec((1, seq, dh), lambda h, t, qi_ref, kj_ref: (h, 0, 0))
    return pl.pallas_call(
        body, name=name,
        grid_spec=pltpu.PrefetchScalarGridSpec(
            num_scalar_prefetch=2, grid=(heads, int(qi.shape[0])),
            in_specs=[q_spec, kv_spec, kv_spec, q_spec, lse_spec, q_spec], out_specs=[head_spec, kv_spec, kv_spec],
            scratch_shapes=[pltpu.VMEM((tk, dh), F32), pltpu.VMEM((tk, dh), F32)]),
        out_shape=[jax.ShapeDtypeStruct((heads, seq, dh), F32)] * 3,
        compiler_params=_params(2),
    )(qi, kj, q, k, v, o, lse, do)


@functools.partial(jax.custom_vjp, nondiff_argnums=(3,))
def attention(q, k, v, name):
    return _attention_fwd(q, k, v, name)[0]


def _attention_fwd(q, k, v, name):
    qb, kb, vb = (q * (ATTN_SCALE * LOG2_E)).astype(MXU), k.astype(MXU), v.astype(MXU)
    o, lse = _attn_fwd_call(qb, kb, vb, name=name + "_f")
    return o, (qb, kb, vb, o, lse)


def _attention_bwd(name, res, do):
    qb, kb, vb, o, lse = res
    return tuple(_attn_bwd_call(qb, kb, vb, o, lse, do, name=name + "_b"))


attention.defvjp(_attention_fwd, _attention_bwd)


def _merge_tiles(rows):
    return _pick(rows, 128, 8)


def _gate_merge_fwd_call(gates, ys, *, name):
    rows, d = ys[0].shape
    nb = len(ys)
    tr = _merge_tiles(rows)

    def body(g_ref, *refs):
        o_ref = refs[nb]
        acc = jax.nn.sigmoid(g_ref[:, 0:d]) * refs[0][...]
        for b in range(1, nb):
            acc = acc + jax.nn.sigmoid(g_ref[:, b * d:(b + 1) * d]) * refs[b][...]
        o_ref[...] = acc

    wide = pl.BlockSpec((tr, nb * d), lambda i: (i, 0))
    row = pl.BlockSpec((tr, d), lambda i: (i, 0))
    return pl.pallas_call(
        body, name=name, grid=(rows // tr,), in_specs=[wide] + [row] * nb, out_specs=row,
        out_shape=jax.ShapeDtypeStruct((rows, d), F32), compiler_params=_params(1),
    )(gates, *ys)


def _gate_merge_bwd_call(gates, ys, dm, *, name):
    rows, d = ys[0].shape
    nb = len(ys)
    tr = _merge_tiles(rows)

    def body(g_ref, *refs):
        y_refs = refs[:nb]
        dm_ref = refs[nb]
        dg_ref = refs[nb + 1]
        dy_refs = refs[nb + 2:]
        dm = dm_ref[...]
        for b in range(nb):
            gt = jax.nn.sigmoid(g_ref[:, b * d:(b + 1) * d])
            dy_refs[b][...] = gt * dm
            dg_ref[:, b * d:(b + 1) * d] = dm * y_refs[b][...] * gt * (1.0 - gt)

    wide = pl.BlockSpec((tr, nb * d), lambda i: (i, 0))
    row = pl.BlockSpec((tr, d), lambda i: (i, 0))
    outs = pl.pallas_call(
        body, name=name, grid=(rows // tr,), in_specs=[wide] + [row] * (nb + 1), out_specs=[wide] + [row] * nb,
        out_shape=[jax.ShapeDtypeStruct((rows, nb * d), F32)] + [jax.ShapeDtypeStruct((rows, d), F32)] * nb,
        compiler_params=_params(1),
    )(gates, *ys, dm)
    return outs[0], tuple(outs[1:])


@functools.partial(jax.custom_vjp, nondiff_argnums=(2,))
def gate_merge(gates, ys, name):
    return _gate_merge_fwd_call(gates, ys, name=name + "_f")


def _gate_merge_fwd(gates, ys, name):
    return _gate_merge_fwd_call(gates, ys, name=name + "_f"), (gates, ys)


def _gate_merge_bwd(name, res, dm):
    gates, ys = res
    return _gate_merge_bwd_call(gates, ys, dm, name=name + "_b")


gate_merge.defvjp(_gate_merge_fwd, _gate_merge_bwd)


def _loss_call(y, target):
    rows, d = y.shape
    tr = _pick(rows, 512, 8)

    def body(y_ref, t_ref, dy_ref, loss_ref):
        @pl.when(pl.program_id(0) == 0)
        def _():
            loss_ref[...] = jnp.zeros_like(loss_ref)

        err = y_ref[...] - t_ref[...]
        dy_ref[...] = err * (1.0 / d)
        loss_ref[...] += jnp.sum(err * err, axis=0, keepdims=True)

    row = pl.BlockSpec((tr, d), lambda i: (i, 0))
    vec = pl.BlockSpec((1, d), lambda i: (0, 0))
    dy, part = pl.pallas_call(
        body, name="loss_head", grid=(rows // tr,), in_specs=[row, row], out_specs=[row, vec],
        out_shape=[jax.ShapeDtypeStruct((rows, d), F32), jax.ShapeDtypeStruct((1, d), F32)],
        compiler_params=_params(1),
    )(y, target)
    return 0.5 * jnp.sum(part) / d, dy


ADAMW_BLOCK_ELEMS = 256 * 1024


def _adamw_call(w, g, m, v, *, name):
    shape = w.shape
    width = shape[-1]
    rows = w.size // width
    w, g, m, v = (a.reshape(rows, width) for a in (w, g, m, v))
    tr = _pick(rows, max(8, ADAMW_BLOCK_ELEMS // width), 8)
    c1 = 1.0 - ADAM_B1 ** ADAM_STEP
    c2 = 1.0 - ADAM_B2 ** ADAM_STEP

    def body(w_ref, g_ref, m_ref, v_ref, d_ref, nm_ref, nv_ref):
        g = g_ref[...]
        m = ADAM_B1 * m_ref[...] + (1.0 - ADAM_B1) * g
        v = ADAM_B2 * v_ref[...] + (1.0 - ADAM_B2) * (g * g)
        m_hat = m / c1
        v_hat = v / c2
        d_ref[...] = -ADAM_LR * (m_hat / (jnp.sqrt(v_hat) + ADAM_EPS) + ADAM_WD * w_ref[...])
        nm_ref[...] = m
        nv_ref[...] = v

    row = pl.BlockSpec((tr, width), lambda i: (i, 0))
    outs = pl.pallas_call(
        body, name=name, grid=(rows // tr,), in_specs=[row] * 4, out_specs=[row] * 3,
        out_shape=[jax.ShapeDtypeStruct((rows, width), F32)] * 3, compiler_params=_params(1),
    )(w, g, m, v)
    return [o.reshape(shape) for o in outs]


def _ordered_sum(y, *, name):
    n, rows, width = y.shape
    tr = _pick(rows, 256, 8)

    def body(y_ref, o_ref):
        acc = y_ref[0]
        for s in range(1, n):
            acc = acc + y_ref[s]
        o_ref[...] = acc

    return pl.pallas_call(
        body, name=name, grid=(rows // tr,), in_specs=[pl.BlockSpec((n, tr, width), lambda i: (0, i, 0))],
        out_specs=pl.BlockSpec((tr, width), lambda i: (i, 0)), out_shape=jax.ShapeDtypeStruct((rows, width), y.dtype),
        compiler_params=_params(1),
    )(y)


_GROUP_FLIPS = {
    "chips": ((1, 0, 0), (0, 1, 0), (1, 1, 0)),
    "cores": ((0, 0, 1),),
    "all": ((0, 0, 1), (0, 1, 0), (0, 1, 1), (1, 0, 0), (1, 0, 1), (1, 1, 0), (1, 1, 1)),
}


def _exchange(x, *, group, broadcast, name):
    flips = _GROUP_FLIPS[group]
    n = len(flips) + 1
    block = x.shape if broadcast else x.shape[1:]
    if not broadcast:
        assert x.shape[0] == n

    def body(x_ref, o_ref, send_sems, recv_sems, local_sem):
        mx, my, mc = lax.axis_index("x"), lax.axis_index("y"), lax.axis_index("c")

        def index(px, py, pc):
            return {"chips": 2 * px + py, "cores": pc, "all": 4 * px + 2 * py + pc}[group]

        def block_for(d):
            return x_ref if broadcast else x_ref.at[d]

        me = index(mx, my, mc)
        mine = pltpu.make_async_copy(block_for(me), o_ref.at[me], local_sem)
        mine.start()
        sends, recvs = [], []
        for k, (fx, fy, fc) in enumerate(flips):
            px = 1 - mx if fx else mx
            py = 1 - my if fy else my
            pc = 1 - mc if fc else mc
            peer = index(px, py, pc)
            sends.append(pltpu.make_async_remote_copy(
                src_ref=block_for(peer), dst_ref=o_ref.at[me], send_sem=send_sems.at[k], recv_sem=recv_sems.at[k],
                device_id=(px, py, pc), device_id_type=MESH_T))
            recvs.append(pltpu.make_async_remote_copy(
                src_ref=block_for(peer), dst_ref=o_ref.at[peer], send_sem=send_sems.at[k], recv_sem=recv_sems.at[k],
                device_id=(px, py, pc), device_id_type=MESH_T))
        for cp in sends:
            cp.start()
        for cp in recvs:
            cp.wait_recv()
        for cp in sends:
            cp.wait_send()
        mine.wait()

    any_spec = pl.BlockSpec(memory_space=pl.ANY)
    return pl.pallas_call(
        body, name=name, in_specs=[any_spec], out_specs=any_spec,
        out_shape=jax.ShapeDtypeStruct((n,) + tuple(block), x.dtype),
        scratch_shapes=[pltpu.SemaphoreType.DMA((n - 1,)), pltpu.SemaphoreType.DMA((n - 1,)), pltpu.SemaphoreType.DMA],
    )(x)


def _sibling():
    return (lax.axis_index("x"), lax.axis_index("y"), 1 - lax.axis_index("c"))


def _run_copies(copies):
    for cp in copies:
        cp.start()
    for cp in copies:
        cp.wait_recv()
    for cp in copies:
        cp.wait_send()


def _dma_call(body, arrays, out_shapes, n_copies, *, name, extra_scratch=()):
    any_spec = pl.BlockSpec(memory_space=pl.ANY)
    return pl.pallas_call(
        body, name=name, in_specs=[any_spec] * len(arrays), out_specs=[any_spec] * len(out_shapes),
        out_shape=out_shapes,
        scratch_shapes=[pltpu.SemaphoreType.DMA((n_copies,)), pltpu.SemaphoreType.DMA((n_copies,))]
        + list(extra_scratch),
    )(*arrays)


def _gather_shards(shards, *, name):
    flips = _GROUP_FLIPS["chips"]
    n = len(shards)
    n_copies = n * len(flips)

    def body(*refs):
        ins, outs = refs[:n], refs[n:2 * n]
        send_sems, recv_sems, pass_send_sems, pass_recv_sems, local_sems = refs[2 * n:]
        mx, my, mc = lax.axis_index("x"), lax.axis_index("y"), lax.axis_index("c")
        me = 2 * mx + my
        local = [pltpu.make_async_copy(ins[t], outs[t].at[me], local_sems.at[t]) for t in range(n)]
        for cp in local:
            cp.start()
        sends, arrivals, passes, passed = [], [], [], []
        for t in range(n):
            half = shards[t].shape[0] // 2
            mine = pl.ds(mc * half, half)
            theirs = pl.ds((1 - mc) * half, half)
            for k, (fx, fy, _) in enumerate(flips):
                px = 1 - mx if fx else mx
                py = 1 - my if fy else my
                peer = 2 * px + py
                idx = t * len(flips) + k
                ici = dict(send_sem=send_sems.at[idx], recv_sem=recv_sems.at[idx], device_id=(px, py, mc),
                           device_id_type=MESH_T)
                d2d = dict(send_sem=pass_send_sems.at[idx], recv_sem=pass_recv_sems.at[idx], device_id=_sibling(),
                           device_id_type=MESH_T)
                sends.append(pltpu.make_async_remote_copy(
                    src_ref=ins[t].at[mine], dst_ref=outs[t].at[me, mine], **ici))
                arrivals.append(pltpu.make_async_remote_copy(
                    src_ref=ins[t].at[mine], dst_ref=outs[t].at[peer, mine], **ici))
                passes.append(pltpu.make_async_remote_copy(
                    src_ref=outs[t].at[peer, mine], dst_ref=outs[t].at[peer, mine], **d2d))
                passed.append(pltpu.make_async_remote_copy(
                    src_ref=outs[t].at[peer, theirs], dst_ref=outs[t].at[peer, theirs], **d2d))
        for cp in sends:
            cp.start()
        for arrived, onward in zip(arrivals, passes):
            arrived.wait_recv()
            onward.start()
        for cp in passed:
            cp.wait_recv()
        for cp in sends + passes:
            cp.wait_send()
        for cp in local:
            cp.wait()

    return _dma_call(body, shards, [jax.ShapeDtypeStruct((4,) + a.shape, a.dtype) for a in shards], n_copies,
                     name=name, extra_scratch=[pltpu.SemaphoreType.DMA((n_copies,)), pltpu.SemaphoreType.DMA((n_copies,)),
                                               pltpu.SemaphoreType.DMA((n,))])


def _pair_send(gs, *, name):
    n = len(gs)

    def body(*refs):
        ins, outs = refs[:n], refs[n:2 * n]
        send_sems, recv_sems = refs[2 * n:]
        copies = []
        for t in range(n):
            half = gs[t].shape[1] // 2
            theirs = (1 - lax.axis_index("c")) * half
            for s in range(4):
                copies.append(pltpu.make_async_remote_copy(
                    src_ref=ins[t].at[s, pl.ds(theirs, half), :], dst_ref=outs[t].at[s],
                    send_sem=send_sems.at[4 * t + s], recv_sem=recv_sems.at[4 * t + s],
                    device_id=_sibling(), device_id_type=MESH_T))
        _run_copies(copies)

    return _dma_call(body, gs, [jax.ShapeDtypeStruct((4, a.shape[1] // 2, a.shape[2]), a.dtype) for a in gs], 4 * n,
                     name=name)


def _quad_send(ps, *, name):
    flips = _GROUP_FLIPS["chips"]
    n = len(ps)

    def body(*refs):
        ins, outs = refs[:n], refs[n:2 * n]
        send_sems, recv_sems = refs[2 * n:]
        mx, my, mc = lax.axis_index("x"), lax.axis_index("y"), lax.axis_index("c")
        copies = []
        for t in range(n):
            for k, (fx, fy, _) in enumerate(flips):
                px = 1 - mx if fx else mx
                py = 1 - my if fy else my
                copies.append(pltpu.make_async_remote_copy(
                    src_ref=ins[t].at[2 * px + py], dst_ref=outs[t].at[k], send_sem=send_sems.at[3 * t + k],
                    recv_sem=recv_sems.at[3 * t + k], device_id=(px, py, mc), device_id_type=MESH_T))
        _run_copies(copies)

    return _dma_call(body, ps, [jax.ShapeDtypeStruct((3,) + a.shape[1:], a.dtype) for a in ps], 3 * n, name=name)


def _sibling_send(ms, *, name):
    n = len(ms)

    def body(*refs):
        ins, outs = refs[:n], refs[n:2 * n]
        send_sems, recv_sems = refs[2 * n:]
        _run_copies([pltpu.make_async_remote_copy(
            src_ref=ins[t], dst_ref=outs[t], send_sem=send_sems.at[t], recv_sem=recv_sems.at[t],
            device_id=_sibling(), device_id_type=MESH_T) for t in range(n)])

    return _dma_call(body, ms, [jax.ShapeDtypeStruct(a.shape, a.dtype) for a in ms], n, name=name)


SUM_BLOCK_ELEMS = 256 * 1024


def _sum_tile(rows, width):
    return _pick(rows, max(16, SUM_BLOCK_ELEMS // width), 16)


def _pair_sum(g, got, core, *, name):
    _, rows, width = g.shape
    half = rows // 2
    tr = _sum_tile(half, width)
    tiles = half // tr

    def body(core_ref, g_ref, got_ref, o_ref):
        o_ref[...] = (g_ref[...] + got_ref[...]).astype(BF16)

    blk = pl.BlockSpec((1, tr, width), lambda s, i, core_ref: (s, i, 0))
    return pl.pallas_call(
        body, name=name,
        grid_spec=pltpu.PrefetchScalarGridSpec(
            num_scalar_prefetch=1, grid=(4, tiles),
            in_specs=[pl.BlockSpec((1, tr, width), lambda s, i, core_ref: (s, core_ref[0] * tiles + i, 0)), blk],
            out_specs=blk),
        out_shape=jax.ShapeDtypeStruct((4, half, width), BF16), compiler_params=_params(2),
    )(core, g, got)


def _quad_sum(pair, others, chip, *, name):
    _, rows, width = pair.shape
    tr = _sum_tile(rows, width)

    def body(chip_ref, p_ref, o3_ref, o_ref):
        acc = p_ref[0].astype(F32)
        for k in range(3):
            acc = acc + o3_ref[k].astype(F32)
        o_ref[...] = acc

    return pl.pallas_call(
        body, name=name,
        grid_spec=pltpu.PrefetchScalarGridSpec(
            num_scalar_prefetch=1, grid=(rows // tr,),
            in_specs=[pl.BlockSpec((1, tr, width), lambda i, chip_ref: (chip_ref[0], i, 0)),
                      pl.BlockSpec((3, tr, width), lambda i, chip_ref: (0, i, 0))],
            out_specs=pl.BlockSpec((tr, width), lambda i, chip_ref: (i, 0))),
        out_shape=jax.ShapeDtypeStruct((rows, width), F32), compiler_params=_params(1),
    )(chip, pair, others)


def _reduce_scatter(gs, *, tag):
    mc = lax.axis_index("c")
    core = mc.reshape(1).astype(jnp.int32)
    chip = (2 * lax.axis_index("x") + lax.axis_index("y")).reshape(1).astype(jnp.int32)
    got = _pair_send(gs, name=tag + "_pair")
    pairs = [_pair_sum(g, r, core, name=tag + "_pair_sum") for g, r in zip(gs, got)]
    others = _quad_send(pairs, name=tag + "_quad")
    mine = [_quad_sum(p, o, chip, name=tag + "_quad_sum") for p, o in zip(pairs, others)]
    theirs = _sibling_send(mine, name=tag + "_share")
    return [jnp.where(mc == 0, jnp.concatenate([m, t], axis=0), jnp.concatenate([t, m], axis=0))
            for m, t in zip(mine, theirs)]


def _round_up(n, q):
    return -(-n // q) * q


def _pack_rows(flat, quantum_rows):
    n = flat.shape[-1]
    total = _round_up(n, PACK_W * quantum_rows)
    pad = [(0, 0)] * (flat.ndim - 1) + [(0, total - n)]
    return jnp.pad(flat, pad).reshape(flat.shape[:-1] + (total // PACK_W, PACK_W))


def _split_shards(full, axis):
    if axis == 0:
        return full.reshape((4, full.shape[0] // 4, full.shape[1]))
    width = full.shape[1] // 4
    return jnp.stack([full[:, s * width:(s + 1) * width] for s in range(4)])


def _join_shards(blocks, axis):
    if axis == 0:
        return blocks.reshape((4 * blocks.shape[1], blocks.shape[2]))
    return jnp.concatenate([blocks[s] for s in range(4)], axis=1)


def _pack_small(tree, names):
    return _pack_rows(jnp.concatenate([tree[name].reshape(-1) for name in names]), 8)


def _unpack_small(packed, shapes, names):
    lead = packed.shape[:-2]
    flat = packed.reshape(lead + (-1,))
    out, off = {}, 0
    for name in names:
        n = int(np.prod(shapes[name]))
        out[name] = flat[..., off:off + n].reshape(lead + tuple(shapes[name]))
        off += n
    return out


def _rope(t, cos, sin):
    half = QK_ROPE // 2
    t1, t2 = t[..., :half], t[..., half:]
    return jnp.concatenate([t1 * cos - t2 * sin, t2 * cos + t1 * sin], axis=-1)


def _pad_in_cols(a):
    z = jnp.zeros(a.shape[:-1] + (IN_PAD,), a.dtype)
    return jnp.concatenate([a[..., :IN_PAD_AT], z, a[..., IN_PAD_AT:]], axis=-1)


def _pool_constants(seq):
    taps = max(POOL_WINDOWS)
    win = jnp.repeat(jnp.asarray(POOL_WINDOWS, jnp.int32), 512 // len(POOL_WINDOWS))
    lag = taps - 1 - jnp.arange(taps, dtype=jnp.int32)
    mask = (lag[:, None] < win[None, :]).astype(F32)
    cnt = jnp.minimum(jnp.arange(seq, dtype=jnp.int32)[:, None] + 1, win[None, :]).astype(F32)
    return mask, 1.0 / cnt


def _block_diag(w):
    g, n, _ = w.shape
    out = jnp.zeros((g * n, g * n), w.dtype)
    for i in range(g):
        out = lax.dynamic_update_slice(out, w[i], (i * n, i * n))
    return out


def _layer(x, c_act, p, wb, cos, sin, pool_mask, pool_inv_cnt, tag):
    seq = x.shape[0]
    d = D_MODEL
    mod = linear_bias(c_act, p["w_ada"], wb["w_ada"], p["b_ada"], tag + "ada")[0]
    sh1, sc1, g1, sh2, sc2, g2 = [mod[i * d:(i + 1) * d][None, :] for i in range(6)]

    h = x * (1.0 + sc1) + sh1
    proj = linear_bias(h, _pad_in_cols(p["w_in"]), _pad_in_cols(wb["w_in"]), _pad_in_cols(p["b_in"]),
                       tag + "in")
    conv_a, conv_b, sc_bg, sc_cg, sc_x, q_lat, kv_lat, k_rope, pool_u, gates = split_cols(proj, IN_SPLITS)

    ya = conv_a * jax.nn.sigmoid(conv_b)
    ya = dwconv(ya, p["conv_dw"], tag + "convA")
    ya = jax.nn.silu(layer_norm(ya, p["conv_ln_g"], p["conv_ln_b"], tag + "convA_ln"))
    ya = linear(ya, p["w_conv_out"], wb["w_conv_out"], tag + "convA_out")

    yb = linear(sc_bg * dwconv(sc_cg * sc_x, p["sc_dw"], tag + "sc"), p["w_sc_out"], wb["w_sc_out"],
                tag + "sc_out")

    q = linear(rms_norm(q_lat, p["q_norm_g"], tag + "q_rms"), p["w_uq"], wb["w_uq"], tag + "uq")
    q = q.reshape(seq, N_HEADS, QK_NOPE + QK_ROPE)
    kv = linear(rms_norm(kv_lat, p["kv_norm_g"], tag + "kv_rms"), p["w_ukv"], wb["w_ukv"], tag + "ukv")
    kv = kv.reshape(seq, N_HEADS, QK_NOPE + V_DIM)
    q_rope = _rope(q[..., QK_NOPE:], cos[:, None, :], sin[:, None, :])
    k_rope_r = jnp.broadcast_to(_rope(k_rope, cos, sin)[:, None, :], (seq, N_HEADS, QK_ROPE))
    zq = jnp.zeros((seq, N_HEADS, HEAD_PAD - QK_NOPE - QK_ROPE), F32)
    zv = jnp.zeros((seq, N_HEADS, HEAD_PAD - V_DIM), F32).at[:, :, DEN_COL - V_DIM].set(1.0)
    qh = jnp.concatenate([q[..., :QK_NOPE], q_rope, zq], axis=-1).transpose(1, 0, 2)
    kh = jnp.concatenate([kv[..., :QK_NOPE], k_rope_r, zq], axis=-1).transpose(1, 0, 2)
    vh = jnp.concatenate([kv[..., QK_NOPE:], zv], axis=-1).transpose(1, 0, 2)
    att = attention(qh, kh, vh, tag + "attn")
    att = att[:, :, :V_DIM].transpose(1, 0, 2).reshape(seq, N_HEADS * V_DIM)
    yc = linear(att, p["w_mla_out"], wb["w_mla_out"], tag + "mla_out")

    pd = dwconv(pool_u, pool_mask, tag + "pool") * pool_inv_cnt - pool_u
    w_mix = _block_diag(p["w_pool"])
    yd = linear(pd, w_mix, w_mix, tag + "pool_mix") * p["pool_scale"][None, :]
    yd = linear(yd, p["w_pool_out"], wb["w_pool_out"], tag + "pool_out")

    merged = gate_merge(gates, (ya, yb, yc, yd), tag + "merge")
    mix = linear(merged, p["w_o"], wb["w_o"], tag + "o")
    x = layer_norm(ALPHA * x + (1.0 + g1) * mix, p["ln1_g"], p["ln1_b"], tag + "ln1")

    h = x * (1.0 + sc2) + sh2
    up = dwconv(linear(h, p["w_up"], wb["w_up"], tag + "up"), p["ffn_dw"], tag + "ffn_conv")
    d_ff = up.shape[1] // 2
    val, gate = split_cols(up, ((0, d_ff), (d_ff, 2 * d_ff)))
    ffn = linear(jax.nn.silu(gate) * val, p["w_down"], wb["w_down"], tag + "down")
    return layer_norm(ALPHA * x + (1.0 + g2) * ffn, p["ln2_g"], p["ln2_b"], tag + "ln2")


def _forward(x, layers, gathered, c_act, cos, sin, pool_mask, pool_inv_cnt):
    for p, wb in zip(layers, gathered):
        x = _layer(x, c_act, p, wb, cos, sin, pool_mask, pool_inv_cnt, "")
    return x


def kernel(x, c, positions, w_ada, b_ada, w_in, b_in, conv_dw, conv_ln_g, conv_ln_b, w_conv_out, sc_dw, w_sc_out, q_norm_g, w_uq, kv_norm_g, w_ukv, w_mla_out, w_pool, pool_scale, w_pool_out, w_o, ln1_g, ln1_b, w_up, ffn_dw, w_down, ln2_g, ln2_b, loss_target, m_w_ada, m_b_ada, m_w_in, m_b_in, m_conv_dw, m_conv_ln_g, m_conv_ln_b, m_w_conv_out, m_sc_dw, m_w_sc_out, m_q_norm_g, m_w_uq, m_kv_norm_g, m_w_ukv, m_w_mla_out, m_w_pool, m_pool_scale, m_w_pool_out, m_w_o, m_ln1_g, m_ln1_b, m_w_up, m_ffn_dw, m_w_down, m_ln2_g, m_ln2_b, v_w_ada, v_b_ada, v_w_in, v_b_in, v_conv_dw, v_conv_ln_g, v_conv_ln_b, v_w_conv_out, v_sc_dw, v_w_sc_out, v_q_norm_g, v_w_uq, v_kv_norm_g, v_w_ukv, v_w_mla_out, v_w_pool, v_pool_scale, v_w_pool_out, v_w_o, v_ln1_g, v_ln1_b, v_w_up, v_ffn_dw, v_w_down, v_ln2_g, v_ln2_b):
    given = dict(locals())
    weights = {n: given[n] for n in WEIGHT_ORDER}
    mom_m = {n: given["m_" + n] for n in WEIGHT_ORDER}
    mom_v = {n: given["v_" + n] for n in WEIGHT_ORDER}
    depth = w_ada.shape[0]
    seq = x.shape[1]
    big = [(s[0], s[2]) for s in SHARDED if s[3]]
    tiny = [s[0] for s in SHARDED if not s[3]]
    chip = 2 * lax.axis_index("x") + lax.axis_index("y")

    tiny_shapes = {n: weights[n].shape for n in tiny}
    tiny_all = _exchange(_pack_small(weights, tiny), group="chips", broadcast=True, name="gather_taps")
    tiny_full = {n: jnp.concatenate([a[s] for s in range(4)], axis=-1)
                 for n, a in _unpack_small(tiny_all, tiny_shapes, tiny).items()}
    layers, gathered = [], []
    for l in range(depth):
        blocks = _gather_shards([weights[n][l].astype(BF16) for n, _ in big], name="gather_weights")
        gathered.append({n: _join_shards(a, axis) for (n, axis), a in zip(big, blocks)})
        p = {n: jnp.zeros(gathered[l][n].shape, F32) for n, _ in big}
        for n in tiny:
            p[n] = tiny_full[n][l]
        for n in REPLICATED:
            p[n] = weights[n][l]
        layers.append(p)

    inv = 1.0 / (ROPE_THETA ** (jnp.arange(0, QK_ROPE, 2, dtype=F32) / QK_ROPE))
    ang = positions[0].astype(F32)[:, None] * inv
    cos, sin = jnp.cos(ang), jnp.sin(ang)
    c_act = jnp.pad(jax.nn.silu(c), ((0, 15), (0, 0)))
    pool_mask, pool_inv_cnt = _pool_constants(seq)

    y, vjp_fn = jax.vjp(lambda xx, ll: _forward(xx, ll, gathered, c_act, cos, sin, pool_mask, pool_inv_cnt),
                        x[0], layers)
    loss_local, dy = _loss_call(y, loss_target[0])
    grad_x, grad_layers = vjp_fn(dy)
    loss = lax.psum(loss_local, ("x", "y", "c"))

    outs = {"grad": {}, "delta": {}, "m": {}, "v": {}}
    reduced = [_reduce_scatter([_split_shards(grad_layers[l][n], axis) for n, axis in big], tag="rs")
               for l in range(depth)]
    for t, (n, _) in enumerate(big):
        outs["grad"][n] = jnp.stack([reduced[l][t] for l in range(depth)])
    small = list(REPLICATED) + tiny
    small_local = {n: jnp.stack([grad_layers[l][n] for l in range(depth)]) for n in small}
    small_shapes = {n: small_local[n].shape for n in small}
    small_all = _exchange(_pack_small(small_local, small), group="all", broadcast=True, name="gather_small_grads")
    small_sum = _unpack_small(_ordered_sum(small_all, name="small_grads_sum"), small_shapes, small)
    for n in REPLICATED:
        outs["grad"][n] = small_sum[n]
    for n in tiny:
        width = weights[n].shape[-1]
        outs["grad"][n] = lax.dynamic_slice_in_dim(small_sum[n], chip * width, width, axis=2)

    for n in WEIGHT_ORDER:
        outs["delta"][n], outs["m"][n], outs["v"][n] = _adamw_call(
            weights[n], outs["grad"][n], mom_m[n], mom_v[n], name="adamw_" + n)

    result = [loss, grad_x[None]]
    for key in ("grad", "delta", "m", "v"):
        result.extend(outs[key][n] for n in WEIGHT_ORDER)
    return tuple(result)
```

```python
import functools
import math

import jax
import jax.numpy as jnp
import numpy as np
from jax import lax
from jax.experimental import pallas as pl
from jax.experimental.pallas import tpu as pltpu

F32 = jnp.float32
BF16 = jnp.bfloat16
MXU = jnp.bfloat16

D_MODEL = 1024
N_HEADS = 8
QK_NOPE = 64
QK_ROPE = 32
V_DIM = 64
HEAD_PAD = 128
DEN_COL = V_DIM
ROPE_THETA = 10000.0
POOL_WINDOWS = (2, 4, 8, 16)
LN_EPS = 1e-5
RMS_EPS = 1e-6
DEPTH_FOR_DEEPNORM = 4
ALPHA = (2.0 * DEPTH_FOR_DEEPNORM) ** 0.25
ATTN_SCALE = (QK_NOPE + QK_ROPE) ** -0.5
LOG2_E = math.log2(math.e)
IN_COLS = 7584
IN_PAD_AT = 2976
IN_PAD = 96
IN_SPLITS = ((0, 512), (512, 1024), (1024, 1536), (1536, 2048), (2048, 2560), (2560, 2816), (2816, 2944),
             (2944, 2976), (3072, 3584))
IN_GATES_AT = 3584
ADAM_LR, ADAM_B1, ADAM_B2, ADAM_EPS, ADAM_WD, ADAM_STEP = 0.001, 0.9, 0.999, 1e-08, 0.01, 10

VMEM_LIMIT = 48 * 1024 * 1024
LANE = 128
CONV_HALO = 32
PACK_W = 512
MESH_T = pl.DeviceIdType.MESH

SHARDED = (
    ("w_ada", (1024, 6144), 1, True),
    ("w_in", (1024, IN_COLS), 1, True),
    ("conv_dw", (31, 512), 1, False),
    ("w_conv_out", (512, 1024), 1, True),
    ("sc_dw", (3, 512), 1, False),
    ("w_sc_out", (512, 1024), 1, True),
    ("w_uq", (256, 768), 1, True),
    ("w_ukv", (128, 1024), 1, True),
    ("w_mla_out", (512, 1024), 1, True),
    ("w_pool_out", (512, 1024), 1, True),
    ("w_o", (1024, 1024), 0, True),
    ("w_up", (1024, 5632), 1, True),
    ("ffn_dw", (3, 5632), 1, False),
    ("w_down", (2816, 1024), 0, True),
)
REPLICATED = ("b_ada", "b_in", "conv_ln_g", "conv_ln_b", "q_norm_g", "kv_norm_g", "w_pool", "pool_scale",
              "ln1_g", "ln1_b", "ln2_g", "ln2_b")
WEIGHT_ORDER = ("w_ada", "b_ada", "w_in", "b_in", "conv_dw", "conv_ln_g", "conv_ln_b", "w_conv_out", "sc_dw",
                "w_sc_out", "q_norm_g", "w_uq", "kv_norm_g", "w_ukv", "w_mla_out", "w_pool", "pool_scale",
                "w_pool_out", "w_o", "ln1_g", "ln1_b", "w_up", "ffn_dw", "w_down", "ln2_g", "ln2_b")


def _pick(dim, cap, quantum=LANE):
    best = None
    t = quantum
    while t <= min(dim, cap):
        if dim % t == 0:
            best = t
        t += quantum
    return dim if best is None else best


def _params(n_grid):
    return pltpu.CompilerParams(dimension_semantics=("arbitrary",) * n_grid, vmem_limit_bytes=VMEM_LIMIT)


def _mm_tile(dim):
    t = _pick(dim, 1024)
    return t if t >= 512 or t == dim else _pick(dim, 1536)


def _mm(a, b, *, ta=False, tb=False, bias=None, name):
    if ta:
        k_dim, m_dim = a.shape
    else:
        m_dim, k_dim = a.shape
    if tb:
        n_dim, k2 = b.shape
    else:
        k2, n_dim = b.shape
    assert k_dim == k2, (a.shape, b.shape, ta, tb)
    tm, tn, tk = _mm_tile(m_dim), _mm_tile(n_dim), _mm_tile(k_dim)
    nk = k_dim // tk
    a_spec = pl.BlockSpec((tk, tm), lambda i, j, k: (k, i)) if ta else pl.BlockSpec((tm, tk), lambda i, j, k: (i, k))
    b_spec = pl.BlockSpec((tn, tk), lambda i, j, k: (j, k)) if tb else pl.BlockSpec((tk, tn), lambda i, j, k: (k, j))
    dims = (((0 if ta else 1,), (1 if tb else 0,)), ((), ()))
    has_bias = bias is not None

    def body(*refs):
        a_ref, b_ref = refs[:2]
        bias_ref = refs[2] if has_bias else None
        o_ref = refs[3] if has_bias else refs[2]
        part = lax.dot_general(a_ref[...].astype(MXU), b_ref[...].astype(MXU), dims, preferred_element_type=F32)
        if nk == 1:
            o_ref[...] = part + bias_ref[...] if has_bias else part
            return
        acc_ref = refs[-1]
        k = pl.program_id(2)

        @pl.when(k == 0)
        def _():
            acc_ref[...] = part

        @pl.when(k > 0)
        def _():
            acc_ref[...] += part

        @pl.when(k == nk - 1)
        def _():
            out = acc_ref[...]
            if has_bias:
                out = out + bias_ref[...]
            o_ref[...] = out

    in_specs = [a_spec, b_spec]
    args = [a, b]
    if has_bias:
        in_specs.append(pl.BlockSpec((1, tn), lambda i, j, k: (0, j)))
        args.append(bias.reshape(1, n_dim))
    return pl.pallas_call(
        body, name=name, grid=(m_dim // tm, n_dim // tn, nk), in_specs=in_specs,
        out_specs=pl.BlockSpec((tm, tn), lambda i, j, k: (i, j)),
        out_shape=jax.ShapeDtypeStruct((m_dim, n_dim), F32),
        scratch_shapes=[] if nk == 1 else [pltpu.VMEM((tm, tn), F32)], compiler_params=_params(3),
    )(*args)


COLSUM_BLOCK_ELEMS = 1024 * 1024


def _colsum(x, *, name):
    rows, n = x.shape
    tr = _pick(rows, max(8, COLSUM_BLOCK_ELEMS // n), 8)

    def body(x_ref, o_ref):
        @pl.when(pl.program_id(0) == 0)
        def _():
            o_ref[...] = jnp.zeros_like(o_ref)

        o_ref[...] += jnp.sum(x_ref[...], axis=0, keepdims=True)

    out = pl.pallas_call(
        body, name=name, grid=(rows // tr,), in_specs=[pl.BlockSpec((tr, n), lambda i: (i, 0))],
        out_specs=pl.BlockSpec((1, n), lambda i: (0, 0)), out_shape=jax.ShapeDtypeStruct((1, n), F32),
        compiler_params=_params(1),
    )(x)
    return out[0]


@functools.partial(jax.custom_vjp, nondiff_argnums=(3,))
def linear(x, slot, w, name):
    return _mm(x, w, name=name + "_f")


def _linear_fwd(x, slot, w, name):
    return _mm(x, w, name=name + "_f"), (x, w)


def _weight_grad(x, dy, name):
    return _mm(x, dy, ta=True, name=name)


def _linear_bwd(name, res, dy):
    x, w = res
    return _mm(dy, w, tb=True, name=name + "_dx"), _weight_grad(x, dy, name + "_dw"), jnp.zeros_like(w)


linear.defvjp(_linear_fwd, _linear_bwd)


@functools.partial(jax.custom_vjp, nondiff_argnums=(4,))
def linear_bias(x, slot, w, b, name):
    return _mm(x, w, bias=b, name=name + "_f")


def _linear_bias_fwd(x, slot, w, b, name):
    return _mm(x, w, bias=b, name=name + "_f"), (x, w)


def _linear_bias_bwd(name, res, dy):
    x, w = res
    return (_mm(dy, w, tb=True, name=name + "_dx"), _weight_grad(x, dy, name + "_dw"), jnp.zeros_like(w),
            _colsum(dy, name=name + "_db"))


linear_bias.defvjp(_linear_bias_fwd, _linear_bias_bwd)


@functools.partial(jax.custom_vjp, nondiff_argnums=(1,))
def split_cols(x, bounds):
    return tuple(x[:, a:b] for a, b in bounds)


def _split_cols_fwd(x, bounds):
    return split_cols(x, bounds), x.shape[1]


def _split_cols_bwd(bounds, width, cts):
    rows = cts[0].shape[0]
    parts, at = [], 0
    for (a, b), ct in zip(bounds, cts):
        if a > at:
            parts.append(jnp.zeros((rows, a - at), ct.dtype))
        parts.append(ct)
        at = b
    if at < width:
        parts.append(jnp.zeros((rows, width - at), cts[0].dtype))
    return (jnp.concatenate(parts, axis=1),)


split_cols.defvjp(_split_cols_fwd, _split_cols_bwd)


def _norm_stats(x, center, eps):
    if center:
        mu = jnp.mean(x, axis=-1, keepdims=True)
        xc = x - mu
    else:
        xc = x
    rstd = lax.rsqrt(jnp.mean(xc * xc, axis=-1, keepdims=True) + eps)
    return xc * rstd


def _norm_fwd_call(x, g, b, *, center, name):
    rows, d = x.shape
    tr = _pick(rows, 512, 8)
    eps = LN_EPS if center else RMS_EPS

    def body(x_ref, g_ref, b_ref, o_ref):
        xhat = _norm_stats(x_ref[...], center, eps)
        y = xhat * g_ref[...]
        if center:
            y = y + b_ref[...]
        o_ref[...] = y

    vec = pl.BlockSpec((1, d), lambda i: (0, 0))
    return pl.pallas_call(
        body, name=name, grid=(rows // tr,), in_specs=[pl.BlockSpec((tr, d), lambda i: (i, 0)), vec, vec],
        out_specs=pl.BlockSpec((tr, d), lambda i: (i, 0)), out_shape=jax.ShapeDtypeStruct((rows, d), F32),
        compiler_params=_params(1),
    )(x, g.reshape(1, d), b.reshape(1, d))


def _norm_bwd_call(x, g, dy, *, center, name):
    rows, d = x.shape
    tr = _pick(rows, 512, 8)
    eps = LN_EPS if center else RMS_EPS

    def body(x_ref, g_ref, dy_ref, dx_ref, dg_ref, db_ref):
        @pl.when(pl.program_id(0) == 0)
        def _():
            dg_ref[...] = jnp.zeros_like(dg_ref)
            db_ref[...] = jnp.zeros_like(db_ref)

        x = x_ref[...]
        dy = dy_ref[...]
        if center:
            mu = jnp.mean(x, axis=-1, keepdims=True)
            xc = x - mu
        else:
            xc = x
        rstd = lax.rsqrt(jnp.mean(xc * xc, axis=-1, keepdims=True) + eps)
        xhat = xc * rstd
        dyg = dy * g_ref[...]
        proj = jnp.mean(dyg * xhat, axis=-1, keepdims=True)
        dx = dyg - xhat * proj
        if center:
            dx = dx - jnp.mean(dyg, axis=-1, keepdims=True)
        dx_ref[...] = dx * rstd
        dg_ref[...] += jnp.sum(dy * xhat, axis=0, keepdims=True)
        db_ref[...] += jnp.sum(dy, axis=0, keepdims=True)

    vec = pl.BlockSpec((1, d), lambda i: (0, 0))
    row = pl.BlockSpec((tr, d), lambda i: (i, 0))
    dx, dg, db = pl.pallas_call(
        body, name=name, grid=(rows // tr,), in_specs=[row, vec, row], out_specs=[row, vec, vec],
        out_shape=[jax.ShapeDtypeStruct((rows, d), F32), jax.ShapeDtypeStruct((1, d), F32),
                   jax.ShapeDtypeStruct((1, d), F32)],
        compiler_params=_params(1),
    )(x, g.reshape(1, d), dy)
    return dx, dg[0], db[0]


@functools.partial(jax.custom_vjp, nondiff_argnums=(3,))
def layer_norm(x, g, b, name):
    return _norm_fwd_call(x, g, b, center=True, name=name + "_f")


def _layer_norm_fwd(x, g, b, name):
    return _norm_fwd_call(x, g, b, center=True, name=name + "_f"), (x, g)


def _layer_norm_bwd(name, res, dy):
    x, g = res
    return _norm_bwd_call(x, g, dy, center=True, name=name + "_b")


layer_norm.defvjp(_layer_norm_fwd, _layer_norm_bwd)


@functools.partial(jax.custom_vjp, nondiff_argnums=(2,))
def rms_norm(x, g, name):
    return _norm_fwd_call(x, g, jnp.zeros_like(g), center=False, name=name + "_f")


def _rms_norm_fwd(x, g, name):
    return _norm_fwd_call(x, g, jnp.zeros_like(g), center=False, name=name + "_f"), (x, g)


def _rms_norm_bwd(name, res, dy):
    x, g = res
    dx, dg, _ = _norm_bwd_call(x, g, dy, center=False, name=name + "_b")
    return dx, dg


rms_norm.defvjp(_rms_norm_fwd, _rms_norm_bwd)


def _conv_tiles(rows, ch):
    tr = _pick(rows, 512, CONV_HALO)
    assert tr >= CONV_HALO and rows % tr == 0
    return tr, _pick(ch, 512)


def _pad_taps(w):
    k = w.shape[0]
    kp = -(-k // 8) * 8
    return jnp.pad(w, ((0, kp - k), (0, 0))), k, kp


def _dwconv_fwd_call(x, w, *, name):
    rows, ch = x.shape
    tr, tc = _conv_tiles(rows, ch)
    wp, taps, kp = _pad_taps(w)
    assert taps - 1 <= CONV_HALO
    halo_per_tile = tr // CONV_HALO

    def body(x_ref, xprev_ref, w_ref, o_ref):
        i = pl.program_id(1)
        halo = xprev_ref[...]
        halo = jnp.where(i > 0, halo, jnp.zeros_like(halo))
        xx = jnp.concatenate([halo, x_ref[...]], axis=0)
        acc = jnp.zeros((tr, tc), F32)
        for k in range(taps):
            shift = taps - 1 - k
            term = xx if shift == 0 else pltpu.roll(xx, shift, 0)
            acc = acc + w_ref[k:k + 1, :] * term[CONV_HALO:, :]
        o_ref[...] = acc

    return pl.pallas_call(
        body, name=name, grid=(ch // tc, rows // tr),
        in_specs=[pl.BlockSpec((tr, tc), lambda j, i: (i, j)),
                  pl.BlockSpec((CONV_HALO, tc), lambda j, i: (jnp.maximum(i * halo_per_tile - 1, 0), j)),
                  pl.BlockSpec((kp, tc), lambda j, i: (0, j))],
        out_specs=pl.BlockSpec((tr, tc), lambda j, i: (i, j)), out_shape=jax.ShapeDtypeStruct((rows, ch), F32),
        compiler_params=_params(2),
    )(x, x, wp)


def _dwconv_bwd_call(x, w, dy, *, name):
    rows, ch = x.shape
    tr, tc = _conv_tiles(rows, ch)
    wp, taps, kp = _pad_taps(w)
    n_row_tiles = rows // tr
    halo_per_tile = tr // CONV_HALO
    n_halo_blocks = rows // CONV_HALO
    ext = tr + CONV_HALO

    def body(x_ref, xprev_ref, dy_ref, dynext_ref, w_ref, dx_ref, dw_ref):
        i = pl.program_id(1)

        @pl.when(i == 0)
        def _():
            dw_ref[...] = jnp.zeros_like(dw_ref)

        halo = xprev_ref[...]
        halo = jnp.where(i > 0, halo, jnp.zeros_like(halo))
        xx = jnp.concatenate([halo, x_ref[...]], axis=0)
        dy = dy_ref[...]
        ahead = dynext_ref[...]
        ahead = jnp.where(i < n_row_tiles - 1, ahead, jnp.zeros_like(ahead))
        yy = jnp.concatenate([dy, ahead], axis=0)
        dx = jnp.zeros((tr, tc), F32)
        for k in range(taps):
            shift = taps - 1 - k
            fwd = yy if shift == 0 else pltpu.roll(yy, ext - shift, 0)
            dx = dx + w_ref[k:k + 1, :] * fwd[:tr, :]
            back = xx if shift == 0 else pltpu.roll(xx, shift, 0)
            dw_ref[k:k + 1, :] += jnp.sum(dy * back[CONV_HALO:, :], axis=0, keepdims=True)
        dx_ref[...] = dx

    cur = pl.BlockSpec((tr, tc), lambda j, i: (i, j))
    dx, dw = pl.pallas_call(
        body, name=name, grid=(ch // tc, n_row_tiles),
        in_specs=[cur, pl.BlockSpec((CONV_HALO, tc), lambda j, i: (jnp.maximum(i * halo_per_tile - 1, 0), j)), cur,
                  pl.BlockSpec((CONV_HALO, tc),
                               lambda j, i: (jnp.minimum((i + 1) * halo_per_tile, n_halo_blocks - 1), j)),
                  pl.BlockSpec((kp, tc), lambda j, i: (0, j))],
        out_specs=[cur, pl.BlockSpec((kp, tc), lambda j, i: (0, j))],
        out_shape=[jax.ShapeDtypeStruct((rows, ch), F32), jax.ShapeDtypeStruct((kp, ch), F32)],
        compiler_params=_params(2),
    )(x, x, dy, dy, wp)
    return dx, dw[:taps]


@functools.partial(jax.custom_vjp, nondiff_argnums=(2,))
def dwconv(x, w, name):
    return _dwconv_fwd_call(x, w, name=name + "_f")


def _dwconv_fwd(x, w, name):
    return _dwconv_fwd_call(x, w, name=name + "_f"), (x, w)


def _dwconv_bwd(name, res, dy):
    x, w = res
    return _dwconv_bwd_call(x, w, dy, name=name + "_b")


dwconv.defvjp(_dwconv_fwd, _dwconv_bwd)


def _attn_tiles(seq):
    return _pick(seq, 1024), _pick(seq, 512)


def _lane_tile(v, width):
    return v if width == LANE else jnp.tile(v, (1, width // LANE))


def _causal_pairs(nq, nk, tq, tk, key_major):
    pairs = [(i, j) for i in range(nq) for j in range(nk) if j * tk <= i * tq + tq - 1]
    if key_major:
        pairs.sort(key=lambda p: (p[1], p[0]))
    return (jnp.asarray(np.array([p[0] for p in pairs], np.int32)),
            jnp.asarray(np.array([p[1] for p in pairs], np.int32)))


def _scores(q, k, i, j, tq, tk, masked):
    z = lax.dot_general(q, k, (((1,), (1,)), ((), ())), preferred_element_type=F32)
    if masked:
        row = i * tq + lax.broadcasted_iota(jnp.int32, (tq, tk), 0)
        col = j * tk + lax.broadcasted_iota(jnp.int32, (tq, tk), 1)
        z = jnp.where(col <= row, z, -jnp.inf)
    return z


def _attn_fwd_call(q, k, v, *, name):
    heads, seq, dh = q.shape
    tq, tk = _attn_tiles(seq)
    nq, nk = seq // tq, seq // tk
    qi, kj = _causal_pairs(nq, nk, tq, tk, key_major=False)

    def body(qi_ref, kj_ref, q_ref, k_ref, v_ref, o_ref, lse_ref, m_sc, acc_sc):
        t = pl.program_id(1)
        i = qi_ref[t]
        j = kj_ref[t]

        @pl.when(j == 0)
        def _():
            m_sc[...] = jnp.full(m_sc.shape, -jnp.inf, F32)
            acc_sc[...] = jnp.zeros_like(acc_sc)

        def step(masked):
            z = _scores(q_ref[0], k_ref[0], i, j, tq, tk, masked)
            m_prev = m_sc[...]
            m_new = jnp.maximum(m_prev, jnp.max(z, axis=1, keepdims=True))
            alpha = jnp.exp2(m_prev - m_new)
            p = jnp.exp2(z - _lane_tile(m_new, tk))
            acc_sc[...] = alpha * acc_sc[...] + jnp.dot(p.astype(MXU), v_ref[0], preferred_element_type=F32)
            m_sc[...] = m_new

        crosses_diagonal = j * tk + tk - 1 > i * tq

        @pl.when(jnp.logical_not(crosses_diagonal))
        def _():
            step(False)

        @pl.when(crosses_diagonal)
        def _():
            step(True)

        @pl.when(j == (i * tq + tq - 1) // tk)
        def _():
            acc = acc_sc[...]
            lane = lax.broadcasted_iota(jnp.int32, acc.shape, 1)
            den = jnp.sum(jnp.where(lane == DEN_COL, acc, 0.0), axis=1, keepdims=True)
            o_ref[0] = acc / den
            lse_ref[0] = m_sc[...] + jnp.log(den) * LOG2_E

    q_spec = pl.BlockSpec((1, tq, dh), lambda h, t, qi_ref, kj_ref: (h, qi_ref[t], 0))
    kv_spec = pl.BlockSpec((1, tk, dh), lambda h, t, qi_ref, kj_ref: (h, kj_ref[t], 0))
    return pl.pallas_call(
        body, name=name,
        grid_spec=pltpu.PrefetchScalarGridSpec(
            num_scalar_prefetch=2, grid=(heads, int(qi.shape[0])), in_specs=[q_spec, kv_spec, kv_spec],
            out_specs=[q_spec, q_spec],
            scratch_shapes=[pltpu.VMEM((tq, LANE), F32), pltpu.VMEM((tq, dh), F32)]),
        out_shape=[jax.ShapeDtypeStruct((heads, seq, dh), F32), jax.ShapeDtypeStruct((heads, seq, LANE), F32)],
        compiler_params=_params(2),
    )(qi, kj, q, k, v)


def _attn_bwd_call(q, k, v, o, lse, do, *, name):
    heads, seq, dh = q.shape
    tk, tq = _attn_tiles(seq)
    nq, nk = seq // tq, seq // tk
    qi, kj = _causal_pairs(nq, nk, tq, tk, key_major=True)

    def body(qi_ref, kj_ref, q_ref, k_ref, v_ref, o_ref, lse_ref, do_ref, dq_ref, dk_ref, dv_ref, dk_sc, dv_sc):
        t = pl.program_id(1)
        i = qi_ref[t]
        j = kj_ref[t]
        first_i = (j * tk) // tq

        @pl.when(t == 0)
        def _():
            dq_ref[...] = jnp.zeros_like(dq_ref)

        @pl.when(i == first_i)
        def _():
            dk_sc[...] = jnp.zeros_like(dk_sc)
            dv_sc[...] = jnp.zeros_like(dv_sc)

        def step(masked):
            qb = q_ref[0]
            kb = k_ref[0]
            do_f = do_ref[0]
            do_b = do_f.astype(MXU)
            p = jnp.exp2(_scores(qb, kb, i, j, tq, tk, masked) - _lane_tile(lse_ref[0], tk))
            dp = lax.dot_general(do_b, v_ref[0], (((1,), (1,)), ((), ())), preferred_element_type=F32)
            delta = jnp.sum(do_f * o_ref[0], axis=1, keepdims=True)
            ds = (p * (dp - delta)).astype(MXU)
            dv_sc[...] += lax.dot_general(p.astype(MXU), do_b, (((0,), (0,)), ((), ())), preferred_element_type=F32)
            dk_sc[...] += lax.dot_general(ds, qb, (((0,), (0,)), ((), ())), preferred_element_type=F32)
            rows = pl.ds(pl.multiple_of(i * tq, tq), tq)
            dq_ref[0, rows, :] += jnp.dot(ds, kb, preferred_element_type=F32) * ATTN_SCALE

        crosses_diagonal = j * tk + tk - 1 > i * tq

        @pl.when(jnp.logical_not(crosses_diagonal))
        def _():
            step(False)

        @pl.when(crosses_diagonal)
        def _():
            step(True)

        @pl.when(i == nq - 1)
        def _():
            dk_ref[0] = dk_sc[...] * (1.0 / LOG2_E)
            dv_ref[0] = dv_sc[...]

    def q_map(h, t, qi_ref, kj_ref):
        return (h, qi_ref[t], 0)

    def kv_map(h, t, qi_ref, kj_ref):
        return (h, kj_ref[t], 0)

    q_spec = pl.BlockSpec((1, tq, dh), q_map)
    lse_spec = pl.BlockSpec((1, tq, LANE), q_map)
    kv_spec = pl.BlockSpec((1, tk, dh), kv_map)
    head_spec = pl.BlockSpec((1, seq, dh), lambda h, t, qi_ref, kj_ref: (h, 0, 0))
    return pl.pallas_call(
        body, name=name,
        grid_spec=pltpu.PrefetchScalarGridSpec(
            num_scalar_prefetch=2, grid=(heads, int(qi.shape[0])),
            in_specs=[q_spec, kv_spec, kv_spec, q_spec, lse_spec, q_spec], out_specs=[head_spec, kv_spec, kv_spec],
            scratch_shapes=[pltpu.VMEM((tk, dh), F32), pltpu.VMEM((tk, dh), F32)]),
        out_shape=[jax.ShapeDtypeStruct((heads, seq, dh), F32)] * 3,
        compiler_params=_params(2),
    )(qi, kj, q, k, v, o, lse, do)


@functools.partial(jax.custom_vjp, nondiff_argnums=(3,))
def attention(q, k, v, name):
    return _attention_fwd(q, k, v, name)[0]


def _attention_fwd(q, k, v, name):
    qb, kb, vb = (q * (ATTN_SCALE * LOG2_E)).astype(MXU), k.astype(MXU), v.astype(MXU)
    o, lse = _attn_fwd_call(qb, kb, vb, name=name + "_f")
    return o, (qb, kb, vb, o, lse)


def _attention_bwd(name, res, do):
    qb, kb, vb, o, lse = res
    return tuple(_attn_bwd_call(qb, kb, vb, o, lse, do, name=name + "_b"))


attention.defvjp(_attention_fwd, _attention_bwd)


def _merge_tiles(rows):
    return _pick(rows, 128, 8)


def _gate_merge_fwd_call(gates, ys, *, name):
    rows, d = ys[0].shape
    nb = len(ys)
    tr = _merge_tiles(rows)

    def body(g_ref, *refs):
        o_ref = refs[nb]
        acc = jax.nn.sigmoid(g_ref[:, 0:d]) * refs[0][...]
        for b in range(1, nb):
            acc = acc + jax.nn.sigmoid(g_ref[:, b * d:(b + 1) * d]) * refs[b][...]
        o_ref[...] = acc

    wide = pl.BlockSpec((tr, nb * d), lambda i: (i, 0))
    row = pl.BlockSpec((tr, d), lambda i: (i, 0))
    return pl.pallas_call(
        body, name=name, grid=(rows // tr,), in_specs=[wide] + [row] * nb, out_specs=row,
        out_shape=jax.ShapeDtypeStruct((rows, d), F32), compiler_params=_params(1),
    )(gates, *ys)


def _gate_merge_bwd_call(gates, ys, dm, *, name):
    rows, d = ys[0].shape
    nb = len(ys)
    tr = _merge_tiles(rows)

    def body(g_ref, *refs):
        y_refs = refs[:nb]
        dm_ref = refs[nb]
        dg_ref = refs[nb + 1]
        dy_refs = refs[nb + 2:]
        dm = dm_ref[...]
        for b in range(nb):
            gt = jax.nn.sigmoid(g_ref[:, b * d:(b + 1) * d])
            dy_refs[b][...] = gt * dm
            dg_ref[:, b * d:(b + 1) * d] = dm * y_refs[b][...] * gt * (1.0 - gt)

    wide = pl.BlockSpec((tr, nb * d), lambda i: (i, 0))
    row = pl.BlockSpec((tr, d), lambda i: (i, 0))
    outs = pl.pallas_call(
        body, name=name, grid=(rows // tr,), in_specs=[wide] + [row] * (nb + 1), out_specs=[wide] + [row] * nb,
        out_shape=[jax.ShapeDtypeStruct((rows, nb * d), F32)] + [jax.ShapeDtypeStruct((rows, d), F32)] * nb,
        compiler_params=_params(1),
    )(gates, *ys, dm)
    return outs[0], tuple(outs[1:])


@functools.partial(jax.custom_vjp, nondiff_argnums=(2,))
def gate_merge(gates, ys, name):
    return _gate_merge_fwd_call(gates, ys, name=name + "_f")


def _gate_merge_fwd(gates, ys, name):
    return _gate_merge_fwd_call(gates, ys, name=name + "_f"), (gates, ys)


def _gate_merge_bwd(name, res, dm):
    gates, ys = res
    return _gate_merge_bwd_call(gates, ys, dm, name=name + "_b")


gate_merge.defvjp(_gate_merge_fwd, _gate_merge_bwd)


def _loss_call(y, target):
    rows, d = y.shape
    tr = _pick(rows, 512, 8)

    def body(y_ref, t_ref, dy_ref, loss_ref):
        @pl.when(pl.program_id(0) == 0)
        def _():
            loss_ref[...] = jnp.zeros_like(loss_ref)

        err = y_ref[...] - t_ref[...]
        dy_ref[...] = err * (1.0 / d)
        loss_ref[...] += jnp.sum(err * err, axis=0, keepdims=True)

    row = pl.BlockSpec((tr, d), lambda i: (i, 0))
    vec = pl.BlockSpec((1, d), lambda i: (0, 0))
    dy, part = pl.pallas_call(
        body, name="loss_head", grid=(rows // tr,), in_specs=[row, row], out_specs=[row, vec],
        out_shape=[jax.ShapeDtypeStruct((rows, d), F32), jax.ShapeDtypeStruct((1, d), F32)],
        compiler_params=_params(1),
    )(y, target)
    return 0.5 * jnp.sum(part) / d, dy


ADAMW_BLOCK_ELEMS = 256 * 1024


def _adamw_call(w, g, m, v, *, name):
    shape = w.shape
    width = shape[-1]
    rows = w.size // width
    w, g, m, v = (a.reshape(rows, width) for a in (w, g, m, v))
    tr = _pick(rows, max(8, ADAMW_BLOCK_ELEMS // width), 8)
    c1 = 1.0 - ADAM_B1 ** ADAM_STEP
    c2 = 1.0 - ADAM_B2 ** ADAM_STEP

    def body(w_ref, g_ref, m_ref, v_ref, d_ref, nm_ref, nv_ref):
        g = g_ref[...]
        m = ADAM_B1 * m_ref[...] + (1.0 - ADAM_B1) * g
        v = ADAM_B2 * v_ref[...] + (1.0 - ADAM_B2) * (g * g)
        m_hat = m / c1
        v_hat = v / c2
        d_ref[...] = -ADAM_LR * (m_hat / (jnp.sqrt(v_hat) + ADAM_EPS) + ADAM_WD * w_ref[...])
        nm_ref[...] = m
        nv_ref[...] = v

    row = pl.BlockSpec((tr, width), lambda i: (i, 0))
    outs = pl.pallas_call(
        body, name=name, grid=(rows // tr,), in_specs=[row] * 4, out_specs=[row] * 3,
        out_shape=[jax.ShapeDtypeStruct((rows, width), F32)] * 3, compiler_params=_params(1),
    )(w, g, m, v)
    return [o.reshape(shape) for o in outs]


def _ordered_sum(y, *, name):
    n, rows, width = y.shape
    tr = _pick(rows, 256, 8)

    def body(y_ref, o_ref):
        acc = y_ref[0]
        for s in range(1, n):
            acc = acc + y_ref[s]
        o_ref[...] = acc

    return pl.pallas_call(
        body, name=name, grid=(rows // tr,), in_specs=[pl.BlockSpec((n, tr, width), lambda i: (0, i, 0))],
        out_specs=pl.BlockSpec((tr, width), lambda i: (i, 0)), out_shape=jax.ShapeDtypeStruct((rows, width), y.dtype),
        compiler_params=_params(1),
    )(y)


_GROUP_FLIPS = {
    "chips": ((1, 0, 0), (0, 1, 0), (1, 1, 0)),
    "cores": ((0, 0, 1),),
    "all": ((0, 0, 1), (0, 1, 0), (0, 1, 1), (1, 0, 0), (1, 0, 1), (1, 1, 0), (1, 1, 1)),
}


def _exchange(x, *, group, broadcast, name):
    flips = _GROUP_FLIPS[group]
    n = len(flips) + 1
    block = x.shape if broadcast else x.shape[1:]
    if not broadcast:
        assert x.shape[0] == n

    def body(x_ref, o_ref, send_sems, recv_sems, local_sem):
        mx, my, mc = lax.axis_index("x"), lax.axis_index("y"), lax.axis_index("c")

        def index(px, py, pc):
            return {"chips": 2 * px + py, "cores": pc, "all": 4 * px + 2 * py + pc}[group]

        def block_for(d):
            return x_ref if broadcast else x_ref.at[d]

        me = index(mx, my, mc)
        mine = pltpu.make_async_copy(block_for(me), o_ref.at[me], local_sem)
        mine.start()
        sends, recvs = [], []
        for k, (fx, fy, fc) in enumerate(flips):
            px = 1 - mx if fx else mx
            py = 1 - my if fy else my
            pc = 1 - mc if fc else mc
            peer = index(px, py, pc)
            sends.append(pltpu.make_async_remote_copy(
                src_ref=block_for(peer), dst_ref=o_ref.at[me], send_sem=send_sems.at[k], recv_sem=recv_sems.at[k],
                device_id=(px, py, pc), device_id_type=MESH_T))
            recvs.append(pltpu.make_async_remote_copy(
                src_ref=block_for(peer), dst_ref=o_ref.at[peer], send_sem=send_sems.at[k], recv_sem=recv_sems.at[k],
                device_id=(px, py, pc), device_id_type=MESH_T))
        for cp in sends:
            cp.start()
        for cp in recvs:
            cp.wait_recv()
        for cp in sends:
            cp.wait_send()
        mine.wait()

    any_spec = pl.BlockSpec(memory_space=pl.ANY)
    return pl.pallas_call(
        body, name=name, in_specs=[any_spec], out_specs=any_spec,
        out_shape=jax.ShapeDtypeStruct((n,) + tuple(block), x.dtype),
        scratch_shapes=[pltpu.SemaphoreType.DMA((n - 1,)), pltpu.SemaphoreType.DMA((n - 1,)), pltpu.SemaphoreType.DMA],
    )(x)


def _sibling():
    return (lax.axis_index("x"), lax.axis_index("y"), 1 - lax.axis_index("c"))


def _run_copies(copies):
    for cp in copies:
        cp.start()
    for cp in copies:
        cp.wait_recv()
    for cp in copies:
        cp.wait_send()


def _dma_call(body, arrays, out_shapes, n_copies, *, name, extra_scratch=()):
    any_spec = pl.BlockSpec(memory_space=pl.ANY)
    return pl.pallas_call(
        body, name=name, in_specs=[any_spec] * len(arrays), out_specs=[any_spec] * len(out_shapes),
        out_shape=out_shapes,
        scratch_shapes=[pltpu.SemaphoreType.DMA((n_copies,)), pltpu.SemaphoreType.DMA((n_copies,))]
        + list(extra_scratch),
    )(*arrays)


def _gather_shards(shards, *, name):
    flips = _GROUP_FLIPS["chips"]
    n = len(shards)
    n_copies = n * len(flips)

    def body(*refs):
        ins, outs = refs[:n], refs[n:2 * n]
        send_sems, recv_sems, pass_send_sems, pass_recv_sems, local_sems = refs[2 * n:]
        mx, my, mc = lax.axis_index("x"), lax.axis_index("y"), lax.axis_index("c")
        me = 2 * mx + my
        local = [pltpu.make_async_copy(ins[t], outs[t].at[me], local_sems.at[t]) for t in range(n)]
        for cp in local:
            cp.start()
        sends, arrivals, passes, passed = [], [], [], []
        for t in range(n):
            half = shards[t].shape[0] // 2
            mine = pl.ds(mc * half, half)
            theirs = pl.ds((1 - mc) * half, half)
            for k, (fx, fy, _) in enumerate(flips):
                px = 1 - mx if fx else mx
                py = 1 - my if fy else my
                peer = 2 * px + py
                idx = t * len(flips) + k
                ici = dict(send_sem=send_sems.at[idx], recv_sem=recv_sems.at[idx], device_id=(px, py, mc),
                           device_id_type=MESH_T)
                d2d = dict(send_sem=pass_send_sems.at[idx], recv_sem=pass_recv_sems.at[idx], device_id=_sibling(),
                           device_id_type=MESH_T)
                sends.append(pltpu.make_async_remote_copy(
                    src_ref=ins[t].at[mine], dst_ref=outs[t].at[me, mine], **ici))
                arrivals.append(pltpu.make_async_remote_copy(
                    src_ref=ins[t].at[mine], dst_ref=outs[t].at[peer, mine], **ici))
                passes.append(pltpu.make_async_remote_copy(
                    src_ref=outs[t].at[peer, mine], dst_ref=outs[t].at[peer, mine], **d2d))
                passed.append(pltpu.make_async_remote_copy(
                    src_ref=outs[t].at[peer, theirs], dst_ref=outs[t].at[peer, theirs], **d2d))
        for cp in sends:
            cp.start()
        for arrived, onward in zip(arrivals, passes):
            arrived.wait_recv()
            onward.start()
        for cp in passed:
            cp.wait_recv()
        for cp in sends + passes:
            cp.wait_send()
        for cp in local:
            cp.wait()

    return _dma_call(body, shards, [jax.ShapeDtypeStruct((4,) + a.shape, a.dtype) for a in shards], n_copies,
                     name=name, extra_scratch=[pltpu.SemaphoreType.DMA((n_copies,)), pltpu.SemaphoreType.DMA((n_copies,)),
                                               pltpu.SemaphoreType.DMA((n,))])


def _pair_send(gs, *, name):
    n = len(gs)

    def body(*refs):
        ins, outs = refs[:n], refs[n:2 * n]
        send_sems, recv_sems = refs[2 * n:]
        copies = []
        for t in range(n):
            half = gs[t].shape[1] // 2
            theirs = (1 - lax.axis_index("c")) * half
            for s in range(4):
                copies.append(pltpu.make_async_remote_copy(
                    src_ref=ins[t].at[s, pl.ds(theirs, half), :], dst_ref=outs[t].at[s],
                    send_sem=send_sems.at[4 * t + s], recv_sem=recv_sems.at[4 * t + s],
                    device_id=_sibling(), device_id_type=MESH_T))
        _run_copies(copies)

    return _dma_call(body, gs, [jax.ShapeDtypeStruct((4, a.shape[1] // 2, a.shape[2]), a.dtype) for a in gs], 4 * n,
                     name=name)


def _quad_send(ps, *, name):
    flips = _GROUP_FLIPS["chips"]
    n = len(ps)

    def body(*refs):
        ins, outs = refs[:n], refs[n:2 * n]
        send_sems, recv_sems = refs[2 * n:]
        mx, my, mc = lax.axis_index("x"), lax.axis_index("y"), lax.axis_index("c")
        copies = []
        for t in range(n):
            for k, (fx, fy, _) in enumerate(flips):
                px = 1 - mx if fx else mx
                py = 1 - my if fy else my
                copies.append(pltpu.make_async_remote_copy(
                    src_ref=ins[t].at[2 * px + py], dst_ref=outs[t].at[k], send_sem=send_sems.at[3 * t + k],
                    recv_sem=recv_sems.at[3 * t + k], device_id=(px, py, mc), device_id_type=MESH_T))
        _run_copies(copies)

    return _dma_call(body, ps, [jax.ShapeDtypeStruct((3,) + a.shape[1:], a.dtype) for a in ps], 3 * n, name=name)


def _sibling_send(ms, *, name):
    n = len(ms)

    def body(*refs):
        ins, outs = refs[:n], refs[n:2 * n]
        send_sems, recv_sems = refs[2 * n:]
        _run_copies([pltpu.make_async_remote_copy(
            src_ref=ins[t], dst_ref=outs[t], send_sem=send_sems.at[t], recv_sem=recv_sems.at[t],
            device_id=_sibling(), device_id_type=MESH_T) for t in range(n)])

    return _dma_call(body, ms, [jax.ShapeDtypeStruct(a.shape, a.dtype) for a in ms], n, name=name)


SUM_BLOCK_ELEMS = 256 * 1024


def _sum_tile(rows, width):
    return _pick(rows, max(16, SUM_BLOCK_ELEMS // width), 16)


def _pair_sum(g, got, core, *, name):
    _, rows, width = g.shape
    half = rows // 2
    tr = _sum_tile(half, width)
    tiles = half // tr

    def body(core_ref, g_ref, got_ref, o_ref):
        o_ref[...] = (g_ref[...] + got_ref[...]).astype(BF16)

    blk = pl.BlockSpec((1, tr, width), lambda s, i, core_ref: (s, i, 0))
    return pl.pallas_call(
        body, name=name,
        grid_spec=pltpu.PrefetchScalarGridSpec(
            num_scalar_prefetch=1, grid=(4, tiles),
            in_specs=[pl.BlockSpec((1, tr, width), lambda s, i, core_ref: (s, core_ref[0] * tiles + i, 0)), blk],
            out_specs=blk),
        out_shape=jax.ShapeDtypeStruct((4, half, width), BF16), compiler_params=_params(2),
    )(core, g, got)


def _quad_sum(pair, others, chip, *, name):
    _, rows, width = pair.shape
    tr = _sum_tile(rows, width)

    def body(chip_ref, p_ref, o3_ref, o_ref):
        acc = p_ref[0].astype(F32)
        for k in range(3):
            acc = acc + o3_ref[k].astype(F32)
        o_ref[...] = acc

    return pl.pallas_call(
        body, name=name,
        grid_spec=pltpu.PrefetchScalarGridSpec(
            num_scalar_prefetch=1, grid=(rows // tr,),
            in_specs=[pl.BlockSpec((1, tr, width), lambda i, chip_ref: (chip_ref[0], i, 0)),
                      pl.BlockSpec((3, tr, width), lambda i, chip_ref: (0, i, 0))],
            out_specs=pl.BlockSpec((tr, width), lambda i, chip_ref: (i, 0))),
        out_shape=jax.ShapeDtypeStruct((rows, width), F32), compiler_params=_params(1),
    )(chip, pair, others)


def _reduce_scatter(gs, *, tag):
    mc = lax.axis_index("c")
    core = mc.reshape(1).astype(jnp.int32)
    chip = (2 * lax.axis_index("x") + lax.axis_index("y")).reshape(1).astype(jnp.int32)
    got = _pair_send(gs, name=tag + "_pair")
    pairs = [_pair_sum(g, r, core, name=tag + "_pair_sum") for g, r in zip(gs, got)]
    others = _quad_send(pairs, name=tag + "_quad")
    mine = [_quad_sum(p, o, chip, name=tag + "_quad_sum") for p, o in zip(pairs, others)]
    theirs = _sibling_send(mine, name=tag + "_share")
    return [jnp.where(mc == 0, jnp.concatenate([m, t], axis=0), jnp.concatenate([t, m], axis=0))
            for m, t in zip(mine, theirs)]


def _round_up(n, q):
    return -(-n // q) * q


def _pack_rows(flat, quantum_rows):
    n = flat.shape[-1]
    total = _round_up(n, PACK_W * quantum_rows)
    pad = [(0, 0)] * (flat.ndim - 1) + [(0, total - n)]
    return jnp.pad(flat, pad).reshape(flat.shape[:-1] + (total // PACK_W, PACK_W))


def _split_shards(full, axis):
    if axis == 0:
        return full.reshape((4, full.shape[0] // 4, full.shape[1]))
    width = full.shape[1] // 4
    return jnp.stack([full[:, s * width:(s + 1) * width] for s in range(4)])


def _join_shards(blocks, axis):
    if axis == 0:
        return blocks.reshape((4 * blocks.shape[1], blocks.shape[2]))
    return jnp.concatenate([blocks[s] for s in range(4)], axis=1)


def _pack_small(tree, names):
    return _pack_rows(jnp.concatenate([tree[name].reshape(-1) for name in names]), 8)


def _unpack_small(packed, shapes, names):
    lead = packed.shape[:-2]
    flat = packed.reshape(lead + (-1,))
    out, off = {}, 0
    for name in names:
        n = int(np.prod(shapes[name]))
        out[name] = flat[..., off:off + n].reshape(lead + tuple(shapes[name]))
        off += n
    return out


def _rope(t, cos, sin):
    half = QK_ROPE // 2
    t1, t2 = t[..., :half], t[..., half:]
    return jnp.concatenate([t1 * cos - t2 * sin, t2 * cos + t1 * sin], axis=-1)


def _pad_in_cols(a):
    z = jnp.zeros(a.shape[:-1] + (IN_PAD,), a.dtype)
    return jnp.concatenate([a[..., :IN_PAD_AT], z, a[..., IN_PAD_AT:]], axis=-1)


def _pool_constants(seq):
    taps = max(POOL_WINDOWS)
    win = jnp.repeat(jnp.asarray(POOL_WINDOWS, jnp.int32), 512 // len(POOL_WINDOWS))
    lag = taps - 1 - jnp.arange(taps, dtype=jnp.int32)
    mask = (lag[:, None] < win[None, :]).astype(F32)
    cnt = jnp.minimum(jnp.arange(seq, dtype=jnp.int32)[:, None] + 1, win[None, :]).astype(F32)
    return mask, 1.0 / cnt


def _block_diag(w):
    g, n, _ = w.shape
    out = jnp.zeros((g * n, g * n), w.dtype)
    for i in range(g):
        out = lax.dynamic_update_slice(out, w[i], (i * n, i * n))
    return out


def _layer(x, c_act, p, wb, cos, sin, pool_mask, pool_inv_cnt, tag):
    seq = x.shape[0]
    d = D_MODEL
    mod = linear_bias(c_act, p["w_ada"], wb["w_ada"], p["b_ada"], tag + "ada")[0]
    sh1, sc1, g1, sh2, sc2, g2 = [mod[i * d:(i + 1) * d][None, :] for i in range(6)]

    h = x * (1.0 + sc1) + sh1
    w_slot, w_in, b_in = _pad_in_cols(p["w_in"]), _pad_in_cols(wb["w_in"]), _pad_in_cols(p["b_in"])
    proj = linear_bias(h, w_slot[:, :IN_GATES_AT], w_in[:, :IN_GATES_AT], b_in[:IN_GATES_AT], tag + "in")
    gates = linear_bias(h, w_slot[:, IN_GATES_AT:], w_in[:, IN_GATES_AT:], b_in[IN_GATES_AT:], tag + "in_gates")
    conv_a, conv_b, sc_bg, sc_cg, sc_x, q_lat, kv_lat, k_rope, pool_u = split_cols(proj, IN_SPLITS)

    ya = conv_a * jax.nn.sigmoid(conv_b)
    ya = dwconv(ya, p["conv_dw"], tag + "convA")
    ya = jax.nn.silu(layer_norm(ya, p["conv_ln_g"], p["conv_ln_b"], tag + "convA_ln"))
    ya = linear(ya, p["w_conv_out"], wb["w_conv_out"], tag + "convA_out")

    yb = linear(sc_bg * dwconv(sc_cg * sc_x, p["sc_dw"], tag + "sc"), p["w_sc_out"], wb["w_sc_out"],
                tag + "sc_out")

    q = linear(rms_norm(q_lat, p["q_norm_g"], tag + "q_rms"), p["w_uq"], wb["w_uq"], tag + "uq")
    q = q.reshape(seq, N_HEADS, QK_NOPE + QK_ROPE)
    kv = linear(rms_norm(kv_lat, p["kv_norm_g"], tag + "kv_rms"), p["w_ukv"], wb["w_ukv"], tag + "ukv")
    kv = kv.reshape(seq, N_HEADS, QK_NOPE + V_DIM)
    q_rope = _rope(q[..., QK_NOPE:], cos[:, None, :], sin[:, None, :])
    k_rope_r = jnp.broadcast_to(_rope(k_rope, cos, sin)[:, None, :], (seq, N_HEADS, QK_ROPE))
    zq = jnp.zeros((seq, N_HEADS, HEAD_PAD - QK_NOPE - QK_ROPE), F32)
    zv = jnp.zeros((seq, N_HEADS, HEAD_PAD - V_DIM), F32).at[:, :, DEN_COL - V_DIM].set(1.0)
    qh = jnp.concatenate([q[..., :QK_NOPE], q_rope, zq], axis=-1).transpose(1, 0, 2)
    kh = jnp.concatenate([kv[..., :QK_NOPE], k_rope_r, zq], axis=-1).transpose(1, 0, 2)
    vh = jnp.concatenate([kv[..., QK_NOPE:], zv], axis=-1).transpose(1, 0, 2)
    att = attention(qh, kh, vh, tag + "attn")
    att = att[:, :, :V_DIM].transpose(1, 0, 2).reshape(seq, N_HEADS * V_DIM)
    yc = linear(att, p["w_mla_out"], wb["w_mla_out"], tag + "mla_out")

    pd = dwconv(pool_u, pool_mask, tag + "pool") * pool_inv_cnt - pool_u
    w_mix = _block_diag(p["w_pool"])
    yd = linear(pd, w_mix, w_mix, tag + "pool_mix") * p["pool_scale"][None, :]
    yd = linear(yd, p["w_pool_out"], wb["w_pool_out"], tag + "pool_out")

    merged = gate_merge(gates, (ya, yb, yc, yd), tag + "merge")
    mix = linear(merged, p["w_o"], wb["w_o"], tag + "o")
    x = layer_norm(ALPHA * x + (1.0 + g1) * mix, p["ln1_g"], p["ln1_b"], tag + "ln1")

    h = x * (1.0 + sc2) + sh2
    up = dwconv(linear(h, p["w_up"], wb["w_up"], tag + "up"), p["ffn_dw"], tag + "ffn_conv")
    d_ff = up.shape[1] // 2
    val, gate = split_cols(up, ((0, d_ff), (d_ff, 2 * d_ff)))
    ffn = linear(jax.nn.silu(gate) * val, p["w_down"], wb["w_down"], tag + "down")
    return layer_norm(ALPHA * x + (1.0 + g2) * ffn, p["ln2_g"], p["ln2_b"], tag + "ln2")


def _forward(x, layers, gathered, c_act, cos, sin, pool_mask, pool_inv_cnt):
    for p, wb in zip(layers, gathered):
        x = _layer(x, c_act, p, wb, cos, sin, pool_mask, pool_inv_cnt, "")
    return x


def kernel(x, c, positions, w_ada, b_ada, w_in, b_in, conv_dw, conv_ln_g, conv_ln_b, w_conv_out, sc_dw, w_sc_out, q_norm_g, w_uq, kv_norm_g, w_ukv, w_mla_out, w_pool, pool_scale, w_pool_out, w_o, ln1_g, ln1_b, w_up, ffn_dw, w_down, ln2_g, ln2_b, loss_target, m_w_ada, m_b_ada, m_w_in, m_b_in, m_conv_dw, m_conv_ln_g, m_conv_ln_b, m_w_conv_out, m_sc_dw, m_w_sc_out, m_q_norm_g, m_w_uq, m_kv_norm_g, m_w_ukv, m_w_mla_out, m_w_pool, m_pool_scale, m_w_pool_out, m_w_o, m_ln1_g, m_ln1_b, m_w_up, m_ffn_dw, m_w_down, m_ln2_g, m_ln2_b, v_w_ada, v_b_ada, v_w_in, v_b_in, v_conv_dw, v_conv_ln_g, v_conv_ln_b, v_w_conv_out, v_sc_dw, v_w_sc_out, v_q_norm_g, v_w_uq, v_kv_norm_g, v_w_ukv, v_w_mla_out, v_w_pool, v_pool_scale, v_w_pool_out, v_w_o, v_ln1_g, v_ln1_b, v_w_up, v_ffn_dw, v_w_down, v_ln2_g, v_ln2_b):
    given = dict(locals())
    weights = {n: given[n] for n in WEIGHT_ORDER}
    mom_m = {n: given["m_" + n] for n in WEIGHT_ORDER}
    mom_v = {n: given["v_" + n] for n in WEIGHT_ORDER}
    depth = w_ada.shape[0]
    seq = x.shape[1]
    big = [(s[0], s[2]) for s in SHARDED if s[3]]
    tiny = [s[0] for s in SHARDED if not s[3]]
    chip = 2 * lax.axis_index("x") + lax.axis_index("y")

    tiny_shapes = {n: weights[n].shape for n in tiny}
    tiny_all = _exchange(_pack_small(weights, tiny), group="chips", broadcast=True, name="gather_taps")
    tiny_full = {n: jnp.concatenate([a[s] for s in range(4)], axis=-1)
                 for n, a in _unpack_small(tiny_all, tiny_shapes, tiny).items()}
    layers, gathered = [], []
    for l in range(depth):
        blocks = _gather_shards([weights[n][l].astype(BF16) for n, _ in big], name="gather_weights")
        gathered.append({n: _join_shards(a, axis) for (n, axis), a in zip(big, blocks)})
        p = {n: jnp.zeros(gathered[l][n].shape, F32) for n, _ in big}
        for n in tiny:
            p[n] = tiny_full[n][l]
        for n in REPLICATED:
            p[n] = weights[n][l]
        layers.append(p)

    inv = 1.0 / (ROPE_THETA ** (jnp.arange(0, QK_ROPE, 2, dtype=F32) / QK_ROPE))
    ang = positions[0].astype(F32)[:, None] * inv
    cos, sin = jnp.cos(ang), jnp.sin(ang)
    c_act = jnp.pad(jax.nn.silu(c), ((0, 15), (0, 0)))
    pool_mask, pool_inv_cnt = _pool_constants(seq)

    y, vjp_fn = jax.vjp(lambda xx, ll: _forward(xx, ll, gathered, c_act, cos, sin, pool_mask, pool_inv_cnt),
                        x[0], layers)
    loss_local, dy = _loss_call(y, loss_target[0])
    grad_x, grad_layers = vjp_fn(dy)
    loss = lax.psum(loss_local, ("x", "y", "c"))

    outs = {"grad": {}, "delta": {}, "m": {}, "v": {}}
    reduced = [_reduce_scatter([_split_shards(grad_layers[l][n], axis) for n, axis in big], tag="rs")
               for l in range(depth)]
    for t, (n, _) in enumerate(big):
        outs["grad"][n] = jnp.stack([reduced[l][t] for l in range(depth)])
    small = list(REPLICATED) + tiny
    small_local = {n: jnp.stack([grad_layers[l][n] for l in range(depth)]) for n in small}
    small_shapes = {n: small_local[n].shape for n in small}
    small_all = _exchange(_pack_small(small_local, small), group="all", broadcast=True, name="gather_small_grads")
    small_sum = _unpack_small(_ordered_sum(small_all, name="small_grads_sum"), small_shapes, small)
    for n in REPLICATED:
        outs["grad"][n] = small_sum[n]
    for n in tiny:
        width = weights[n].shape[-1]
        outs["grad"][n] = lax.dynamic_slice_in_dim(small_sum[n], chip * width, width, axis=2)

    for n in WEIGHT_ORDER:
        outs["delta"][n], outs["m"][n], outs["v"][n] = _adamw_call(
            weights[n], outs["grad"][n], mom_m[n], mom_v[n], name="adamw_" + n)

    result = [loss, grad_x[None]]
    for key in ("grad", "delta", "m", "v"):
        result.extend(outs[key][n] for n in WEIGHT_ORDER)
    return tuple(result)
```

```python
import functools
import math

import jax
import jax.numpy as jnp
import numpy as np
from jax import lax
from jax.experimental import pallas as pl
from jax.experimental.pallas import tpu as pltpu

F32 = jnp.float32
BF16 = jnp.bfloat16
MXU = jnp.bfloat16

D_MODEL = 1024
N_HEADS = 8
QK_NOPE = 64
QK_ROPE = 32
V_DIM = 64
HEAD_PAD = 128
DEN_COL = V_DIM
ROPE_THETA = 10000.0
POOL_WINDOWS = (2, 4, 8, 16)
LN_EPS = 1e-5
RMS_EPS = 1e-6
DEPTH_FOR_DEEPNORM = 4
ALPHA = (2.0 * DEPTH_FOR_DEEPNORM) ** 0.25
ATTN_SCALE = (QK_NOPE + QK_ROPE) ** -0.5
LOG2_E = math.log2(math.e)
IN_COLS = 7584
IN_PAD_AT = 2976
IN_PAD = 96
IN_SPLITS = ((0, 512), (512, 1024), (1024, 1536), (1536, 2048), (2048, 2560), (2560, 2816), (2816, 2944),
             (2944, 2976), (3072, 3584))
IN_GATES_AT = 3584
ADAM_LR, ADAM_B1, ADAM_B2, ADAM_EPS, ADAM_WD, ADAM_STEP = 0.001, 0.9, 0.999, 1e-08, 0.01, 10

VMEM_LIMIT = 48 * 1024 * 1024
LANE = 128
CONV_HALO = 32
PACK_W = 512
MESH_T = pl.DeviceIdType.MESH

SHARDED = (
    ("w_ada", (1024, 6144), 1, True),
    ("w_in", (1024, IN_COLS), 1, True),
    ("conv_dw", (31, 512), 1, False),
    ("w_conv_out", (512, 1024), 1, True),
    ("sc_dw", (3, 512), 1, False),
    ("w_sc_out", (512, 1024), 1, True),
    ("w_uq", (256, 768), 1, True),
    ("w_ukv", (128, 1024), 1, True),
    ("w_mla_out", (512, 1024), 1, True),
    ("w_pool_out", (512, 1024), 1, True),
    ("w_o", (1024, 1024), 0, True),
    ("w_up", (1024, 5632), 1, True),
    ("ffn_dw", (3, 5632), 1, False),
    ("w_down", (2816, 1024), 0, True),
)
REPLICATED = ("b_ada", "b_in", "conv_ln_g", "conv_ln_b", "q_norm_g", "kv_norm_g", "w_pool", "pool_scale",
              "ln1_g", "ln1_b", "ln2_g", "ln2_b")
WEIGHT_ORDER = ("w_ada", "b_ada", "w_in", "b_in", "conv_dw", "conv_ln_g", "conv_ln_b", "w_conv_out", "sc_dw",
                "w_sc_out", "q_norm_g", "w_uq", "kv_norm_g", "w_ukv", "w_mla_out", "w_pool", "pool_scale",
                "w_pool_out", "w_o", "ln1_g", "ln1_b", "w_up", "ffn_dw", "w_down", "ln2_g", "ln2_b")


def _pick(dim, cap, quantum=LANE):
    best = None
    t = quantum
    while t <= min(dim, cap):
        if dim % t == 0:
            best = t
        t += quantum
    return dim if best is None else best


def _params(n_grid):
    return pltpu.CompilerParams(dimension_semantics=("arbitrary",) * n_grid, vmem_limit_bytes=VMEM_LIMIT)


def _mm_tile(dim):
    t = _pick(dim, 1024)
    return t if t >= 512 or t == dim else _pick(dim, 1536)


def _mm(a, b, *, ta=False, tb=False, bias=None, name):
    if ta:
        k_dim, m_dim = a.shape
    else:
        m_dim, k_dim = a.shape
    if tb:
        n_dim, k2 = b.shape
    else:
        k2, n_dim = b.shape
    assert k_dim == k2, (a.shape, b.shape, ta, tb)
    tm, tn, tk = _mm_tile(m_dim), _mm_tile(n_dim), _mm_tile(k_dim)
    nk = k_dim // tk
    a_spec = pl.BlockSpec((tk, tm), lambda i, j, k: (k, i)) if ta else pl.BlockSpec((tm, tk), lambda i, j, k: (i, k))
    b_spec = pl.BlockSpec((tn, tk), lambda i, j, k: (j, k)) if tb else pl.BlockSpec((tk, tn), lambda i, j, k: (k, j))
    dims = (((0 if ta else 1,), (1 if tb else 0,)), ((), ()))
    has_bias = bias is not None

    def body(*refs):
        a_ref, b_ref = refs[:2]
        bias_ref = refs[2] if has_bias else None
        o_ref = refs[3] if has_bias else refs[2]
        part = lax.dot_general(a_ref[...].astype(MXU), b_ref[...].astype(MXU), dims, preferred_element_type=F32)
        if nk == 1:
            o_ref[...] = part + bias_ref[...] if has_bias else part
            return
        acc_ref = refs[-1]
        k = pl.program_id(2)

        @pl.when(k == 0)
        def _():
            acc_ref[...] = part

        @pl.when(k > 0)
        def _():
            acc_ref[...] += part

        @pl.when(k == nk - 1)
        def _():
            out = acc_ref[...]
            if has_bias:
                out = out + bias_ref[...]
            o_ref[...] = out

    in_specs = [a_spec, b_spec]
    args = [a, b]
    if has_bias:
        in_specs.append(pl.BlockSpec((1, tn), lambda i, j, k: (0, j)))
        args.append(bias.reshape(1, n_dim))
    return pl.pallas_call(
        body, name=name, grid=(m_dim // tm, n_dim // tn, nk), in_specs=in_specs,
        out_specs=pl.BlockSpec((tm, tn), lambda i, j, k: (i, j)),
        out_shape=jax.ShapeDtypeStruct((m_dim, n_dim), F32),
        scratch_shapes=[] if nk == 1 else [pltpu.VMEM((tm, tn), F32)], compiler_params=_params(3),
    )(*args)


COLSUM_BLOCK_ELEMS = 1024 * 1024


def _colsum(x, *, name):
    rows, n = x.shape
    tr = _pick(rows, max(8, COLSUM_BLOCK_ELEMS // n), 8)

    def body(x_ref, o_ref):
        @pl.when(pl.program_id(0) == 0)
        def _():
            o_ref[...] = jnp.zeros_like(o_ref)

        o_ref[...] += jnp.sum(x_ref[...], axis=0, keepdims=True)

    out = pl.pallas_call(
        body, name=name, grid=(rows // tr,), in_specs=[pl.BlockSpec((tr, n), lambda i: (i, 0))],
        out_specs=pl.BlockSpec((1, n), lambda i: (0, 0)), out_shape=jax.ShapeDtypeStruct((1, n), F32),
        compiler_params=_params(1),
    )(x)
    return out[0]


@functools.partial(jax.custom_vjp, nondiff_argnums=(3,))
def linear(x, slot, w, name):
    return _mm(x, w, name=name + "_f")


def _linear_fwd(x, slot, w, name):
    return _mm(x, w, name=name + "_f"), (x, w)


def _weight_grad(x, dy, name):
    return _mm(x, dy, ta=True, name=name)


def _linear_bwd(name, res, dy):
    x, w = res
    return _mm(dy, w, tb=True, name=name + "_dx"), _weight_grad(x, dy, name + "_dw"), jnp.zeros_like(w)


linear.defvjp(_linear_fwd, _linear_bwd)


@functools.partial(jax.custom_vjp, nondiff_argnums=(4,))
def linear_bias(x, slot, w, b, name):
    return _mm(x, w, bias=b, name=name + "_f")


def _linear_bias_fwd(x, slot, w, b, name):
    return _mm(x, w, bias=b, name=name + "_f"), (x, w)


def _linear_bias_bwd(name, res, dy):
    x, w = res
    return (_mm(dy, w, tb=True, name=name + "_dx"), _weight_grad(x, dy, name + "_dw"), jnp.zeros_like(w),
            _colsum(dy, name=name + "_db"))


linear_bias.defvjp(_linear_bias_fwd, _linear_bias_bwd)


@functools.partial(jax.custom_vjp, nondiff_argnums=(1,))
def split_cols(x, bounds):
    return tuple(x[:, a:b] for a, b in bounds)


def _split_cols_fwd(x, bounds):
    return split_cols(x, bounds), x.shape[1]


def _split_cols_bwd(bounds, width, cts):
    rows = cts[0].shape[0]
    parts, at = [], 0
    for (a, b), ct in zip(bounds, cts):
        if a > at:
            parts.append(jnp.zeros((rows, a - at), ct.dtype))
        parts.append(ct)
        at = b
    if at < width:
        parts.append(jnp.zeros((rows, width - at), cts[0].dtype))
    return (jnp.concatenate(parts, axis=1),)


split_cols.defvjp(_split_cols_fwd, _split_cols_bwd)


def _norm_stats(x, center, eps):
    if center:
        mu = jnp.mean(x, axis=-1, keepdims=True)
        xc = x - mu
    else:
        xc = x
    rstd = lax.rsqrt(jnp.mean(xc * xc, axis=-1, keepdims=True) + eps)
    return xc * rstd


def _norm_fwd_call(x, g, b, *, center, name):
    rows, d = x.shape
    tr = _pick(rows, 512, 8)
    eps = LN_EPS if center else RMS_EPS

    def body(x_ref, g_ref, b_ref, o_ref):
        xhat = _norm_stats(x_ref[...], center, eps)
        y = xhat * g_ref[...]
        if center:
            y = y + b_ref[...]
        o_ref[...] = y

    vec = pl.BlockSpec((1, d), lambda i: (0, 0))
    return pl.pallas_call(
        body, name=name, grid=(rows // tr,), in_specs=[pl.BlockSpec((tr, d), lambda i: (i, 0)), vec, vec],
        out_specs=pl.BlockSpec((tr, d), lambda i: (i, 0)), out_shape=jax.ShapeDtypeStruct((rows, d), F32),
        compiler_params=_params(1),
    )(x, g.reshape(1, d), b.reshape(1, d))


def _norm_bwd_call(x, g, dy, *, center, name):
    rows, d = x.shape
    tr = _pick(rows, 512, 8)
    eps = LN_EPS if center else RMS_EPS

    def body(x_ref, g_ref, dy_ref, dx_ref, dg_ref, db_ref):
        @pl.when(pl.program_id(0) == 0)
        def _():
            dg_ref[...] = jnp.zeros_like(dg_ref)
            db_ref[...] = jnp.zeros_like(db_ref)

        x = x_ref[...]
        dy = dy_ref[...]
        if center:
            mu = jnp.mean(x, axis=-1, keepdims=True)
            xc = x - mu
        else:
            xc = x
        rstd = lax.rsqrt(jnp.mean(xc * xc, axis=-1, keepdims=True) + eps)
        xhat = xc * rstd
        dyg = dy * g_ref[...]
        proj = jnp.mean(dyg * xhat, axis=-1, keepdims=True)
        dx = dyg - xhat * proj
        if center:
            dx = dx - jnp.mean(dyg, axis=-1, keepdims=True)
        dx_ref[...] = dx * rstd
        dg_ref[...] += jnp.sum(dy * xhat, axis=0, keepdims=True)
        db_ref[...] += jnp.sum(dy, axis=0, keepdims=True)

    vec = pl.BlockSpec((1, d), lambda i: (0, 0))
    row = pl.BlockSpec((tr, d), lambda i: (i, 0))
    dx, dg, db = pl.pallas_call(
        body, name=name, grid=(rows // tr,), in_specs=[row, vec, row], out_specs=[row, vec, vec],
        out_shape=[jax.ShapeDtypeStruct((rows, d), F32), jax.ShapeDtypeStruct((1, d), F32),
                   jax.ShapeDtypeStruct((1, d), F32)],
        compiler_params=_params(1),
    )(x, g.reshape(1, d), dy)
    return dx, dg[0], db[0]


@functools.partial(jax.custom_vjp, nondiff_argnums=(3,))
def layer_norm(x, g, b, name):
    return _norm_fwd_call(x, g, b, center=True, name=name + "_f")


def _layer_norm_fwd(x, g, b, name):
    return _norm_fwd_call(x, g, b, center=True, name=name + "_f"), (x, g)


def _layer_norm_bwd(name, res, dy):
    x, g = res
    return _norm_bwd_call(x, g, dy, center=True, name=name + "_b")


layer_norm.defvjp(_layer_norm_fwd, _layer_norm_bwd)


@functools.partial(jax.custom_vjp, nondiff_argnums=(2,))
def rms_norm(x, g, name):
    return _norm_fwd_call(x, g, jnp.zeros_like(g), center=False, name=name + "_f")


def _rms_norm_fwd(x, g, name):
    return _norm_fwd_call(x, g, jnp.zeros_like(g), center=False, name=name + "_f"), (x, g)


def _rms_norm_bwd(name, res, dy):
    x, g = res
    dx, dg, _ = _norm_bwd_call(x, g, dy, center=False, name=name + "_b")
    return dx, dg


rms_norm.defvjp(_rms_norm_fwd, _rms_norm_bwd)


def _conv_tiles(rows, ch):
    tr = _pick(rows, 512, CONV_HALO)
    assert tr >= CONV_HALO and rows % tr == 0
    return tr, _mm_tile(ch)


def _pad_taps(w):
    k = w.shape[0]
    kp = -(-k // 8) * 8
    return jnp.pad(w, ((0, kp - k), (0, 0))), k, kp


def _dwconv_fwd_call(x, w, *, name):
    rows, ch = x.shape
    tr, tc = _conv_tiles(rows, ch)
    wp, taps, kp = _pad_taps(w)
    assert taps - 1 <= CONV_HALO
    halo_per_tile = tr // CONV_HALO

    def body(x_ref, xprev_ref, w_ref, o_ref):
        i = pl.program_id(1)
        halo = xprev_ref[...]
        halo = jnp.where(i > 0, halo, jnp.zeros_like(halo))
        xx = jnp.concatenate([halo, x_ref[...]], axis=0)
        acc = jnp.zeros((tr, tc), F32)
        for k in range(taps):
            shift = taps - 1 - k
            term = xx if shift == 0 else pltpu.roll(xx, shift, 0)
            acc = acc + w_ref[k:k + 1, :] * term[CONV_HALO:, :]
        o_ref[...] = acc

    return pl.pallas_call(
        body, name=name, grid=(ch // tc, rows // tr),
        in_specs=[pl.BlockSpec((tr, tc), lambda j, i: (i, j)),
                  pl.BlockSpec((CONV_HALO, tc), lambda j, i: (jnp.maximum(i * halo_per_tile - 1, 0), j)),
                  pl.BlockSpec((kp, tc), lambda j, i: (0, j))],
        out_specs=pl.BlockSpec((tr, tc), lambda j, i: (i, j)), out_shape=jax.ShapeDtypeStruct((rows, ch), F32),
        compiler_params=_params(2),
    )(x, x, wp)


def _dwconv_bwd_call(x, w, dy, *, name):
    rows, ch = x.shape
    tr, tc = _conv_tiles(rows, ch)
    wp, taps, kp = _pad_taps(w)
    n_row_tiles = rows // tr
    halo_per_tile = tr // CONV_HALO
    n_halo_blocks = rows // CONV_HALO
    ext = tr + CONV_HALO

    def body(x_ref, xprev_ref, dy_ref, dynext_ref, w_ref, dx_ref, dw_ref):
        i = pl.program_id(1)

        @pl.when(i == 0)
        def _():
            dw_ref[...] = jnp.zeros_like(dw_ref)

        halo = xprev_ref[...]
        halo = jnp.where(i > 0, halo, jnp.zeros_like(halo))
        xx = jnp.concatenate([halo, x_ref[...]], axis=0)
        dy = dy_ref[...]
        ahead = dynext_ref[...]
        ahead = jnp.where(i < n_row_tiles - 1, ahead, jnp.zeros_like(ahead))
        yy = jnp.concatenate([dy, ahead], axis=0)
        dx = jnp.zeros((tr, tc), F32)
        for k in range(taps):
            shift = taps - 1 - k
            fwd = yy if shift == 0 else pltpu.roll(yy, ext - shift, 0)
            dx = dx + w_ref[k:k + 1, :] * fwd[:tr, :]
            back = xx if shift == 0 else pltpu.roll(xx, shift, 0)
            dw_ref[k:k + 1, :] += jnp.sum(dy * back[CONV_HALO:, :], axis=0, keepdims=True)
        dx_ref[...] = dx

    cur = pl.BlockSpec((tr, tc), lambda j, i: (i, j))
    dx, dw = pl.pallas_call(
        body, name=name, grid=(ch // tc, n_row_tiles),
        in_specs=[cur, pl.BlockSpec((CONV_HALO, tc), lambda j, i: (jnp.maximum(i * halo_per_tile - 1, 0), j)), cur,
                  pl.BlockSpec((CONV_HALO, tc),
                               lambda j, i: (jnp.minimum((i + 1) * halo_per_tile, n_halo_blocks - 1), j)),
                  pl.BlockSpec((kp, tc), lambda j, i: (0, j))],
        out_specs=[cur, pl.BlockSpec((kp, tc), lambda j, i: (0, j))],
        out_shape=[jax.ShapeDtypeStruct((rows, ch), F32), jax.ShapeDtypeStruct((kp, ch), F32)],
        compiler_params=_params(2),
    )(x, x, dy, dy, wp)
    return dx, dw[:taps]


@functools.partial(jax.custom_vjp, nondiff_argnums=(2,))
def dwconv(x, w, name):
    return _dwconv_fwd_call(x, w, name=name + "_f")


def _dwconv_fwd(x, w, name):
    return _dwconv_fwd_call(x, w, name=name + "_f"), (x, w)


def _dwconv_bwd(name, res, dy):
    x, w = res
    return _dwconv_bwd_call(x, w, dy, name=name + "_b")


dwconv.defvjp(_dwconv_fwd, _dwconv_bwd)


def _attn_tiles(seq):
    return _pick(seq, 1024), _pick(seq, 512)


def _lane_tile(v, width):
    return v if width == LANE else jnp.tile(v, (1, width // LANE))


def _causal_pairs(nq, nk, tq, tk, key_major):
    pairs = [(i, j) for i in range(nq) for j in range(nk) if j * tk <= i * tq + tq - 1]
    if key_major:
        pairs.sort(key=lambda p: (p[1], p[0]))
    return (jnp.asarray(np.array([p[0] for p in pairs], np.int32)),
            jnp.asarray(np.array([p[1] for p in pairs], np.int32)))


def _scores(q, k, i, j, tq, tk, masked):
    z = lax.dot_general(q, k, (((1,), (1,)), ((), ())), preferred_element_type=F32)
    if masked:
        row = i * tq + lax.broadcasted_iota(jnp.int32, (tq, tk), 0)
        col = j * tk + lax.broadcasted_iota(jnp.int32, (tq, tk), 1)
        z = jnp.where(col <= row, z, -jnp.inf)
    return z


def _attn_fwd_call(q, k, v, *, name):
    heads, seq, dh = q.shape
    tq, tk = _attn_tiles(seq)
    nq, nk = seq // tq, seq // tk
    qi, kj = _causal_pairs(nq, nk, tq, tk, key_major=False)

    def body(qi_ref, kj_ref, q_ref, k_ref, v_ref, o_ref, lse_ref, m_sc, acc_sc):
        t = pl.program_id(1)
        i = qi_ref[t]
        j = kj_ref[t]

        @pl.when(j == 0)
        def _():
            m_sc[...] = jnp.full(m_sc.shape, -jnp.inf, F32)
            acc_sc[...] = jnp.zeros_like(acc_sc)

        def step(masked):
            z = _scores(q_ref[0], k_ref[0], i, j, tq, tk, masked)
            m_prev = m_sc[...]
            m_new = jnp.maximum(m_prev, jnp.max(z, axis=1, keepdims=True))
            alpha = jnp.exp2(m_prev - m_new)
            p = jnp.exp2(z - _lane_tile(m_new, tk))
            acc_sc[...] = alpha * acc_sc[...] + jnp.dot(p.astype(MXU), v_ref[0], preferred_element_type=F32)
            m_sc[...] = m_new

        crosses_diagonal = j * tk + tk - 1 > i * tq

        @pl.when(jnp.logical_not(crosses_diagonal))
        def _():
            step(False)

        @pl.when(crosses_diagonal)
        def _():
            step(True)

        @pl.when(j == (i * tq + tq - 1) // tk)
        def _():
            acc = acc_sc[...]
            lane = lax.broadcasted_iota(jnp.int32, acc.shape, 1)
            den = jnp.sum(jnp.where(lane == DEN_COL, acc, 0.0), axis=1, keepdims=True)
            o_ref[0] = acc / den
            lse_ref[0] = m_sc[...] + jnp.log(den) * LOG2_E

    q_spec = pl.BlockSpec((1, tq, dh), lambda h, t, qi_ref, kj_ref: (h, qi_ref[t], 0))
    kv_spec = pl.BlockSpec((1, tk, dh), lambda h, t, qi_ref, kj_ref: (h, kj_ref[t], 0))
    return pl.pallas_call(
        body, name=name,
        grid_spec=pltpu.PrefetchScalarGridSpec(
            num_scalar_prefetch=2, grid=(heads, int(qi.shape[0])), in_specs=[q_spec, kv_spec, kv_spec],
            out_specs=[q_spec, q_spec],
            scratch_shapes=[pltpu.VMEM((tq, LANE), F32), pltpu.VMEM((tq, dh), F32)]),
        out_shape=[jax.ShapeDtypeStruct((heads, seq, dh), F32), jax.ShapeDtypeStruct((heads, seq, LANE), F32)],
        compiler_params=_params(2),
    )(qi, kj, q, k, v)


def _attn_bwd_call(q, k, v, o, lse, do, *, name):
    heads, seq, dh = q.shape
    tk, tq = _attn_tiles(seq)
    nq, nk = seq // tq, seq // tk
    qi, kj = _causal_pairs(nq, nk, tq, tk, key_major=True)

    def body(qi_ref, kj_ref, q_ref, k_ref, v_ref, o_ref, lse_ref, do_ref, dq_ref, dk_ref, dv_ref, dk_sc, dv_sc):
        t = pl.program_id(1)
        i = qi_ref[t]
        j = kj_ref[t]
        first_i = (j * tk) // tq

        @pl.when(t == 0)
        def _():
            dq_ref[...] = jnp.zeros_like(dq_ref)

        @pl.when(i == first_i)
        def _():
            dk_sc[...] = jnp.zeros_like(dk_sc)
            dv_sc[...] = jnp.zeros_like(dv_sc)

        def step(masked):
            qb = q_ref[0]
            kb = k_ref[0]
            do_f = do_ref[0]
            do_b = do_f.astype(MXU)
            p = jnp.exp2(_scores(qb, kb, i, j, tq, tk, masked) - _lane_tile(lse_ref[0], tk))
            dp = lax.dot_general(do_b, v_ref[0], (((1,), (1,)), ((), ())), preferred_element_type=F32)
            delta = jnp.sum(do_f * o_ref[0], axis=1, keepdims=True)
            ds = (p * (dp - delta)).astype(MXU)
            dv_sc[...] += lax.dot_general(p.astype(MXU), do_b, (((0,), (0,)), ((), ())), preferred_element_type=F32)
            dk_sc[...] += lax.dot_general(ds, qb, (((0,), (0,)), ((), ())), preferred_element_type=F32)
            rows = pl.ds(pl.multiple_of(i * tq, tq), tq)
            dq_ref[0, rows, :] += jnp.dot(ds, kb, preferred_element_type=F32) * ATTN_SCALE

        crosses_diagonal = j * tk + tk - 1 > i * tq

        @pl.when(jnp.logical_not(crosses_diagonal))
        def _():
            step(False)

        @pl.when(crosses_diagonal)
        def _():
            step(True)

        @pl.when(i == nq - 1)
        def _():
            dk_ref[0] = dk_sc[...] * (1.0 / LOG2_E)
            dv_ref[0] = dv_sc[...]

    def q_map(h, t, qi_ref, kj_ref):
        return (h, qi_ref[t], 0)

    def kv_map(h, t, qi_ref, kj_ref):
        return (h, kj_ref[t], 0)

    q_spec = pl.BlockSpec((1, tq, dh), q_map)
    lse_spec = pl.BlockSpec((1, tq, LANE), q_map)
    kv_spec = pl.BlockSpec((1, tk, dh), kv_map)
    head_spec = pl.BlockSpec((1, seq, dh), lambda h, t, qi_ref, kj_ref: (h, 0, 0))
    return pl.pallas_call(
        body, name=name,
        grid_spec=pltpu.PrefetchScalarGridSpec(
            num_scalar_prefetch=2, grid=(heads, int(qi.shape[0])),
            in_specs=[q_spec, kv_spec, kv_spec, q_spec, lse_spec, q_spec], out_specs=[head_spec, kv_spec, kv_spec],
            scratch_shapes=[pltpu.VMEM((tk, dh), F32), pltpu.VMEM((tk, dh), F32)]),
        out_shape=[jax.ShapeDtypeStruct((heads, seq, dh), F32)] * 3,
        compiler_params=_params(2),
    )(qi, kj, q, k, v, o, lse, do)


@functools.partial(jax.custom_vjp, nondiff_argnums=(3,))
def attention(q, k, v, name):
    return _attention_fwd(q, k, v, name)[0]


def _attention_fwd(q, k, v, name):
    qb, kb, vb = (q * (ATTN_SCALE * LOG2_E)).astype(MXU), k.astype(MXU), v.astype(MXU)
    o, lse = _attn_fwd_call(qb, kb, vb, name=name + "_f")
    return o, (qb, kb, vb, o, lse)


def _attention_bwd(name, res, do):
    qb, kb, vb, o, lse = res
    return tuple(_attn_bwd_call(qb, kb, vb, o, lse, do, name=name + "_b"))


attention.defvjp(_attention_fwd, _attention_bwd)


def _merge_tiles(rows):
    return _pick(rows, 128, 8)


def _gate_merge_fwd_call(gates, ys, *, name):
    rows, d = ys[0].shape
    nb = len(ys)
    tr = _merge_tiles(rows)

    def body(g_ref, *refs):
        o_ref = refs[nb]
        acc = jax.nn.sigmoid(g_ref[:, 0:d]) * refs[0][...]
        for b in range(1, nb):
            acc = acc + jax.nn.sigmoid(g_ref[:, b * d:(b + 1) * d]) * refs[b][...]
        o_ref[...] = acc

    wide = pl.BlockSpec((tr, nb * d), lambda i: (i, 0))
    row = pl.BlockSpec((tr, d), lambda i: (i, 0))
    return pl.pallas_call(
        body, name=name, grid=(rows // tr,), in_specs=[wide] + [row] * nb, out_specs=row,
        out_shape=jax.ShapeDtypeStruct((rows, d), F32), compiler_params=_params(1),
    )(gates, *ys)


def _gate_merge_bwd_call(gates, ys, dm, *, name):
    rows, d = ys[0].shape
    nb = len(ys)
    tr = _merge_tiles(rows)

    def body(g_ref, *refs):
        y_refs = refs[:nb]
        dm_ref = refs[nb]
        dg_ref = refs[nb + 1]
        dy_refs = refs[nb + 2:]
        dm = dm_ref[...]
        for b in range(nb):
            gt = jax.nn.sigmoid(g_ref[:, b * d:(b + 1) * d])
            dy_refs[b][...] = gt * dm
            dg_ref[:, b * d:(b + 1) * d] = dm * y_refs[b][...] * gt * (1.0 - gt)

    wide = pl.BlockSpec((tr, nb * d), lambda i: (i, 0))
    row = pl.BlockSpec((tr, d), lambda i: (i, 0))
    outs = pl.pallas_call(
        body, name=name, grid=(rows // tr,), in_specs=[wide] + [row] * (nb + 1), out_specs=[wide] + [row] * nb,
        out_shape=[jax.ShapeDtypeStruct((rows, nb * d), F32)] + [jax.ShapeDtypeStruct((rows, d), F32)] * nb,
        compiler_params=_params(1),
    )(gates, *ys, dm)
    return outs[0], tuple(outs[1:])


@functools.partial(jax.custom_vjp, nondiff_argnums=(2,))
def gate_merge(gates, ys, name):
    return _gate_merge_fwd_call(gates, ys, name=name + "_f")


def _gate_merge_fwd(gates, ys, name):
    return _gate_merge_fwd_call(gates, ys, name=name + "_f"), (gates, ys)


def _gate_merge_bwd(name, res, dm):
    gates, ys = res
    return _gate_merge_bwd_call(gates, ys, dm, name=name + "_b")


gate_merge.defvjp(_gate_merge_fwd, _gate_merge_bwd)


def _loss_call(y, target):
    rows, d = y.shape
    tr = _pick(rows, 512, 8)

    def body(y_ref, t_ref, dy_ref, loss_ref):
        @pl.when(pl.program_id(0) == 0)
        def _():
            loss_ref[...] = jnp.zeros_like(loss_ref)

        err = y_ref[...] - t_ref[...]
        dy_ref[...] = err * (1.0 / d)
        loss_ref[...] += jnp.sum(err * err, axis=0, keepdims=True)

    row = pl.BlockSpec((tr, d), lambda i: (i, 0))
    vec = pl.BlockSpec((1, d), lambda i: (0, 0))
    dy, part = pl.pallas_call(
        body, name="loss_head", grid=(rows // tr,), in_specs=[row, row], out_specs=[row, vec],
        out_shape=[jax.ShapeDtypeStruct((rows, d), F32), jax.ShapeDtypeStruct((1, d), F32)],
        compiler_params=_params(1),
    )(y, target)
    return 0.5 * jnp.sum(part) / d, dy


ADAMW_BLOCK_ELEMS = 256 * 1024


def _adamw_call(w, g, m, v, *, name):
    shape = w.shape
    width = shape[-1]
    rows = w.size // width
    w, g, m, v = (a.reshape(rows, width) for a in (w, g, m, v))
    tr = _pick(rows, max(8, ADAMW_BLOCK_ELEMS // width), 8)
    c1 = 1.0 - ADAM_B1 ** ADAM_STEP
    c2 = 1.0 - ADAM_B2 ** ADAM_STEP

    def body(w_ref, g_ref, m_ref, v_ref, d_ref, nm_ref, nv_ref):
        g = g_ref[...]
        m = ADAM_B1 * m_ref[...] + (1.0 - ADAM_B1) * g
        v = ADAM_B2 * v_ref[...] + (1.0 - ADAM_B2) * (g * g)
        m_hat = m / c1
        v_hat = v / c2
        d_ref[...] = -ADAM_LR * (m_hat / (jnp.sqrt(v_hat) + ADAM_EPS) + ADAM_WD * w_ref[...])
        nm_ref[...] = m
        nv_ref[...] = v

    row = pl.BlockSpec((tr, width), lambda i: (i, 0))
    outs = pl.pallas_call(
        body, name=name, grid=(rows // tr,), in_specs=[row] * 4, out_specs=[row] * 3,
        out_shape=[jax.ShapeDtypeStruct((rows, width), F32)] * 3, compiler_params=_params(1),
    )(w, g, m, v)
    return [o.reshape(shape) for o in outs]


def _ordered_sum(y, *, name):
    n, rows, width = y.shape
    tr = _pick(rows, 256, 8)

    def body(y_ref, o_ref):
        acc = y_ref[0]
        for s in range(1, n):
            acc = acc + y_ref[s]
        o_ref[...] = acc

    return pl.pallas_call(
        body, name=name, grid=(rows // tr,), in_specs=[pl.BlockSpec((n, tr, width), lambda i: (0, i, 0))],
        out_specs=pl.BlockSpec((tr, width), lambda i: (i, 0)), out_shape=jax.ShapeDtypeStruct((rows, width), y.dtype),
        compiler_params=_params(1),
    )(y)


_GROUP_FLIPS = {
    "chips": ((1, 0, 0), (0, 1, 0), (1, 1, 0)),
    "cores": ((0, 0, 1),),
    "all": ((0, 0, 1), (0, 1, 0), (0, 1, 1), (1, 0, 0), (1, 0, 1), (1, 1, 0), (1, 1, 1)),
}


def _exchange(x, *, group, broadcast, name):
    flips = _GROUP_FLIPS[group]
    n = len(flips) + 1
    block = x.shape if broadcast else x.shape[1:]
    if not broadcast:
        assert x.shape[0] == n

    def body(x_ref, o_ref, send_sems, recv_sems, local_sem):
        mx, my, mc = lax.axis_index("x"), lax.axis_index("y"), lax.axis_index("c")

        def index(px, py, pc):
            return {"chips": 2 * px + py, "cores": pc, "all": 4 * px + 2 * py + pc}[group]

        def block_for(d):
            return x_ref if broadcast else x_ref.at[d]

        me = index(mx, my, mc)
        mine = pltpu.make_async_copy(block_for(me), o_ref.at[me], local_sem)
        mine.start()
        sends, recvs = [], []
        for k, (fx, fy, fc) in enumerate(flips):
            px = 1 - mx if fx else mx
            py = 1 - my if fy else my
            pc = 1 - mc if fc else mc
            peer = index(px, py, pc)
            sends.append(pltpu.make_async_remote_copy(
                src_ref=block_for(peer), dst_ref=o_ref.at[me], send_sem=send_sems.at[k], recv_sem=recv_sems.at[k],
                device_id=(px, py, pc), device_id_type=MESH_T))
            recvs.append(pltpu.make_async_remote_copy(
                src_ref=block_for(peer), dst_ref=o_ref.at[peer], send_sem=send_sems.at[k], recv_sem=recv_sems.at[k],
                device_id=(px, py, pc), device_id_type=MESH_T))
        for cp in sends:
            cp.start()
        for cp in recvs:
            cp.wait_recv()
        for cp in sends:
            cp.wait_send()
        mine.wait()

    any_spec = pl.BlockSpec(memory_space=pl.ANY)
    return pl.pallas_call(
        body, name=name, in_specs=[any_spec], out_specs=any_spec,
        out_shape=jax.ShapeDtypeStruct((n,) + tuple(block), x.dtype),
        scratch_shapes=[pltpu.SemaphoreType.DMA((n - 1,)), pltpu.SemaphoreType.DMA((n - 1,)), pltpu.SemaphoreType.DMA],
    )(x)


def _sibling():
    return (lax.axis_index("x"), lax.axis_index("y"), 1 - lax.axis_index("c"))


def _run_copies(copies):
    for cp in copies:
        cp.start()
    for cp in copies:
        cp.wait_recv()
    for cp in copies:
        cp.wait_send()


def _dma_call(body, arrays, out_shapes, n_copies, *, name, extra_scratch=()):
    any_spec = pl.BlockSpec(memory_space=pl.ANY)
    return pl.pallas_call(
        body, name=name, in_specs=[any_spec] * len(arrays), out_specs=[any_spec] * len(out_shapes),
        out_shape=out_shapes,
        scratch_shapes=[pltpu.SemaphoreType.DMA((n_copies,)), pltpu.SemaphoreType.DMA((n_copies,))]
        + list(extra_scratch),
    )(*arrays)


def _gather_shards(shards, *, name):
    flips = _GROUP_FLIPS["chips"]
    n = len(shards)
    n_copies = n * len(flips)

    def body(*refs):
        ins, outs = refs[:n], refs[n:2 * n]
        send_sems, recv_sems, pass_send_sems, pass_recv_sems, local_sems = refs[2 * n:]
        mx, my, mc = lax.axis_index("x"), lax.axis_index("y"), lax.axis_index("c")
        me = 2 * mx + my
        local = [pltpu.make_async_copy(ins[t], outs[t].at[me], local_sems.at[t]) for t in range(n)]
        for cp in local:
            cp.start()
        sends, arrivals, passes, passed = [], [], [], []
        for t in range(n):
            half = shards[t].shape[0] // 2
            mine = pl.ds(mc * half, half)
            theirs = pl.ds((1 - mc) * half, half)
            for k, (fx, fy, _) in enumerate(flips):
                px = 1 - mx if fx else mx
                py = 1 - my if fy else my
                peer = 2 * px + py
                idx = t * len(flips) + k
                ici = dict(send_sem=send_sems.at[idx], recv_sem=recv_sems.at[idx], device_id=(px, py, mc),
                           device_id_type=MESH_T)
                d2d = dict(send_sem=pass_send_sems.at[idx], recv_sem=pass_recv_sems.at[idx], device_id=_sibling(),
                           device_id_type=MESH_T)
                sends.append(pltpu.make_async_remote_copy(
                    src_ref=ins[t].at[mine], dst_ref=outs[t].at[me, mine], **ici))
                arrivals.append(pltpu.make_async_remote_copy(
                    src_ref=ins[t].at[mine], dst_ref=outs[t].at[peer, mine], **ici))
                passes.append(pltpu.make_async_remote_copy(
                    src_ref=outs[t].at[peer, mine], dst_ref=outs[t].at[peer, mine], **d2d))
                passed.append(pltpu.make_async_remote_copy(
                    src_ref=outs[t].at[peer, theirs], dst_ref=outs[t].at[peer, theirs], **d2d))
        for cp in sends:
            cp.start()
        for arrived, onward in zip(arrivals, passes):
            arrived.wait_recv()
            onward.start()
        for cp in passed:
            cp.wait_recv()
        for cp in sends + passes:
            cp.wait_send()
        for cp in local:
            cp.wait()

    return _dma_call(body, shards, [jax.ShapeDtypeStruct((4,) + a.shape, a.dtype) for a in shards], n_copies,
                     name=name, extra_scratch=[pltpu.SemaphoreType.DMA((n_copies,)), pltpu.SemaphoreType.DMA((n_copies,)),
                                               pltpu.SemaphoreType.DMA((n,))])


def _pair_send(gs, *, name):
    n = len(gs)

    def body(*refs):
        ins, outs = refs[:n], refs[n:2 * n]
        send_sems, recv_sems = refs[2 * n:]
        copies = []
        for t in range(n):
            half = gs[t].shape[1] // 2
            theirs = (1 - lax.axis_index("c")) * half
            for s in range(4):
                copies.append(pltpu.make_async_remote_copy(
                    src_ref=ins[t].at[s, pl.ds(theirs, half), :], dst_ref=outs[t].at[s],
                    send_sem=send_sems.at[4 * t + s], recv_sem=recv_sems.at[4 * t + s],
                    device_id=_sibling(), device_id_type=MESH_T))
        _run_copies(copies)

    return _dma_call(body, gs, [jax.ShapeDtypeStruct((4, a.shape[1] // 2, a.shape[2]), a.dtype) for a in gs], 4 * n,
                     name=name)


def _quad_send(ps, *, name):
    flips = _GROUP_FLIPS["chips"]
    n = len(ps)

    def body(*refs):
        ins, outs = refs[:n], refs[n:2 * n]
        send_sems, recv_sems = refs[2 * n:]
        mx, my, mc = lax.axis_index("x"), lax.axis_index("y"), lax.axis_index("c")
        copies = []
        for t in range(n):
            for k, (fx, fy, _) in enumerate(flips):
                px = 1 - mx if fx else mx
                py = 1 - my if fy else my
                copies.append(pltpu.make_async_remote_copy(
                    src_ref=ins[t].at[2 * px + py], dst_ref=outs[t].at[k], send_sem=send_sems.at[3 * t + k],
                    recv_sem=recv_sems.at[3 * t + k], device_id=(px, py, mc), device_id_type=MESH_T))
        _run_copies(copies)

    return _dma_call(body, ps, [jax.ShapeDtypeStruct((3,) + a.shape[1:], a.dtype) for a in ps], 3 * n, name=name)


def _sibling_send(ms, *, name):
    n = len(ms)

    def body(*refs):
        ins, outs = refs[:n], refs[n:2 * n]
        send_sems, recv_sems = refs[2 * n:]
        _run_copies([pltpu.make_async_remote_copy(
            src_ref=ins[t], dst_ref=outs[t], send_sem=send_sems.at[t], recv_sem=recv_sems.at[t],
            device_id=_sibling(), device_id_type=MESH_T) for t in range(n)])

    return _dma_call(body, ms, [jax.ShapeDtypeStruct(a.shape, a.dtype) for a in ms], n, name=name)


SUM_BLOCK_ELEMS = 256 * 1024


def _sum_tile(rows, width):
    return _pick(rows, max(16, SUM_BLOCK_ELEMS // width), 16)


def _pair_sum(g, got, core, *, name):
    _, rows, width = g.shape
    half = rows // 2
    tr = _sum_tile(half, width)
    tiles = half // tr

    def body(core_ref, g_ref, got_ref, o_ref):
        o_ref[...] = (g_ref[...] + got_ref[...]).astype(BF16)

    blk = pl.BlockSpec((1, tr, width), lambda s, i, core_ref: (s, i, 0))
    return pl.pallas_call(
        body, name=name,
        grid_spec=pltpu.PrefetchScalarGridSpec(
            num_scalar_prefetch=1, grid=(4, tiles),
            in_specs=[pl.BlockSpec((1, tr, width), lambda s, i, core_ref: (s, core_ref[0] * tiles + i, 0)), blk],
            out_specs=blk),
        out_shape=jax.ShapeDtypeStruct((4, half, width), BF16), compiler_params=_params(2),
    )(core, g, got)


def _quad_sum(pair, others, chip, *, name):
    _, rows, width = pair.shape
    tr = _sum_tile(rows, width)

    def body(chip_ref, p_ref, o3_ref, o_ref):
        acc = p_ref[0].astype(F32)
        for k in range(3):
            acc = acc + o3_ref[k].astype(F32)
        o_ref[...] = acc

    return pl.pallas_call(
        body, name=name,
        grid_spec=pltpu.PrefetchScalarGridSpec(
            num_scalar_prefetch=1, grid=(rows // tr,),
            in_specs=[pl.BlockSpec((1, tr, width), lambda i, chip_ref: (chip_ref[0], i, 0)),
                      pl.BlockSpec((3, tr, width), lambda i, chip_ref: (0, i, 0))],
            out_specs=pl.BlockSpec((tr, width), lambda i, chip_ref: (i, 0))),
        out_shape=jax.ShapeDtypeStruct((rows, width), F32), compiler_params=_params(1),
    )(chip, pair, others)


def _reduce_scatter(gs, *, tag):
    mc = lax.axis_index("c")
    core = mc.reshape(1).astype(jnp.int32)
    chip = (2 * lax.axis_index("x") + lax.axis_index("y")).reshape(1).astype(jnp.int32)
    got = _pair_send(gs, name=tag + "_pair")
    pairs = [_pair_sum(g, r, core, name=tag + "_pair_sum") for g, r in zip(gs, got)]
    others = _quad_send(pairs, name=tag + "_quad")
    mine = [_quad_sum(p, o, chip, name=tag + "_quad_sum") for p, o in zip(pairs, others)]
    theirs = _sibling_send(mine, name=tag + "_share")
    return [jnp.where(mc == 0, jnp.concatenate([m, t], axis=0), jnp.concatenate([t, m], axis=0))
            for m, t in zip(mine, theirs)]


def _round_up(n, q):
    return -(-n // q) * q


def _pack_rows(flat, quantum_rows):
    n = flat.shape[-1]
    total = _round_up(n, PACK_W * quantum_rows)
    pad = [(0, 0)] * (flat.ndim - 1) + [(0, total - n)]
    return jnp.pad(flat, pad).reshape(flat.shape[:-1] + (total // PACK_W, PACK_W))


def _split_shards(full, axis):
    if axis == 0:
        return full.reshape((4, full.shape[0] // 4, full.shape[1]))
    width = full.shape[1] // 4
    return jnp.stack([full[:, s * width:(s + 1) * width] for s in range(4)])


def _join_shards(blocks, axis):
    if axis == 0:
        return blocks.reshape((4 * blocks.shape[1], blocks.shape[2]))
    return jnp.concatenate([blocks[s] for s in range(4)], axis=1)


def _pack_small(tree, names):
    return _pack_rows(jnp.concatenate([tree[name].reshape(-1) for name in names]), 8)


def _unpack_small(packed, shapes, names):
    lead = packed.shape[:-2]
    flat = packed.reshape(lead + (-1,))
    out, off = {}, 0
    for name in names:
        n = int(np.prod(shapes[name]))
        out[name] = flat[..., off:off + n].reshape(lead + tuple(shapes[name]))
        off += n
    return out


def _rope(t, cos, sin):
    half = QK_ROPE // 2
    t1, t2 = t[..., :half], t[..., half:]
    return jnp.concatenate([t1 * cos - t2 * sin, t2 * cos + t1 * sin], axis=-1)


def _pad_in_cols(a):
    z = jnp.zeros(a.shape[:-1] + (IN_PAD,), a.dtype)
    return jnp.concatenate([a[..., :IN_PAD_AT], z, a[..., IN_PAD_AT:]], axis=-1)


def _pool_constants(seq):
    taps = max(POOL_WINDOWS)
    win = jnp.repeat(jnp.asarray(POOL_WINDOWS, jnp.int32), 512 // len(POOL_WINDOWS))
    lag = taps - 1 - jnp.arange(taps, dtype=jnp.int32)
    mask = (lag[:, None] < win[None, :]).astype(F32)
    cnt = jnp.minimum(jnp.arange(seq, dtype=jnp.int32)[:, None] + 1, win[None, :]).astype(F32)
    return mask, 1.0 / cnt


def _block_diag(w):
    g, n, _ = w.shape
    out = jnp.zeros((g * n, g * n), w.dtype)
    for i in range(g):
        out = lax.dynamic_update_slice(out, w[i], (i * n, i * n))
    return out


def _layer(x, c_act, p, wb, cos, sin, pool_mask, pool_inv_cnt, tag):
    seq = x.shape[0]
    d = D_MODEL
    mod = linear_bias(c_act, p["w_ada"], wb["w_ada"], p["b_ada"], tag + "ada")[0]
    sh1, sc1, g1, sh2, sc2, g2 = [mod[i * d:(i + 1) * d][None, :] for i in range(6)]

    h = x * (1.0 + sc1) + sh1
    w_slot, w_in, b_in = _pad_in_cols(p["w_in"]), _pad_in_cols(wb["w_in"]), _pad_in_cols(p["b_in"])
    proj = linear_bias(h, w_slot[:, :IN_GATES_AT], w_in[:, :IN_GATES_AT], b_in[:IN_GATES_AT], tag + "in")
    gates = linear_bias(h, w_slot[:, IN_GATES_AT:], w_in[:, IN_GATES_AT:], b_in[IN_GATES_AT:], tag + "in_gates")
    conv_a, conv_b, sc_bg, sc_cg, sc_x, q_lat, kv_lat, k_rope, pool_u = split_cols(proj, IN_SPLITS)

    ya = conv_a * jax.nn.sigmoid(conv_b)
    ya = dwconv(ya, p["conv_dw"], tag + "convA")
    ya = jax.nn.silu(layer_norm(ya, p["conv_ln_g"], p["conv_ln_b"], tag + "convA_ln"))
    ya = linear(ya, p["w_conv_out"], wb["w_conv_out"], tag + "convA_out")

    yb = linear(sc_bg * dwconv(sc_cg * sc_x, p["sc_dw"], tag + "sc"), p["w_sc_out"], wb["w_sc_out"],
                tag + "sc_out")

    q = linear(rms_norm(q_lat, p["q_norm_g"], tag + "q_rms"), p["w_uq"], wb["w_uq"], tag + "uq")
    q = q.reshape(seq, N_HEADS, QK_NOPE + QK_ROPE)
    kv = linear(rms_norm(kv_lat, p["kv_norm_g"], tag + "kv_rms"), p["w_ukv"], wb["w_ukv"], tag + "ukv")
    kv = kv.reshape(seq, N_HEADS, QK_NOPE + V_DIM)
    q_rope = _rope(q[..., QK_NOPE:], cos[:, None, :], sin[:, None, :])
    k_rope_r = jnp.broadcast_to(_rope(k_rope, cos, sin)[:, None, :], (seq, N_HEADS, QK_ROPE))
    zq = jnp.zeros((seq, N_HEADS, HEAD_PAD - QK_NOPE - QK_ROPE), F32)
    zv = jnp.zeros((seq, N_HEADS, HEAD_PAD - V_DIM), F32).at[:, :, DEN_COL - V_DIM].set(1.0)
    qh = jnp.concatenate([q[..., :QK_NOPE], q_rope, zq], axis=-1).transpose(1, 0, 2)
    kh = jnp.concatenate([kv[..., :QK_NOPE], k_rope_r, zq], axis=-1).transpose(1, 0, 2)
    vh = jnp.concatenate([kv[..., QK_NOPE:], zv], axis=-1).transpose(1, 0, 2)
    att = attention(qh, kh, vh, tag + "attn")
    att = att[:, :, :V_DIM].transpose(1, 0, 2).reshape(seq, N_HEADS * V_DIM)
    yc = linear(att, p["w_mla_out"], wb["w_mla_out"], tag + "mla_out")

    pd = dwconv(pool_u, pool_mask, tag + "pool") * pool_inv_cnt - pool_u
    w_mix = _block_diag(p["w_pool"])
    yd = linear(pd, w_mix, w_mix, tag + "pool_mix") * p["pool_scale"][None, :]
    yd = linear(yd, p["w_pool_out"], wb["w_pool_out"], tag + "pool_out")

    merged = gate_merge(gates, (ya, yb, yc, yd), tag + "merge")
    mix = linear(merged, p["w_o"], wb["w_o"], tag + "o")
    x = layer_norm(ALPHA * x + (1.0 + g1) * mix, p["ln1_g"], p["ln1_b"], tag + "ln1")

    h = x * (1.0 + sc2) + sh2
    d_ff = p["w_up"].shape[1] // 2
    val = dwconv(linear(h, p["w_up"][:, :d_ff], wb["w_up"][:, :d_ff], tag + "up"), p["ffn_dw"][:, :d_ff],
                 tag + "ffn_conv")
    gate = dwconv(linear(h, p["w_up"][:, d_ff:], wb["w_up"][:, d_ff:], tag + "up"), p["ffn_dw"][:, d_ff:],
                  tag + "ffn_conv")
    ffn = linear(jax.nn.silu(gate) * val, p["w_down"], wb["w_down"], tag + "down")
    return layer_norm(ALPHA * x + (1.0 + g2) * ffn, p["ln2_g"], p["ln2_b"], tag + "ln2")


def _forward(x, layers, gathered, c_act, cos, sin, pool_mask, pool_inv_cnt):
    for p, wb in zip(layers, gathered):
        x = _layer(x, c_act, p, wb, cos, sin, pool_mask, pool_inv_cnt, "")
    return x


def kernel(x, c, positions, w_ada, b_ada, w_in, b_in, conv_dw, conv_ln_g, conv_ln_b, w_conv_out, sc_dw, w_sc_out, q_norm_g, w_uq, kv_norm_g, w_ukv, w_mla_out, w_pool, pool_scale, w_pool_out, w_o, ln1_g, ln1_b, w_up, ffn_dw, w_down, ln2_g, ln2_b, loss_target, m_w_ada, m_b_ada, m_w_in, m_b_in, m_conv_dw, m_conv_ln_g, m_conv_ln_b, m_w_conv_out, m_sc_dw, m_w_sc_out, m_q_norm_g, m_w_uq, m_kv_norm_g, m_w_ukv, m_w_mla_out, m_w_pool, m_pool_scale, m_w_pool_out, m_w_o, m_ln1_g, m_ln1_b, m_w_up, m_ffn_dw, m_w_down, m_ln2_g, m_ln2_b, v_w_ada, v_b_ada, v_w_in, v_b_in, v_conv_dw, v_conv_ln_g, v_conv_ln_b, v_w_conv_out, v_sc_dw, v_w_sc_out, v_q_norm_g, v_w_uq, v_kv_norm_g, v_w_ukv, v_w_mla_out, v_w_pool, v_pool_scale, v_w_pool_out, v_w_o, v_ln1_g, v_ln1_b, v_w_up, v_ffn_dw, v_w_down, v_ln2_g, v_ln2_b):
    given = dict(locals())
    weights = {n: given[n] for n in WEIGHT_ORDER}
    mom_m = {n: given["m_" + n] for n in WEIGHT_ORDER}
    mom_v = {n: given["v_" + n] for n in WEIGHT_ORDER}
    depth = w_ada.shape[0]
    seq = x.shape[1]
    big = [(s[0], s[2]) for s in SHARDED if s[3]]
    tiny = [s[0] for s in SHARDED if not s[3]]
    chip = 2 * lax.axis_index("x") + lax.axis_index("y")

    tiny_shapes = {n: weights[n].shape for n in tiny}
    tiny_all = _exchange(_pack_small(weights, tiny), group="chips", broadcast=True, name="gather_taps")
    tiny_full = {n: jnp.concatenate([a[s] for s in range(4)], axis=-1)
                 for n, a in _unpack_small(tiny_all, tiny_shapes, tiny).items()}
    layers, gathered = [], []
    for l in range(depth):
        blocks = _gather_shards([weights[n][l].astype(BF16) for n, _ in big], name="gather_weights")
        gathered.append({n: _join_shards(a, axis) for (n, axis), a in zip(big, blocks)})
        p = {n: jnp.zeros(gathered[l][n].shape, F32) for n, _ in big}
        for n in tiny:
            p[n] = tiny_full[n][l]
        for n in REPLICATED:
            p[n] = weights[n][l]
        layers.append(p)

    inv = 1.0 / (ROPE_THETA ** (jnp.arange(0, QK_ROPE, 2, dtype=F32) / QK_ROPE))
    ang = positions[0].astype(F32)[:, None] * inv
    cos, sin = jnp.cos(ang), jnp.sin(ang)
    c_act = jnp.pad(jax.nn.silu(c), ((0, 15), (0, 0)))
    pool_mask, pool_inv_cnt = _pool_constants(seq)

    y, vjp_fn = jax.vjp(lambda xx, ll: _forward(xx, ll, gathered, c_act, cos, sin, pool_mask, pool_inv_cnt),
                        x[0], layers)
    loss_local, dy = _loss_call(y, loss_target[0])
    grad_x, grad_layers = vjp_fn(dy)
    loss = lax.psum(loss_local, ("x", "y", "c"))

    outs = {"grad": {}, "delta": {}, "m": {}, "v": {}}
    reduced = [_reduce_scatter([_split_shards(grad_layers[l][n], axis) for n, axis in big], tag="rs")
               for l in range(depth)]
    for t, (n, _) in enumerate(big):
        outs["grad"][n] = jnp.stack([reduced[l][t] for l in range(depth)])
    small = list(REPLICATED) + tiny
    small_local = {n: jnp.stack([grad_layers[l][n] for l in range(depth)]) for n in small}
    small_shapes = {n: small_local[n].shape for n in small}
    small_all = _exchange(_pack_small(small_local, small), group="all", broadcast=True, name="gather_small_grads")
    small_sum = _unpack_small(_ordered_sum(small_all, name="small_grads_sum"), small_shapes, small)
    for n in REPLICATED:
        outs["grad"][n] = small_sum[n]
    for n in tiny:
        width = weights[n].shape[-1]
        outs["grad"][n] = lax.dynamic_slice_in_dim(small_sum[n], chip * width, width, axis=2)

    for n in WEIGHT_ORDER:
        outs["delta"][n], outs["m"][n], outs["v"][n] = _adamw_call(
            weights[n], outs["grad"][n], mom_m[n], mom_v[n], name="adamw_" + n)

    result = [loss, grad_x[None]]
    for key in ("grad", "delta", "m", "v"):
        result.extend(outs[key][n] for n in WEIGHT_ORDER)
    return tuple(result)
```

```python
import functools
import math

import jax
import jax.numpy as jnp
import numpy as np
from jax import lax
from jax.experimental import pallas as pl
from jax.experimental.pallas import tpu as pltpu

F32 = jnp.float32
BF16 = jnp.bfloat16
MXU = jnp.bfloat16

D_MODEL = 1024
N_HEADS = 8
QK_NOPE = 64
QK_ROPE = 32
V_DIM = 64
HEAD_PAD = 128
DEN_COL = V_DIM
ROPE_THETA = 10000.0
POOL_WINDOWS = (2, 4, 8, 16)
LN_EPS = 1e-5
RMS_EPS = 1e-6
DEPTH_FOR_DEEPNORM = 4
ALPHA = (2.0 * DEPTH_FOR_DEEPNORM) ** 0.25
ATTN_SCALE = (QK_NOPE + QK_ROPE) ** -0.5
LOG2_E = math.log2(math.e)
IN_COLS = 7584
IN_PAD_AT = 2976
IN_PAD = 96
IN_SPLITS = ((0, 512), (512, 1024), (1024, 1536), (1536, 2048), (2048, 2560), (2560, 2816), (2816, 2944),
             (2944, 2976), (3072, 3584))
IN_GATES_AT = 3584
ADAM_LR, ADAM_B1, ADAM_B2, ADAM_EPS, ADAM_WD, ADAM_STEP = 0.001, 0.9, 0.999, 1e-08, 0.01, 10

VMEM_LIMIT = 48 * 1024 * 1024
LANE = 128
CONV_HALO = 32
PACK_W = 512
MESH_T = pl.DeviceIdType.MESH

SHARDED = (
    ("w_ada", (1024, 6144), 1, True),
    ("w_in", (1024, IN_COLS), 1, True),
    ("conv_dw", (31, 512), 1, False),
    ("w_conv_out", (512, 1024), 1, True),
    ("sc_dw", (3, 512), 1, False),
    ("w_sc_out", (512, 1024), 1, True),
    ("w_uq", (256, 768), 1, True),
    ("w_ukv", (128, 1024), 1, True),
    ("w_mla_out", (512, 1024), 1, True),
    ("w_pool_out", (512, 1024), 1, True),
    ("w_o", (1024, 1024), 0, True),
    ("w_up", (1024, 5632), 1, True),
    ("ffn_dw", (3, 5632), 1, False),
    ("w_down", (2816, 1024), 0, True),
)
REPLICATED = ("b_ada", "b_in", "conv_ln_g", "conv_ln_b", "q_norm_g", "kv_norm_g", "w_pool", "pool_scale",
              "ln1_g", "ln1_b", "ln2_g", "ln2_b")
WEIGHT_ORDER = ("w_ada", "b_ada", "w_in", "b_in", "conv_dw", "conv_ln_g", "conv_ln_b", "w_conv_out", "sc_dw",
                "w_sc_out", "q_norm_g", "w_uq", "kv_norm_g", "w_ukv", "w_mla_out", "w_pool", "pool_scale",
                "w_pool_out", "w_o", "ln1_g", "ln1_b", "w_up", "ffn_dw", "w_down", "ln2_g", "ln2_b")


def _pick(dim, cap, quantum=LANE):
    best = None
    t = quantum
    while t <= min(dim, cap):
        if dim % t == 0:
            best = t
        t += quantum
    return dim if best is None else best


def _params(n_grid):
    return pltpu.CompilerParams(dimension_semantics=("arbitrary",) * n_grid, vmem_limit_bytes=VMEM_LIMIT)


def _mm_tile(dim):
    t = _pick(dim, 1024)
    return t if t >= 512 or t == dim else _pick(dim, 1536)


def _mm(a, b, *, ta=False, tb=False, bias=None, name):
    if ta:
        k_dim, m_dim = a.shape
    else:
        m_dim, k_dim = a.shape
    if tb:
        n_dim, k2 = b.shape
    else:
        k2, n_dim = b.shape
    assert k_dim == k2, (a.shape, b.shape, ta, tb)
    tm, tn, tk = _mm_tile(m_dim), _mm_tile(n_dim), _mm_tile(k_dim)
    nk = k_dim // tk
    a_spec = pl.BlockSpec((tk, tm), lambda i, j, k: (k, i)) if ta else pl.BlockSpec((tm, tk), lambda i, j, k: (i, k))
    b_spec = pl.BlockSpec((tn, tk), lambda i, j, k: (j, k)) if tb else pl.BlockSpec((tk, tn), lambda i, j, k: (k, j))
    dims = (((0 if ta else 1,), (1 if tb else 0,)), ((), ()))
    has_bias = bias is not None

    def body(*refs):
        a_ref, b_ref = refs[:2]
        bias_ref = refs[2] if has_bias else None
        o_ref = refs[3] if has_bias else refs[2]
        part = lax.dot_general(a_ref[...].astype(MXU), b_ref[...].astype(MXU), dims, preferred_element_type=F32)
        if nk == 1:
            o_ref[...] = part + bias_ref[...] if has_bias else part
            return
        acc_ref = refs[-1]
        k = pl.program_id(2)

        @pl.when(k == 0)
        def _():
            acc_ref[...] = part

        @pl.when(k > 0)
        def _():
            acc_ref[...] += part

        @pl.when(k == nk - 1)
        def _():
            out = acc_ref[...]
            if has_bias:
                out = out + bias_ref[...]
            o_ref[...] = out

    in_specs = [a_spec, b_spec]
    args = [a, b]
    if has_bias:
        in_specs.append(pl.BlockSpec((1, tn), lambda i, j, k: (0, j)))
        args.append(bias.reshape(1, n_dim))
    return pl.pallas_call(
        body, name=name, grid=(m_dim // tm, n_dim // tn, nk), in_specs=in_specs,
        out_specs=pl.BlockSpec((tm, tn), lambda i, j, k: (i, j)),
        out_shape=jax.ShapeDtypeStruct((m_dim, n_dim), F32),
        scratch_shapes=[] if nk == 1 else [pltpu.VMEM((tm, tn), F32)], compiler_params=_params(3),
    )(*args)


COLSUM_BLOCK_ELEMS = 1024 * 1024


def _colsum(x, *, name):
    rows, n = x.shape
    tr = _pick(rows, max(8, COLSUM_BLOCK_ELEMS // n), 8)

    def body(x_ref, o_ref):
        @pl.when(pl.program_id(0) == 0)
        def _():
            o_ref[...] = jnp.zeros_like(o_ref)

        o_ref[...] += jnp.sum(x_ref[...], axis=0, keepdims=True)

    out = pl.pallas_call(
        body, name=name, grid=(rows // tr,), in_specs=[pl.BlockSpec((tr, n), lambda i: (i, 0))],
        out_specs=pl.BlockSpec((1, n), lambda i: (0, 0)), out_shape=jax.ShapeDtypeStruct((1, n), F32),
        compiler_params=_params(1),
    )(x)
    return out[0]


@functools.partial(jax.custom_vjp, nondiff_argnums=(3,))
def linear(x, slot, w, name):
    return _mm(x, w, name=name + "_f")


def _linear_fwd(x, slot, w, name):
    return _mm(x, w, name=name + "_f"), (x, w)


def _weight_grad(x, dy, name):
    return _mm(x, dy, ta=True, name=name)


def _linear_bwd(name, res, dy):
    x, w = res
    return _mm(dy, w, tb=True, name=name + "_dx"), _weight_grad(x, dy, name + "_dw"), jnp.zeros_like(w)


linear.defvjp(_linear_fwd, _linear_bwd)


@functools.partial(jax.custom_vjp, nondiff_argnums=(4,))
def linear_bias(x, slot, w, b, name):
    return _mm(x, w, bias=b, name=name + "_f")


def _linear_bias_fwd(x, slot, w, b, name):
    return _mm(x, w, bias=b, name=name + "_f"), (x, w)


def _linear_bias_bwd(name, res, dy):
    x, w = res
    return (_mm(dy, w, tb=True, name=name + "_dx"), _weight_grad(x, dy, name + "_dw"), jnp.zeros_like(w),
            _colsum(dy, name=name + "_db"))


linear_bias.defvjp(_linear_bias_fwd, _linear_bias_bwd)


@functools.partial(jax.custom_vjp, nondiff_argnums=(1,))
def split_cols(x, bounds):
    return tuple(x[:, a:b] for a, b in bounds)


def _split_cols_fwd(x, bounds):
    return split_cols(x, bounds), x.shape[1]


def _split_cols_bwd(bounds, width, cts):
    rows = cts[0].shape[0]
    parts, at = [], 0
    for (a, b), ct in zip(bounds, cts):
        if a > at:
            parts.append(jnp.zeros((rows, a - at), ct.dtype))
        parts.append(ct)
        at = b
    if at < width:
        parts.append(jnp.zeros((rows, width - at), cts[0].dtype))
    return (jnp.concatenate(parts, axis=1),)


split_cols.defvjp(_split_cols_fwd, _split_cols_bwd)


def _norm_stats(x, center, eps):
    if center:
        mu = jnp.mean(x, axis=-1, keepdims=True)
        xc = x - mu
    else:
        xc = x
    rstd = lax.rsqrt(jnp.mean(xc * xc, axis=-1, keepdims=True) + eps)
    return xc * rstd


def _norm_fwd_call(x, g, b, *, center, name):
    rows, d = x.shape
    tr = _pick(rows, 512, 8)
    eps = LN_EPS if center else RMS_EPS

    def body(x_ref, g_ref, b_ref, o_ref):
        xhat = _norm_stats(x_ref[...], center, eps)
        y = xhat * g_ref[...]
        if center:
            y = y + b_ref[...]
        o_ref[...] = y

    vec = pl.BlockSpec((1, d), lambda i: (0, 0))
    return pl.pallas_call(
        body, name=name, grid=(rows // tr,), in_specs=[pl.BlockSpec((tr, d), lambda i: (i, 0)), vec, vec],
        out_specs=pl.BlockSpec((tr, d), lambda i: (i, 0)), out_shape=jax.ShapeDtypeStruct((rows, d), F32),
        compiler_params=_params(1),
    )(x, g.reshape(1, d), b.reshape(1, d))


def _norm_bwd_call(x, g, dy, *, center, name):
    rows, d = x.shape
    tr = _pick(rows, 512, 8)
    eps = LN_EPS if center else RMS_EPS

    def body(x_ref, g_ref, dy_ref, dx_ref, dg_ref, db_ref):
        @pl.when(pl.program_id(0) == 0)
        def _():
            dg_ref[...] = jnp.zeros_like(dg_ref)
            db_ref[...] = jnp.zeros_like(db_ref)

        x = x_ref[...]
        dy = dy_ref[...]
        if center:
            mu = jnp.mean(x, axis=-1, keepdims=True)
            xc = x - mu
        else:
            xc = x
        rstd = lax.rsqrt(jnp.mean(xc * xc, axis=-1, keepdims=True) + eps)
        xhat = xc * rstd
        dyg = dy * g_ref[...]
        proj = jnp.mean(dyg * xhat, axis=-1, keepdims=True)
        dx = dyg - xhat * proj
        if center:
            dx = dx - jnp.mean(dyg, axis=-1, keepdims=True)
        dx_ref[...] = dx * rstd
        dg_ref[...] += jnp.sum(dy * xhat, axis=0, keepdims=True)
        db_ref[...] += jnp.sum(dy, axis=0, keepdims=True)

    vec = pl.BlockSpec((1, d), lambda i: (0, 0))
    row = pl.BlockSpec((tr, d), lambda i: (i, 0))
    dx, dg, db = pl.pallas_call(
        body, name=name, grid=(rows // tr,), in_specs=[row, vec, row], out_specs=[row, vec, vec],
        out_shape=[jax.ShapeDtypeStruct((rows, d), F32), jax.ShapeDtypeStruct((1, d), F32),
                   jax.ShapeDtypeStruct((1, d), F32)],
        compiler_params=_params(1),
    )(x, g.reshape(1, d), dy)
    return dx, dg[0], db[0]


@functools.partial(jax.custom_vjp, nondiff_argnums=(3,))
def layer_norm(x, g, b, name):
    return _norm_fwd_call(x, g, b, center=True, name=name + "_f")


def _layer_norm_fwd(x, g, b, name):
    return _norm_fwd_call(x, g, b, center=True, name=name + "_f"), (x, g)


def _layer_norm_bwd(name, res, dy):
    x, g = res
    return _norm_bwd_call(x, g, dy, center=True, name=name + "_b")


layer_norm.defvjp(_layer_norm_fwd, _layer_norm_bwd)


@functools.partial(jax.custom_vjp, nondiff_argnums=(2,))
def rms_norm(x, g, name):
    return _norm_fwd_call(x, g, jnp.zeros_like(g), center=False, name=name + "_f")


def _rms_norm_fwd(x, g, name):
    return _norm_fwd_call(x, g, jnp.zeros_like(g), center=False, name=name + "_f"), (x, g)


def _rms_norm_bwd(name, res, dy):
    x, g = res
    dx, dg, _ = _norm_bwd_call(x, g, dy, center=False, name=name + "_b")
    return dx, dg


rms_norm.defvjp(_rms_norm_fwd, _rms_norm_bwd)


def _conv_tiles(rows, ch):
    tr = _pick(rows, 512, CONV_HALO)
    assert tr >= CONV_HALO and rows % tr == 0
    return tr, _mm_tile(ch)


def _pad_taps(w):
    k = w.shape[0]
    kp = -(-k // 8) * 8
    return jnp.pad(w, ((0, kp - k), (0, 0))), k, kp


def _dwconv_fwd_call(x, w, *, name):
    rows, ch = x.shape
    tr, tc = _conv_tiles(rows, ch)
    wp, taps, kp = _pad_taps(w)
    assert taps - 1 <= CONV_HALO
    halo_per_tile = tr // CONV_HALO

    def body(x_ref, xprev_ref, w_ref, o_ref):
        i = pl.program_id(1)
        halo = xprev_ref[...]
        halo = jnp.where(i > 0, halo, jnp.zeros_like(halo))
        xx = jnp.concatenate([halo, x_ref[...]], axis=0)
        acc = jnp.zeros((tr, tc), F32)
        for k in range(taps):
            shift = taps - 1 - k
            term = xx if shift == 0 else pltpu.roll(xx, shift, 0)
            acc = acc + w_ref[k:k + 1, :] * term[CONV_HALO:, :]
        o_ref[...] = acc

    return pl.pallas_call(
        body, name=name, grid=(ch // tc, rows // tr),
        in_specs=[pl.BlockSpec((tr, tc), lambda j, i: (i, j)),
                  pl.BlockSpec((CONV_HALO, tc), lambda j, i: (jnp.maximum(i * halo_per_tile - 1, 0), j)),
                  pl.BlockSpec((kp, tc), lambda j, i: (0, j))],
        out_specs=pl.BlockSpec((tr, tc), lambda j, i: (i, j)), out_shape=jax.ShapeDtypeStruct((rows, ch), F32),
        compiler_params=_params(2),
    )(x, x, wp)


def _dwconv_bwd_call(x, w, dy, *, name):
    rows, ch = x.shape
    tr, tc = _conv_tiles(rows, ch)
    wp, taps, kp = _pad_taps(w)
    n_row_tiles = rows // tr
    halo_per_tile = tr // CONV_HALO
    n_halo_blocks = rows // CONV_HALO
    ext = tr + CONV_HALO

    def body(x_ref, xprev_ref, dy_ref, dynext_ref, w_ref, dx_ref, dw_ref):
        i = pl.program_id(1)

        @pl.when(i == 0)
        def _():
            dw_ref[...] = jnp.zeros_like(dw_ref)

        halo = xprev_ref[...]
        halo = jnp.where(i > 0, halo, jnp.zeros_like(halo))
        xx = jnp.concatenate([halo, x_ref[...]], axis=0)
        dy = dy_ref[...]
        ahead = dynext_ref[...]
        ahead = jnp.where(i < n_row_tiles - 1, ahead, jnp.zeros_like(ahead))
        yy = jnp.concatenate([dy, ahead], axis=0)
        dx = jnp.zeros((tr, tc), F32)
        for k in range(taps):
            shift = taps - 1 - k
            fwd = yy if shift == 0 else pltpu.roll(yy, ext - shift, 0)
            dx = dx + w_ref[k:k + 1, :] * fwd[:tr, :]
            back = xx if shift == 0 else pltpu.roll(xx, shift, 0)
            dw_ref[k:k + 1, :] += jnp.sum(dy * back[CONV_HALO:, :], axis=0, keepdims=True)
        dx_ref[...] = dx

    cur = pl.BlockSpec((tr, tc), lambda j, i: (i, j))
    dx, dw = pl.pallas_call(
        body, name=name, grid=(ch // tc, n_row_tiles),
        in_specs=[cur, pl.BlockSpec((CONV_HALO, tc), lambda j, i: (jnp.maximum(i * halo_per_tile - 1, 0), j)), cur,
                  pl.BlockSpec((CONV_HALO, tc),
                               lambda j, i: (jnp.minimum((i + 1) * halo_per_tile, n_halo_blocks - 1), j)),
                  pl.BlockSpec((kp, tc), lambda j, i: (0, j))],
        out_specs=[cur, pl.BlockSpec((kp, tc), lambda j, i: (0, j))],
        out_shape=[jax.ShapeDtypeStruct((rows, ch), F32), jax.ShapeDtypeStruct((kp, ch), F32)],
        compiler_params=_params(2),
    )(x, x, dy, dy, wp)
    return dx, dw[:taps]


@functools.partial(jax.custom_vjp, nondiff_argnums=(2,))
def dwconv(x, w, name):
    return _dwconv_fwd_call(x, w, name=name + "_f")


def _dwconv_fwd(x, w, name):
    return _dwconv_fwd_call(x, w, name=name + "_f"), (x, w)


def _dwconv_bwd(name, res, dy):
    x, w = res
    return _dwconv_bwd_call(x, w, dy, name=name + "_b")


dwconv.defvjp(_dwconv_fwd, _dwconv_bwd)


def _attn_tiles(seq):
    return _pick(seq, 2048), _pick(seq, 512)


def _lane_tile(v, width):
    return v if width == LANE else jnp.tile(v, (1, width // LANE))


def _causal_pairs(nq, nk, tq, tk, key_major):
    pairs = [(i, j) for i in range(nq) for j in range(nk) if j * tk <= i * tq + tq - 1]
    if key_major:
        pairs.sort(key=lambda p: (p[1], p[0]))
    return (jnp.asarray(np.array([p[0] for p in pairs], np.int32)),
            jnp.asarray(np.array([p[1] for p in pairs], np.int32)))


def _scores(q, k, i, j, tq, tk, masked):
    z = lax.dot_general(q, k, (((1,), (1,)), ((), ())), preferred_element_type=F32)
    if masked:
        row = i * tq + lax.broadcasted_iota(jnp.int32, (tq, tk), 0)
        col = j * tk + lax.broadcasted_iota(jnp.int32, (tq, tk), 1)
        z = jnp.where(col <= row, z, -jnp.inf)
    return z


def _attn_fwd_call(q, k, v, *, name):
    heads, seq, dh = q.shape
    tq, tk = _attn_tiles(seq)
    nq, nk = seq // tq, seq // tk
    qi, kj = _causal_pairs(nq, nk, tq, tk, key_major=False)

    def body(qi_ref, kj_ref, q_ref, k_ref, v_ref, o_ref, lse_ref, m_sc, acc_sc):
        t = pl.program_id(1)
        i = qi_ref[t]
        j = kj_ref[t]

        @pl.when(j == 0)
        def _():
            m_sc[...] = jnp.full(m_sc.shape, -jnp.inf, F32)
            acc_sc[...] = jnp.zeros_like(acc_sc)

        def step(masked):
            z = _scores(q_ref[0], k_ref[0], i, j, tq, tk, masked)
            m_prev = m_sc[...]
            m_new = jnp.maximum(m_prev, jnp.max(z, axis=1, keepdims=True))
            alpha = jnp.exp2(m_prev - m_new)
            p = jnp.exp2(z - _lane_tile(m_new, tk))
            acc_sc[...] = alpha * acc_sc[...] + jnp.dot(p.astype(MXU), v_ref[0], preferred_element_type=F32)
            m_sc[...] = m_new

        crosses_diagonal = j * tk + tk - 1 > i * tq

        @pl.when(jnp.logical_not(crosses_diagonal))
        def _():
            step(False)

        @pl.when(crosses_diagonal)
        def _():
            step(True)

        @pl.when(j == (i * tq + tq - 1) // tk)
        def _():
            acc = acc_sc[...]
            lane = lax.broadcasted_iota(jnp.int32, acc.shape, 1)
            den = jnp.sum(jnp.where(lane == DEN_COL, acc, 0.0), axis=1, keepdims=True)
            o_ref[0] = acc / den
            lse_ref[0] = m_sc[...] + jnp.log(den) * LOG2_E

    q_spec = pl.BlockSpec((1, tq, dh), lambda h, t, qi_ref, kj_ref: (h, qi_ref[t], 0))
    kv_spec = pl.BlockSpec((1, tk, dh), lambda h, t, qi_ref, kj_ref: (h, kj_ref[t], 0))
    return pl.pallas_call(
        body, name=name,
        grid_spec=pltpu.PrefetchScalarGridSpec(
            num_scalar_prefetch=2, grid=(heads, int(qi.shape[0])), in_specs=[q_spec, kv_spec, kv_spec],
            out_specs=[q_spec, q_spec],
            scratch_shapes=[pltpu.VMEM((tq, LANE), F32), pltpu.VMEM((tq, dh), F32)]),
        out_shape=[jax.ShapeDtypeStruct((heads, seq, dh), F32), jax.ShapeDtypeStruct((heads, seq, LANE), F32)],
        compiler_params=_params(2),
    )(qi, kj, q, k, v)


def _attn_bwd_call(q, k, v, o, lse, do, *, name):
    heads, seq, dh = q.shape
    tk, tq = _attn_tiles(seq)
    nq, nk = seq // tq, seq // tk
    qi, kj = _causal_pairs(nq, nk, tq, tk, key_major=True)

    def body(qi_ref, kj_ref, q_ref, k_ref, v_ref, o_ref, lse_ref, do_ref, dq_ref, dk_ref, dv_ref, dk_sc, dv_sc):
        t = pl.program_id(1)
        i = qi_ref[t]
        j = kj_ref[t]
        first_i = (j * tk) // tq

        @pl.when(t == 0)
        def _():
            dq_ref[...] = jnp.zeros_like(dq_ref)

        @pl.when(i == first_i)
        def _():
            dk_sc[...] = jnp.zeros_like(dk_sc)
            dv_sc[...] = jnp.zeros_like(dv_sc)

        def step(masked):
            qb = q_ref[0]
            kb = k_ref[0]
            do_f = do_ref[0]
            do_b = do_f.astype(MXU)
            p = jnp.exp2(_scores(qb, kb, i, j, tq, tk, masked) - _lane_tile(lse_ref[0], tk))
            dp = lax.dot_general(do_b, v_ref[0], (((1,), (1,)), ((), ())), preferred_element_type=F32)
            delta = jnp.sum(do_f * o_ref[0], axis=1, keepdims=True)
            ds = (p * (dp - delta)).astype(MXU)
            dv_sc[...] += lax.dot_general(p.astype(MXU), do_b, (((0,), (0,)), ((), ())), preferred_element_type=F32)
            dk_sc[...] += lax.dot_general(ds, qb, (((0,), (0,)), ((), ())), preferred_element_type=F32)
            rows = pl.ds(pl.multiple_of(i * tq, tq), tq)
            dq_ref[0, rows, :] += jnp.dot(ds, kb, preferred_element_type=F32) * ATTN_SCALE

        crosses_diagonal = j * tk + tk - 1 > i * tq

        @pl.when(jnp.logical_not(crosses_diagonal))
        def _():
            step(False)

        @pl.when(crosses_diagonal)
        def _():
            step(True)

        @pl.when(i == nq - 1)
        def _():
            dk_ref[0] = dk_sc[...] * (1.0 / LOG2_E)
            dv_ref[0] = dv_sc[...]

    def q_map(h, t, qi_ref, kj_ref):
        return (h, qi_ref[t], 0)

    def kv_map(h, t, qi_ref, kj_ref):
        return (h, kj_ref[t], 0)

    q_spec = pl.BlockSpec((1, tq, dh), q_map)
    lse_spec = pl.BlockSpec((1, tq, LANE), q_map)
    kv_spec = pl.BlockSpec((1, tk, dh), kv_map)
    head_spec = pl.BlockSpec((1, seq, dh), lambda h, t, qi_ref, kj_ref: (h, 0, 0))
    return pl.pallas_call(
        body, name=name,
        grid_spec=pltpu.PrefetchScalarGridSpec(
            num_scalar_prefetch=2, grid=(heads, int(qi.shape[0])),
            in_specs=[q_spec, kv_spec, kv_spec, q_spec, lse_spec, q_spec], out_specs=[head_spec, kv_spec, kv_spec],
            scratch_shapes=[pltpu.VMEM((tk, dh), F32), pltpu.VMEM((tk, dh), F32)]),
        out_shape=[jax.ShapeDtypeStruct((heads, seq, dh), F32)] * 3,
        compiler_params=_params(2),
    )(qi, kj, q, k, v, o, lse, do)


@functools.partial(jax.custom_vjp, nondiff_argnums=(3,))
def attention(q, k, v, name):
    return _attention_fwd(q, k, v, name)[0]


def _attention_fwd(q, k, v, name):
    qb, kb, vb = (q * (ATTN_SCALE * LOG2_E)).astype(MXU), k.astype(MXU), v.astype(MXU)
    o, lse = _attn_fwd_call(qb, kb, vb, name=name + "_f")
    return o, (qb, kb, vb, o, lse)


def _attention_bwd(name, res, do):
    qb, kb, vb, o, lse = res
    return tuple(_attn_bwd_call(qb, kb, vb, o, lse, do, name=name + "_b"))


attention.defvjp(_attention_fwd, _attention_bwd)


def _merge_tiles(rows):
    return _pick(rows, 128, 8)


def _gate_merge_fwd_call(gates, ys, *, name):
    rows, d = ys[0].shape
    nb = len(ys)
    tr = _merge_tiles(rows)

    def body(g_ref, *refs):
        o_ref = refs[nb]
        acc = jax.nn.sigmoid(g_ref[:, 0:d]) * refs[0][...]
        for b in range(1, nb):
            acc = acc + jax.nn.sigmoid(g_ref[:, b * d:(b + 1) * d]) * refs[b][...]
        o_ref[...] = acc

    wide = pl.BlockSpec((tr, nb * d), lambda i: (i, 0))
    row = pl.BlockSpec((tr, d), lambda i: (i, 0))
    return pl.pallas_call(
        body, name=name, grid=(rows // tr,), in_specs=[wide] + [row] * nb, out_specs=row,
        out_shape=jax.ShapeDtypeStruct((rows, d), F32), compiler_params=_params(1),
    )(gates, *ys)


def _gate_merge_bwd_call(gates, ys, dm, *, name):
    rows, d = ys[0].shape
    nb = len(ys)
    tr = _merge_tiles(rows)

    def body(g_ref, *refs):
        y_refs = refs[:nb]
        dm_ref = refs[nb]
        dg_ref = refs[nb + 1]
        dy_refs = refs[nb + 2:]
        dm = dm_ref[...]
        for b in range(nb):
            gt = jax.nn.sigmoid(g_ref[:, b * d:(b + 1) * d])
            dy_refs[b][...] = gt * dm
            dg_ref[:, b * d:(b + 1) * d] = dm * y_refs[b][...] * gt * (1.0 - gt)

    wide = pl.BlockSpec((tr, nb * d), lambda i: (i, 0))
    row = pl.BlockSpec((tr, d), lambda i: (i, 0))
    outs = pl.pallas_call(
        body, name=name, grid=(rows // tr,), in_specs=[wide] + [row] * (nb + 1), out_specs=[wide] + [row] * nb,
        out_shape=[jax.ShapeDtypeStruct((rows, nb * d), F32)] + [jax.ShapeDtypeStruct((rows, d), F32)] * nb,
        compiler_params=_params(1),
    )(gates, *ys, dm)
    return outs[0], tuple(outs[1:])


@functools.partial(jax.custom_vjp, nondiff_argnums=(2,))
def gate_merge(gates, ys, name):
    return _gate_merge_fwd_call(gates, ys, name=name + "_f")


def _gate_merge_fwd(gates, ys, name):
    return _gate_merge_fwd_call(gates, ys, name=name + "_f"), (gates, ys)


def _gate_merge_bwd(name, res, dm):
    gates, ys = res
    return _gate_merge_bwd_call(gates, ys, dm, name=name + "_b")


gate_merge.defvjp(_gate_merge_fwd, _gate_merge_bwd)


def _loss_call(y, target):
    rows, d = y.shape
    tr = _pick(rows, 512, 8)

    def body(y_ref, t_ref, dy_ref, loss_ref):
        @pl.when(pl.program_id(0) == 0)
        def _():
            loss_ref[...] = jnp.zeros_like(loss_ref)

        err = y_ref[...] - t_ref[...]
        dy_ref[...] = err * (1.0 / d)
        loss_ref[...] += jnp.sum(err * err, axis=0, keepdims=True)

    row = pl.BlockSpec((tr, d), lambda i: (i, 0))
    vec = pl.BlockSpec((1, d), lambda i: (0, 0))
    dy, part = pl.pallas_call(
        body, name="loss_head", grid=(rows // tr,), in_specs=[row, row], out_specs=[row, vec],
        out_shape=[jax.ShapeDtypeStruct((rows, d), F32), jax.ShapeDtypeStruct((1, d), F32)],
        compiler_params=_params(1),
    )(y, target)
    return 0.5 * jnp.sum(part) / d, dy


ADAMW_BLOCK_ELEMS = 256 * 1024


def _adamw_call(w, g, m, v, *, name):
    shape = w.shape
    width = shape[-1]
    rows = w.size // width
    w, g, m, v = (a.reshape(rows, width) for a in (w, g, m, v))
    tr = _pick(rows, max(8, ADAMW_BLOCK_ELEMS // width), 8)
    c1 = 1.0 - ADAM_B1 ** ADAM_STEP
    c2 = 1.0 - ADAM_B2 ** ADAM_STEP

    def body(w_ref, g_ref, m_ref, v_ref, d_ref, nm_ref, nv_ref):
        g = g_ref[...]
        m = ADAM_B1 * m_ref[...] + (1.0 - ADAM_B1) * g
        v = ADAM_B2 * v_ref[...] + (1.0 - ADAM_B2) * (g * g)
        m_hat = m / c1
        v_hat = v / c2
        d_ref[...] = -ADAM_LR * (m_hat / (jnp.sqrt(v_hat) + ADAM_EPS) + ADAM_WD * w_ref[...])
        nm_ref[...] = m
        nv_ref[...] = v

    row = pl.BlockSpec((tr, width), lambda i: (i, 0))
    outs = pl.pallas_call(
        body, name=name, grid=(rows // tr,), in_specs=[row] * 4, out_specs=[row] * 3,
        out_shape=[jax.ShapeDtypeStruct((rows, width), F32)] * 3, compiler_params=_params(1),
    )(w, g, m, v)
    return [o.reshape(shape) for o in outs]


def _ordered_sum(y, *, name):
    n, rows, width = y.shape
    tr = _pick(rows, 256, 8)

    def body(y_ref, o_ref):
        acc = y_ref[0]
        for s in range(1, n):
            acc = acc + y_ref[s]
        o_ref[...] = acc

    return pl.pallas_call(
        body, name=name, grid=(rows // tr,), in_specs=[pl.BlockSpec((n, tr, width), lambda i: (0, i, 0))],
        out_specs=pl.BlockSpec((tr, width), lambda i: (i, 0)), out_shape=jax.ShapeDtypeStruct((rows, width), y.dtype),
        compiler_params=_params(1),
    )(y)


_GROUP_FLIPS = {
    "chips": ((1, 0, 0), (0, 1, 0), (1, 1, 0)),
    "cores": ((0, 0, 1),),
    "all": ((0, 0, 1), (0, 1, 0), (0, 1, 1), (1, 0, 0), (1, 0, 1), (1, 1, 0), (1, 1, 1)),
}


def _exchange(x, *, group, broadcast, name):
    flips = _GROUP_FLIPS[group]
    n = len(flips) + 1
    block = x.shape if broadcast else x.shape[1:]
    if not broadcast:
        assert x.shape[0] == n

    def body(x_ref, o_ref, send_sems, recv_sems, local_sem):
        mx, my, mc = lax.axis_index("x"), lax.axis_index("y"), lax.axis_index("c")

        def index(px, py, pc):
            return {"chips": 2 * px + py, "cores": pc, "all": 4 * px + 2 * py + pc}[group]

        def block_for(d):
            return x_ref if broadcast else x_ref.at[d]

        me = index(mx, my, mc)
        mine = pltpu.make_async_copy(block_for(me), o_ref.at[me], local_sem)
        mine.start()
        sends, recvs = [], []
        for k, (fx, fy, fc) in enumerate(flips):
            px = 1 - mx if fx else mx
            py = 1 - my if fy else my
            pc = 1 - mc if fc else mc
            peer = index(px, py, pc)
            sends.append(pltpu.make_async_remote_copy(
                src_ref=block_for(peer), dst_ref=o_ref.at[me], send_sem=send_sems.at[k], recv_sem=recv_sems.at[k],
                device_id=(px, py, pc), device_id_type=MESH_T))
            recvs.append(pltpu.make_async_remote_copy(
                src_ref=block_for(peer), dst_ref=o_ref.at[peer], send_sem=send_sems.at[k], recv_sem=recv_sems.at[k],
                device_id=(px, py, pc), device_id_type=MESH_T))
        for cp in sends:
            cp.start()
        for cp in recvs:
            cp.wait_recv()
        for cp in sends:
            cp.wait_send()
        mine.wait()

    any_spec = pl.BlockSpec(memory_space=pl.ANY)
    return pl.pallas_call(
        body, name=name, in_specs=[any_spec], out_specs=any_spec,
        out_shape=jax.ShapeDtypeStruct((n,) + tuple(block), x.dtype),
        scratch_shapes=[pltpu.SemaphoreType.DMA((n - 1,)), pltpu.SemaphoreType.DMA((n - 1,)), pltpu.SemaphoreType.DMA],
    )(x)


def _sibling():
    return (lax.axis_index("x"), lax.axis_index("y"), 1 - lax.axis_index("c"))


def _run_copies(copies):
    for cp in copies:
        cp.start()
    for cp in copies:
        cp.wait_recv()
    for cp in copies:
        cp.wait_send()


def _dma_call(body, arrays, out_shapes, n_copies, *, name, extra_scratch=()):
    any_spec = pl.BlockSpec(memory_space=pl.ANY)
    return pl.pallas_call(
        body, name=name, in_specs=[any_spec] * len(arrays), out_specs=[any_spec] * len(out_shapes),
        out_shape=out_shapes,
        scratch_shapes=[pltpu.SemaphoreType.DMA((n_copies,)), pltpu.SemaphoreType.DMA((n_copies,))]
        + list(extra_scratch),
    )(*arrays)


def _gather_shards(shards, *, name):
    flips = _GROUP_FLIPS["chips"]
    n = len(shards)
    n_copies = n * len(flips)

    def body(*refs):
        ins, outs = refs[:n], refs[n:2 * n]
        send_sems, recv_sems, pass_send_sems, pass_recv_sems, local_sems = refs[2 * n:]
        mx, my, mc = lax.axis_index("x"), lax.axis_index("y"), lax.axis_index("c")
        me = 2 * mx + my
        local = [pltpu.make_async_copy(ins[t], outs[t].at[me], local_sems.at[t]) for t in range(n)]
        for cp in local:
            cp.start()
        sends, arrivals, passes, passed = [], [], [], []
        for t in range(n):
            half = shards[t].shape[0] // 2
            mine = pl.ds(mc * half, half)
            theirs = pl.ds((1 - mc) * half, half)
            for k, (fx, fy, _) in enumerate(flips):
                px = 1 - mx if fx else mx
                py = 1 - my if fy else my
                peer = 2 * px + py
                idx = t * len(flips) + k
                ici = dict(send_sem=send_sems.at[idx], recv_sem=recv_sems.at[idx], device_id=(px, py, mc),
                           device_id_type=MESH_T)
                d2d = dict(send_sem=pass_send_sems.at[idx], recv_sem=pass_recv_sems.at[idx], device_id=_sibling(),
                           device_id_type=MESH_T)
                sends.append(pltpu.make_async_remote_copy(
                    src_ref=ins[t].at[mine], dst_ref=outs[t].at[me, mine], **ici))
                arrivals.append(pltpu.make_async_remote_copy(
                    src_ref=ins[t].at[mine], dst_ref=outs[t].at[peer, mine], **ici))
                passes.append(pltpu.make_async_remote_copy(
                    src_ref=outs[t].at[peer, mine], dst_ref=outs[t].at[peer, mine], **d2d))
                passed.append(pltpu.make_async_remote_copy(
                    src_ref=outs[t].at[peer, theirs], dst_ref=outs[t].at[peer, theirs], **d2d))
        for cp in sends:
            cp.start()
        for arrived, onward in zip(arrivals, passes):
            arrived.wait_recv()
            onward.start()
        for cp in passed:
            cp.wait_recv()
        for cp in sends + passes:
            cp.wait_send()
        for cp in local:
            cp.wait()

    return _dma_call(body, shards, [jax.ShapeDtypeStruct((4,) + a.shape, a.dtype) for a in shards], n_copies,
                     name=name, extra_scratch=[pltpu.SemaphoreType.DMA((n_copies,)), pltpu.SemaphoreType.DMA((n_copies,)),
                                               pltpu.SemaphoreType.DMA((n,))])


def _pair_send(gs, *, name):
    n = len(gs)

    def body(*refs):
        ins, outs = refs[:n], refs[n:2 * n]
        send_sems, recv_sems = refs[2 * n:]
        copies = []
        for t in range(n):
            half = gs[t].shape[1] // 2
            theirs = (1 - lax.axis_index("c")) * half
            for s in range(4):
                copies.append(pltpu.make_async_remote_copy(
                    src_ref=ins[t].at[s, pl.ds(theirs, half), :], dst_ref=outs[t].at[s],
                    send_sem=send_sems.at[4 * t + s], recv_sem=recv_sems.at[4 * t + s],
                    device_id=_sibling(), device_id_type=MESH_T))
        _run_copies(copies)

    return _dma_call(body, gs, [jax.ShapeDtypeStruct((4, a.shape[1] // 2, a.shape[2]), a.dtype) for a in gs], 4 * n,
                     name=name)


def _quad_send(ps, *, name):
    flips = _GROUP_FLIPS["chips"]
    n = len(ps)

    def body(*refs):
        ins, outs = refs[:n], refs[n:2 * n]
        send_sems, recv_sems = refs[2 * n:]
        mx, my, mc = lax.axis_index("x"), lax.axis_index("y"), lax.axis_index("c")
        copies = []
        for t in range(n):
            for k, (fx, fy, _) in enumerate(flips):
                px = 1 - mx if fx else mx
                py = 1 - my if fy else my
                copies.append(pltpu.make_async_remote_copy(
                    src_ref=ins[t].at[2 * px + py], dst_ref=outs[t].at[k], send_sem=send_sems.at[3 * t + k],
                    recv_sem=recv_sems.at[3 * t + k], device_id=(px, py, mc), device_id_type=MESH_T))
        _run_copies(copies)

    return _dma_call(body, ps, [jax.ShapeDtypeStruct((3,) + a.shape[1:], a.dtype) for a in ps], 3 * n, name=name)


def _sibling_send(ms, *, name):
    n = len(ms)

    def body(*refs):
        ins, outs = refs[:n], refs[n:2 * n]
        send_sems, recv_sems = refs[2 * n:]
        _run_copies([pltpu.make_async_remote_copy(
            src_ref=ins[t], dst_ref=outs[t], send_sem=send_sems.at[t], recv_sem=recv_sems.at[t],
            device_id=_sibling(), device_id_type=MESH_T) for t in range(n)])

    return _dma_call(body, ms, [jax.ShapeDtypeStruct(a.shape, a.dtype) for a in ms], n, name=name)


SUM_BLOCK_ELEMS = 256 * 1024


def _sum_tile(rows, width):
    return _pick(rows, max(16, SUM_BLOCK_ELEMS // width), 16)


def _pair_sum(g, got, core, *, name):
    _, rows, width = g.shape
    half = rows // 2
    tr = _sum_tile(half, width)
    tiles = half // tr

    def body(core_ref, g_ref, got_ref, o_ref):
        o_ref[...] = (g_ref[...] + got_ref[...]).astype(BF16)

    blk = pl.BlockSpec((1, tr, width), lambda s, i, core_ref: (s, i, 0))
    return pl.pallas_call(
        body, name=name,
        grid_spec=pltpu.PrefetchScalarGridSpec(
            num_scalar_prefetch=1, grid=(4, tiles),
            in_specs=[pl.BlockSpec((1, tr, width), lambda s, i, core_ref: (s, core_ref[0] * tiles + i, 0)), blk],
            out_specs=blk),
        out_shape=jax.ShapeDtypeStruct((4, half, width), BF16), compiler_params=_params(2),
    )(core, g, got)


def _quad_sum(pair, others, chip, *, name):
    _, rows, width = pair.shape
    tr = _sum_tile(rows, width)

    def body(chip_ref, p_ref, o3_ref, o_ref):
        acc = p_ref[0].astype(F32)
        for k in range(3):
            acc = acc + o3_ref[k].astype(F32)
        o_ref[...] = acc

    return pl.pallas_call(
        body, name=name,
        grid_spec=pltpu.PrefetchScalarGridSpec(
            num_scalar_prefetch=1, grid=(rows // tr,),
            in_specs=[pl.BlockSpec((1, tr, width), lambda i, chip_ref: (chip_ref[0], i, 0)),
                      pl.BlockSpec((3, tr, width), lambda i, chip_ref: (0, i, 0))],
            out_specs=pl.BlockSpec((tr, width), lambda i, chip_ref: (i, 0))),
        out_shape=jax.ShapeDtypeStruct((rows, width), F32), compiler_params=_params(1),
    )(chip, pair, others)


def _reduce_scatter(gs, *, tag):
    mc = lax.axis_index("c")
    core = mc.reshape(1).astype(jnp.int32)
    chip = (2 * lax.axis_index("x") + lax.axis_index("y")).reshape(1).astype(jnp.int32)
    got = _pair_send(gs, name=tag + "_pair")
    pairs = [_pair_sum(g, r, core, name=tag + "_pair_sum") for g, r in zip(gs, got)]
    others = _quad_send(pairs, name=tag + "_quad")
    mine = [_quad_sum(p, o, chip, name=tag + "_quad_sum") for p, o in zip(pairs, others)]
    theirs = _sibling_send(mine, name=tag + "_share")
    return [jnp.where(mc == 0, jnp.concatenate([m, t], axis=0), jnp.concatenate([t, m], axis=0))
            for m, t in zip(mine, theirs)]


def _round_up(n, q):
    return -(-n // q) * q


def _pack_rows(flat, quantum_rows):
    n = flat.shape[-1]
    total = _round_up(n, PACK_W * quantum_rows)
    pad = [(0, 0)] * (flat.ndim - 1) + [(0, total - n)]
    return jnp.pad(flat, pad).reshape(flat.shape[:-1] + (total // PACK_W, PACK_W))


def _split_shards(full, axis):
    if axis == 0:
        return full.reshape((4, full.shape[0] // 4, full.shape[1]))
    width = full.shape[1] // 4
    return jnp.stack([full[:, s * width:(s + 1) * width] for s in range(4)])


def _join_shards(blocks, axis):
    if axis == 0:
        return blocks.reshape((4 * blocks.shape[1], blocks.shape[2]))
    return jnp.concatenate([blocks[s] for s in range(4)], axis=1)


def _pack_small(tree, names):
    return _pack_rows(jnp.concatenate([tree[name].reshape(-1) for name in names]), 8)


def _unpack_small(packed, shapes, names):
    lead = packed.shape[:-2]
    flat = packed.reshape(lead + (-1,))
    out, off = {}, 0
    for name in names:
        n = int(np.prod(shapes[name]))
        out[name] = flat[..., off:off + n].reshape(lead + tuple(shapes[name]))
        off += n
    return out


def _rope(t, cos, sin):
    half = QK_ROPE // 2
    t1, t2 = t[..., :half], t[..., half:]
    return jnp.concatenate([t1 * cos - t2 * sin, t2 * cos + t1 * sin], axis=-1)


def _pad_in_cols(a):
    z = jnp.zeros(a.shape[:-1] + (IN_PAD,), a.dtype)
    return jnp.concatenate([a[..., :IN_PAD_AT], z, a[..., IN_PAD_AT:]], axis=-1)


def _pool_constants(seq):
    taps = max(POOL_WINDOWS)
    win = jnp.repeat(jnp.asarray(POOL_WINDOWS, jnp.int32), 512 // len(POOL_WINDOWS))
    lag = taps - 1 - jnp.arange(taps, dtype=jnp.int32)
    mask = (lag[:, None] < win[None, :]).astype(F32)
    cnt = jnp.minimum(jnp.arange(seq, dtype=jnp.int32)[:, None] + 1, win[None, :]).astype(F32)
    return mask, 1.0 / cnt


def _block_diag(w):
    g, n, _ = w.shape
    out = jnp.zeros((g * n, g * n), w.dtype)
    for i in range(g):
        out = lax.dynamic_update_slice(out, w[i], (i * n, i * n))
    return out


def _layer(x, c_act, p, wb, cos, sin, pool_mask, pool_inv_cnt, tag):
    seq = x.shape[0]
    d = D_MODEL
    mod = linear_bias(c_act, p["w_ada"], wb["w_ada"], p["b_ada"], tag + "ada")[0]
    sh1, sc1, g1, sh2, sc2, g2 = [mod[i * d:(i + 1) * d][None, :] for i in range(6)]

    h = x * (1.0 + sc1) + sh1
    w_slot, w_in, b_in = _pad_in_cols(p["w_in"]), _pad_in_cols(wb["w_in"]), _pad_in_cols(p["b_in"])
    proj = linear_bias(h, w_slot[:, :IN_GATES_AT], w_in[:, :IN_GATES_AT], b_in[:IN_GATES_AT], tag + "in")
    gates = linear_bias(h, w_slot[:, IN_GATES_AT:], w_in[:, IN_GATES_AT:], b_in[IN_GATES_AT:], tag + "in_gates")
    conv_a, conv_b, sc_bg, sc_cg, sc_x, q_lat, kv_lat, k_rope, pool_u = split_cols(proj, IN_SPLITS)

    ya = conv_a * jax.nn.sigmoid(conv_b)
    ya = dwconv(ya, p["conv_dw"], tag + "convA")
    ya = jax.nn.silu(layer_norm(ya, p["conv_ln_g"], p["conv_ln_b"], tag + "convA_ln"))
    ya = linear(ya, p["w_conv_out"], wb["w_conv_out"], tag + "convA_out")

    yb = linear(sc_bg * dwconv(sc_cg * sc_x, p["sc_dw"], tag + "sc"), p["w_sc_out"], wb["w_sc_out"],
                tag + "sc_out")

    q = linear(rms_norm(q_lat, p["q_norm_g"], tag + "q_rms"), p["w_uq"], wb["w_uq"], tag + "uq")
    q = q.reshape(seq, N_HEADS, QK_NOPE + QK_ROPE)
    kv = linear(rms_norm(kv_lat, p["kv_norm_g"], tag + "kv_rms"), p["w_ukv"], wb["w_ukv"], tag + "ukv")
    kv = kv.reshape(seq, N_HEADS, QK_NOPE + V_DIM)
    q_rope = _rope(q[..., QK_NOPE:], cos[:, None, :], sin[:, None, :])
    k_rope_r = jnp.broadcast_to(_rope(k_rope, cos, sin)[:, None, :], (seq, N_HEADS, QK_ROPE))
    zq = jnp.zeros((seq, N_HEADS, HEAD_PAD - QK_NOPE - QK_ROPE), F32)
    zv = jnp.zeros((seq, N_HEADS, HEAD_PAD - V_DIM), F32).at[:, :, DEN_COL - V_DIM].set(1.0)
    qh = jnp.concatenate([q[..., :QK_NOPE], q_rope, zq], axis=-1).transpose(1, 0, 2)
    kh = jnp.concatenate([kv[..., :QK_NOPE], k_rope_r, zq], axis=-1).transpose(1, 0, 2)
    vh = jnp.concatenate([kv[..., QK_NOPE:], zv], axis=-1).transpose(1, 0, 2)
    att = attention(qh, kh, vh, tag + "attn")
    att = att[:, :, :V_DIM].transpose(1, 0, 2).reshape(seq, N_HEADS * V_DIM)
    yc = linear(att, p["w_mla_out"], wb["w_mla_out"], tag + "mla_out")

    pd = dwconv(pool_u, pool_mask, tag + "pool") * pool_inv_cnt - pool_u
    w_mix = _block_diag(p["w_pool"])
    yd = linear(pd, w_mix, w_mix, tag + "pool_mix") * p["pool_scale"][None, :]
    yd = linear(yd, p["w_pool_out"], wb["w_pool_out"], tag + "pool_out")

    merged = gate_merge(gates, (ya, yb, yc, yd), tag + "merge")
    mix = linear(merged, p["w_o"], wb["w_o"], tag + "o")
    x = layer_norm(ALPHA * x + (1.0 + g1) * mix, p["ln1_g"], p["ln1_b"], tag + "ln1")

    h = x * (1.0 + sc2) + sh2
    d_ff = p["w_up"].shape[1] // 2
    val = dwconv(linear(h, p["w_up"][:, :d_ff], wb["w_up"][:, :d_ff], tag + "up"), p["ffn_dw"][:, :d_ff],
                 tag + "ffn_conv")
    gate = dwconv(linear(h, p["w_up"][:, d_ff:], wb["w_up"][:, d_ff:], tag + "up"), p["ffn_dw"][:, d_ff:],
                  tag + "ffn_conv")
    ffn = linear(jax.nn.silu(gate) * val, p["w_down"], wb["w_down"], tag + "down")
    return layer_norm(ALPHA * x + (1.0 + g2) * ffn, p["ln2_g"], p["ln2_b"], tag + "ln2")


def _forward(x, layers, gathered, c_act, cos, sin, pool_mask, pool_inv_cnt):
    for p, wb in zip(layers, gathered):
        x = _layer(x, c_act, p, wb, cos, sin, pool_mask, pool_inv_cnt, "")
    return x


def kernel(x, c, positions, w_ada, b_ada, w_in, b_in, conv_dw, conv_ln_g, conv_ln_b, w_conv_out, sc_dw, w_sc_out, q_norm_g, w_uq, kv_norm_g, w_ukv, w_mla_out, w_pool, pool_scale, w_pool_out, w_o, ln1_g, ln1_b, w_up, ffn_dw, w_down, ln2_g, ln2_b, loss_target, m_w_ada, m_b_ada, m_w_in, m_b_in, m_conv_dw, m_conv_ln_g, m_conv_ln_b, m_w_conv_out, m_sc_dw, m_w_sc_out, m_q_norm_g, m_w_uq, m_kv_norm_g, m_w_ukv, m_w_mla_out, m_w_pool, m_pool_scale, m_w_pool_out, m_w_o, m_ln1_g, m_ln1_b, m_w_up, m_ffn_dw, m_w_down, m_ln2_g, m_ln2_b, v_w_ada, v_b_ada, v_w_in, v_b_in, v_conv_dw, v_conv_ln_g, v_conv_ln_b, v_w_conv_out, v_sc_dw, v_w_sc_out, v_q_norm_g, v_w_uq, v_kv_norm_g, v_w_ukv, v_w_mla_out, v_w_pool, v_pool_scale, v_w_pool_out, v_w_o, v_ln1_g, v_ln1_b, v_w_up, v_ffn_dw, v_w_down, v_ln2_g, v_ln2_b):
    given = dict(locals())
    weights = {n: given[n] for n in WEIGHT_ORDER}
    mom_m = {n: given["m_" + n] for n in WEIGHT_ORDER}
    mom_v = {n: given["v_" + n] for n in WEIGHT_ORDER}
    depth = w_ada.shape[0]
    seq = x.shape[1]
    big = [(s[0], s[2]) for s in SHARDED if s[3]]
    tiny = [s[0] for s in SHARDED if not s[3]]
    chip = 2 * lax.axis_index("x") + lax.axis_index("y")

    tiny_shapes = {n: weights[n].shape for n in tiny}
    tiny_all = _exchange(_pack_small(weights, tiny), group="chips", broadcast=True, name="gather_taps")
    tiny_full = {n: jnp.concatenate([a[s] for s in range(4)], axis=-1)
                 for n, a in _unpack_small(tiny_all, tiny_shapes, tiny).items()}
    layers, gathered = [], []
    for l in range(depth):
        blocks = _gather_shards([weights[n][l].astype(BF16) for n, _ in big], name="gather_weights")
        gathered.append({n: _join_shards(a, axis) for (n, axis), a in zip(big, blocks)})
        p = {n: jnp.zeros(gathered[l][n].shape, F32) for n, _ in big}
        for n in tiny:
            p[n] = tiny_full[n][l]
        for n in REPLICATED:
            p[n] = weights[n][l]
        layers.append(p)

    inv = 1.0 / (ROPE_THETA ** (jnp.arange(0, QK_ROPE, 2, dtype=F32) / QK_ROPE))
    ang = positions[0].astype(F32)[:, None] * inv
    cos, sin = jnp.cos(ang), jnp.sin(ang)
    c_act = jnp.pad(jax.nn.silu(c), ((0, 15), (0, 0)))
    pool_mask, pool_inv_cnt = _pool_constants(seq)

    y, vjp_fn = jax.vjp(lambda xx, ll: _forward(xx, ll, gathered, c_act, cos, sin, pool_mask, pool_inv_cnt),
                        x[0], layers)
    loss_local, dy = _loss_call(y, loss_target[0])
    grad_x, grad_layers = vjp_fn(dy)
    loss = lax.psum(loss_local, ("x", "y", "c"))

    outs = {"grad": {}, "delta": {}, "m": {}, "v": {}}
    reduced = [_reduce_scatter([_split_shards(grad_layers[l][n], axis) for n, axis in big], tag="rs")
               for l in range(depth)]
    for t, (n, _) in enumerate(big):
        outs["grad"][n] = jnp.stack([reduced[l][t] for l in range(depth)])
    small = list(REPLICATED) + tiny
    small_local = {n: jnp.stack([grad_layers[l][n] for l in range(depth)]) for n in small}
    small_shapes = {n: small_local[n].shape for n in small}
    small_all = _exchange(_pack_small(small_local, small), group="all", broadcast=True, name="gather_small_grads")
    small_sum = _unpack_small(_ordered_sum(small_all, name="small_grads_sum"), small_shapes, small)
    for n in REPLICATED:
        outs["grad"][n] = small_sum[n]
    for n in tiny:
        width = weights[n].shape[-1]
        outs["grad"][n] = lax.dynamic_slice_in_dim(small_sum[n], chip * width, width, axis=2)

    for n in WEIGHT_ORDER:
        outs["delta"][n], outs["m"][n], outs["v"][n] = _adamw_call(
            weights[n], outs["grad"][n], mom_m[n], mom_v[n], name="adamw_" + n)

    result = [loss, grad_x[None]]
    for key in ("grad", "delta", "m", "v"):
        result.extend(outs[key][n] for n in WEIGHT_ORDER)
    return tuple(result)
```
